```python
import jax
import jax.numpy as jnp
from jax import lax
import numpy as np

D_MODEL = 1024
BATCH = 1
SEQ = 16384
DEPTH = 1
DEC_BATCH = 8
DEC_SEQ = 32
PAST_LEN = 2048

CHUNK = 64
EPS = 1e-6
M_HEADS = 4
M_DIM = D_MODEL
M_HD = M_DIM // M_HEADS
CONV_W = 4
G_HEADS = 4
G_KDIM = D_MODEL // 2
G_VDIM = D_MODEL
G_DK = G_KDIM // G_HEADS
G_DV = G_VDIM // G_HEADS
G_RANK = 16
G_TAU = 16.0
N_EXPERTS = 32
TOP_K = 4
D_FF = D_MODEL
SWIGLU_LIMIT = 7.0
SWIGLU_ALPHA = 1.702
MOE_BLOCK = 128
PLE_DIM = 256
SPLITS = (M_DIM, M_DIM, M_DIM, M_HEADS, M_HEADS, M_DIM, G_KDIM, G_KDIM, G_VDIM, G_RANK, G_VDIM, D_MODEL, D_MODEL)
IN_COLS = sum(SPLITS)
I_GATE_SEG = 3
F_GATE_SEG = 4

kernel_name = 'hybrid_mlstm_gla_moe_stream_step'


def rmsnorm(x, g):
    xf = x.astype(jnp.float32)
    xf = xf * lax.rsqrt(jnp.mean(xf * xf, axis=-1, keepdims=True) + EPS)
    return (xf * g.astype(jnp.float32)).astype(x.dtype)


def to_heads(a, n_heads):
    b, t, _ = a.shape
    return a.reshape(b, t, n_heads, -1).transpose(0, 2, 1, 3)


def head_rmsnorm(hd, g):
    b, nh, t, d = hd.shape
    hd = hd * lax.rsqrt(jnp.mean(hd * hd, axis=-1, keepdims=True) + EPS)
    hd = hd * g.astype(jnp.float32).reshape(nh, 1, d)
    return hd.transpose(0, 2, 1, 3).reshape(b, t, nh * d)


def to_chunks(a, L):
    b, nh, t = a.shape[:3]
    a = a.reshape(b, nh, t // L, L, *a.shape[3:])
    return jnp.moveaxis(a, 2, 0)


def from_chunks(a):
    a = jnp.moveaxis(a, 0, 2)
    return a.reshape(a.shape[0], a.shape[1], -1, *a.shape[4:])


def tril(L):
    return jnp.tril(jnp.ones((L, L), dtype=bool))


def causal_dwconv(u, buf, w, b):
    full = jnp.concatenate([buf.astype(u.dtype), u], axis=1)
    out = lax.conv_general_dilated(full, w.astype(u.dtype)[:, None, :], window_strides=(1,), padding='VALID',
                                   dimension_numbers=('NWC', 'WIO', 'NWC'), feature_group_count=u.shape[-1])
    return out + b.astype(u.dtype), full[:, full.shape[1] - (CONV_W - 1):]


def mlstm_chunk(carry, xs):
    C, n, m = carry
    q, k, v, ig, lf = xs
    L = q.shape[2]
    b = jnp.cumsum(lf, axis=-1)
    logD = jnp.where(tril(L), b[..., :, None] - b[..., None, :] + ig[..., None, :], -jnp.inf)
    inter = b + m[..., None]
    m_t = jnp.maximum(inter, jnp.max(logD, axis=-1))
    s = jnp.einsum('bhtd,bhsd->bhts', q, k) * jnp.exp(logD - m_t[..., None])
    w_int = jnp.exp(inter - m_t)
    num = jnp.einsum('bhts,bhse->bhte', s, v) + w_int[..., None] * jnp.einsum('bhtd,bhde->bhte', q, C)
    den = jnp.sum(s, axis=-1) + w_int * jnp.einsum('bhtd,bhd->bht', q, n)
    h = num / jnp.maximum(jnp.abs(den), jnp.exp(-m_t))[..., None]
    m_new = m_t[..., L - 1]
    w_k = jnp.exp(b[..., L - 1:] - b + ig - m_new[..., None])
    dec = jnp.exp(b[..., L - 1] + m - m_new)
    C_new = dec[..., None, None] * C + jnp.einsum('bhs,bhsd,bhse->bhde', w_k, k, v)
    n_new = dec[..., None] * n + jnp.einsum('bhs,bhsd->bhd', w_k, k)
    return (C_new, n_new, m_new), h


def gla_chunk(S, xs):
    q, k, v, la = xs
    L = q.shape[2]
    cb = jnp.cumsum(la, axis=2)
    diff = cb[:, :, :, None, :] - cb[:, :, None, :, :]
    decay = jnp.exp(jnp.where(tril(L)[:, :, None], diff, -jnp.inf))
    att = jnp.einsum('bhtd,bhtsd,bhsd->bhts', q, decay, k)
    o = jnp.einsum('bhts,bhse->bhte', att, v) + jnp.einsum('bhtd,bhde->bhte', q * jnp.exp(cb), S)
    last = cb[:, :, L - 1:, :]
    S_new = jnp.exp(last[:, :, 0, :])[..., None] * S + jnp.einsum('bhsd,bhse->bhde', k * jnp.exp(last - cb), v)
    return S_new, o


def token_mixers(h, conv_st, C0, n0, m0, S0, w_in, b_in, conv_w, conv_b, g_mnorm, w_a2, b_a2, g_gnorm,
                 w_pa, w_pb, w_out):
    f32 = jnp.float32
    T = h.shape[1]
    L = min(CHUNK, T)
    z = h @ w_in + b_in
    idx = np.cumsum(SPLITS)[:-1].tolist()
    mq, mk, mv, mi, mf, mo, gq, gk, gv, ga, gg, ua, ub = jnp.split(z, idx, axis=-1)
    qk, conv_new = causal_dwconv(jnp.concatenate([mq, mk], axis=-1), conv_st, conv_w, conv_b)
    qk = jax.nn.silu(qk)
    q_m = to_heads(qk[..., :M_DIM], M_HEADS).astype(f32)
    k_m = to_heads(qk[..., M_DIM:], M_HEADS).astype(f32) * (M_HD ** -0.5)
    v_m = to_heads(mv, M_HEADS).astype(f32)
    ig = jnp.swapaxes(mi.astype(f32), 1, 2)
    lf = jax.nn.log_sigmoid(jnp.swapaxes(mf.astype(f32), 1, 2))
    (C1, n1, m1), hm = lax.scan(mlstm_chunk, (C0.astype(f32), n0.astype(f32), m0.astype(f32)),
                                (to_chunks(q_m, L), to_chunks(k_m, L), to_chunks(v_m, L),
                                 to_chunks(ig, L), to_chunks(lf, L)))
    ya = head_rmsnorm(from_chunks(hm), g_mnorm) * jax.nn.sigmoid(mo.astype(f32))
    ya = ya.astype(h.dtype) @ w_pa
    q_g = to_heads(gq, G_HEADS).astype(f32) * (G_DK ** -0.5)
    k_g = to_heads(gk, G_HEADS).astype(f32)
    v_g = to_heads(gv, G_HEADS).astype(f32)
    la = to_heads(jax.nn.log_sigmoid((ga @ w_a2 + b_a2).astype(f32)) / G_TAU, G_HEADS)
    S1, og = lax.scan(gla_chunk, S0.astype(f32),
                      (to_chunks(q_g, L), to_chunks(k_g, L), to_chunks(v_g, L), to_chunks(la, L)))
    yb = head_rmsnorm(from_chunks(og), g_gnorm) * jax.nn.silu(gg.astype(f32))
    yb = yb.astype(h.dtype) @ w_pb
    y = (jax.nn.sigmoid(ua) * ya + jax.nn.sigmoid(ub) * yb) @ w_out
    return y, (conv_new, C1, n1, m1, S1)


def moe(xn, w_router, b_router, w_gate_up, b_gate_up, w_down, b_down):
    bsz, t, d = xn.shape
    xf = xn.reshape(-1, d)
    N = xf.shape[0]
    NK = N * TOP_K
    logits = (xf @ w_router).astype(jnp.float32) + b_router.astype(jnp.float32)
    top_v, top_e = lax.top_k(logits, TOP_K)
    gate_w = jax.nn.softmax(top_v, axis=-1)
    flat_e = top_e.reshape(-1)
    flat_w = gate_w.reshape(-1)
    flat_tok = jnp.arange(NK, dtype=jnp.int32) // TOP_K
    order = jnp.argsort(flat_e)
    se = flat_e[order]
    counts = jnp.bincount(flat_e, length=N_EXPERTS)
    padded = (counts + MOE_BLOCK - 1) // MOE_BLOCK * MOE_BLOCK
    pend = jnp.cumsum(padded)
    pstart = pend - padded
    gstart = jnp.cumsum(counts) - counts
    dest = pstart[se] + jnp.arange(NK) - gstart[se]
    n_blocks = (NK + N_EXPERTS * (MOE_BLOCK - 1) + MOE_BLOCK - 1) // MOE_BLOCK
    P = n_blocks * MOE_BLOCK
    buf_tok = jnp.zeros((P,), jnp.int32).at[dest].set(flat_tok[order])
    buf_w = jnp.zeros((P,), jnp.float32).at[dest].set(flat_w[order])
    block_e = jnp.minimum(jnp.searchsorted(pend, jnp.arange(n_blocks) * MOE_BLOCK, side='right'), N_EXPERTS - 1)
    xb = xf[buf_tok].reshape(n_blocks, MOE_BLOCK, d)

    def expert_block(args):
        xblk, e = args
        zz = xblk @ w_gate_up[e] + b_gate_up[e]
        g = jnp.minimum(zz[..., :D_FF], SWIGLU_LIMIT)
        u = jnp.clip(zz[..., D_FF:], -SWIGLU_LIMIT, SWIGLU_LIMIT)
        a = g * jax.nn.sigmoid(SWIGLU_ALPHA * g) * (u + 1.0)
        return a @ w_down[e] + b_down[e]

    yb = lax.map(expert_block, (xb, block_e)).reshape(P, d)
    y = jnp.zeros((N, d), jnp.float32).at[buf_tok].add(yb.astype(jnp.float32) * buf_w[:, None])
    return y.astype(xn.dtype).reshape(bsz, t, d)


def layer(x, p, st, lw):
    (g_mix, w_in, b_in, conv_w, conv_b, g_mnorm, w_a2, b_a2, g_gnorm, w_pa, w_pb, w_out, g_ffn,
     w_router, b_router, w_gate_up, b_gate_up, w_down, b_down, g_ple, w_ple_proj, w_ple_gate) = lw
    conv_st, C0, n0, m0, S0 = st
    mix, new_st = token_mixers(rmsnorm(x, g_mix), conv_st, C0, n0, m0, S0, w_in, b_in, conv_w, conv_b,
                               g_mnorm, w_a2, b_a2, g_gnorm, w_pa, w_pb, w_out)
    x = x + mix
    x = x + moe(rmsnorm(x, g_ffn), w_router, b_router, w_gate_up, b_gate_up, w_down, b_down)
    x = x + (p @ w_ple_proj) * jax.nn.sigmoid(rmsnorm(x, g_ple) @ w_ple_gate)
    return x, new_st


def setup_inputs(seed: int = 0) -> dict:
    key = jax.random.key(seed)
    ks = iter(jax.random.split(key, 64))

    def nrm(shape, scale):
        return jax.random.normal(next(ks), shape, jnp.float32) * scale

    x_prompt = nrm((BATCH, SEQ, D_MODEL), 1.0)
    x_sample = nrm((DEC_BATCH, DEC_SEQ, D_MODEL), 1.0)
    state_conv = nrm((DEPTH, DEC_BATCH, CONV_W - 1, 2 * M_DIM), 1.0)
    state_mlstm_C = nrm((DEPTH, DEC_BATCH, M_HEADS, M_HD, M_HD), 0.1)
    state_mlstm_n = nrm((DEPTH, DEC_BATCH, M_HEADS, M_HD), 0.1)
    state_mlstm_m = 2.0 + nrm((DEPTH, DEC_BATCH, M_HEADS), 0.5)
    state_gla_S = nrm((DEPTH, DEC_BATCH, G_HEADS, G_DK, G_DV), 0.3)
    p_prompt = nrm((DEPTH, BATCH, SEQ, PLE_DIM), 1.0)
    p_sample = nrm((DEPTH, DEC_BATCH, DEC_SEQ, PLE_DIM), 1.0)
    g_mix = 1.0 + nrm((DEPTH, D_MODEL), 0.05)
    w_in = nrm((DEPTH, D_MODEL, IN_COLS), D_MODEL ** -0.5)
    segs = [nrm((DEPTH, s), 0.02) for s in SPLITS]
    segs[F_GATE_SEG] = jnp.linspace(3.0, 6.0, M_HEADS)[None, :] + nrm((DEPTH, M_HEADS), 0.1)
    segs[I_GATE_SEG] = nrm((DEPTH, M_HEADS), 0.1)
    b_in = jnp.concatenate(segs, axis=-1)
    conv_w = nrm((DEPTH, CONV_W, 2 * M_DIM), CONV_W ** -0.5)
    conv_b = nrm((DEPTH, 2 * M_DIM), 0.02)
    g_mnorm = 1.0 + nrm((DEPTH, M_DIM), 0.05)
    w_a2 = nrm((DEPTH, G_RANK, G_KDIM), G_RANK ** -0.5)
    b_a2 = nrm((DEPTH, G_KDIM), 0.1)
    g_gnorm = 1.0 + nrm((DEPTH, G_VDIM), 0.05)
    w_pa = nrm((DEPTH, M_DIM, D_MODEL), M_DIM ** -0.5)
    w_pb = nrm((DEPTH, G_VDIM, D_MODEL), G_VDIM ** -0.5)
    w_out = nrm((DEPTH, D_MODEL, D_MODEL), D_MODEL ** -0.5)
    g_ffn = 1.0 + nrm((DEPTH, D_MODEL), 0.05)
    w_router = nrm((DEPTH, D_MODEL, N_EXPERTS), D_MODEL ** -0.5)
    b_router = nrm((DEPTH, N_EXPERTS), 0.01)
    w_gate_up = nrm((DEPTH, N_EXPERTS, D_MODEL, 2 * D_FF), D_MODEL ** -0.5)
    b_gate_up = nrm((DEPTH, N_EXPERTS, 2 * D_FF), 0.02)
    w_down = nrm((DEPTH, N_EXPERTS, D_FF, D_MODEL), D_FF ** -0.5)
    b_down = nrm((DEPTH, N_EXPERTS, D_MODEL), 0.02)
    g_ple = 1.0 + nrm((DEPTH, D_MODEL), 0.05)
    w_ple_proj = nrm((DEPTH, PLE_DIM, D_MODEL), PLE_DIM ** -0.5)
    w_ple_gate = nrm((DEPTH, D_MODEL, D_MODEL), D_MODEL ** -0.5)
    g_final = 1.0 + nrm((D_MODEL,), 0.05)
    return {'x_prompt': x_prompt, 'x_sample': x_sample, 'state_conv': state_conv,
            'state_mlstm_C': state_mlstm_C, 'state_mlstm_n': state_mlstm_n, 'state_mlstm_m': state_mlstm_m,
            'state_gla_S': state_gla_S, 'p_prompt': p_prompt, 'p_sample': p_sample,
            'g_mix': g_mix, 'w_in': w_in, 'b_in': b_in, 'conv_w': conv_w, 'conv_b': conv_b,
            'g_mnorm': g_mnorm, 'w_a2': w_a2, 'b_a2': b_a2, 'g_gnorm': g_gnorm, 'w_pa': w_pa, 'w_pb': w_pb,
            'w_out': w_out, 'g_ffn': g_ffn, 'w_router': w_router, 'b_router': b_router,
            'w_gate_up': w_gate_up, 'b_gate_up': b_gate_up, 'w_down': w_down, 'b_down': b_down,
            'g_ple': g_ple, 'w_ple_proj': w_ple_proj, 'w_ple_gate': w_ple_gate, 'g_final': g_final}


def reference(x_prompt, x_sample, state_conv, state_mlstm_C, state_mlstm_n, state_mlstm_m, state_gla_S,
              p_prompt, p_sample, g_mix, w_in, b_in, conv_w, conv_b, g_mnorm, w_a2, b_a2, g_gnorm,
              w_pa, w_pb, w_out, g_ffn, w_router, b_router, w_gate_up, b_gate_up, w_down, b_down,
              g_ple, w_ple_proj, w_ple_gate, g_final):
    f32 = jnp.float32
    bp = x_prompt.shape[0]
    yp = x_prompt
    ys = x_sample
    new_p = [[], [], [], [], []]
    new_s = [[], [], [], [], []]
    for i in range(DEPTH):
        lw = (g_mix[i], w_in[i], b_in[i], conv_w[i], conv_b[i], g_mnorm[i], w_a2[i], b_a2[i], g_gnorm[i],
              w_pa[i], w_pb[i], w_out[i], g_ffn[i], w_router[i], b_router[i], w_gate_up[i], b_gate_up[i],
              w_down[i], b_down[i], g_ple[i], w_ple_proj[i], w_ple_gate[i])
        zero_st = (jnp.zeros((bp, CONV_W - 1, 2 * M_DIM), x_prompt.dtype),
                   jnp.zeros((bp, M_HEADS, M_HD, M_HD), f32),
                   jnp.zeros((bp, M_HEADS, M_HD), f32),
                   jnp.zeros((bp, M_HEADS), f32),
                   jnp.zeros((bp, G_HEADS, G_DK, G_DV), f32))
        yp, sp = layer(yp, p_prompt[i], zero_st, lw)
        ys, ss = layer(ys, p_sample[i], (state_conv[i], state_mlstm_C[i], state_mlstm_n[i],
                                          state_mlstm_m[i], state_gla_S[i]), lw)
        for j in range(5):
            new_p[j].append(sp[j])
            new_s[j].append(ss[j])
    y_prompt = rmsnorm(yp, g_final)
    y_sample = rmsnorm(ys, g_final)
    conv_p, C_p, n_p, m_p, S_p = [jnp.stack(a, axis=0) for a in new_p]
    conv_s, C_s, n_s, m_s, S_s = [jnp.stack(a, axis=0) for a in new_s]
    return (y_prompt, y_sample, conv_p, C_p, n_p, m_p, S_p, conv_s, C_s, n_s, m_s, S_s)
```

```python
import functools

import numpy as np
import jax
import jax.numpy as jnp
from jax import lax
from jax.experimental import pallas as pl
from jax.experimental.pallas import tpu as pltpu

F32 = jnp.float32
BF16 = jnp.bfloat16

D_MODEL = 1024
EPS = 1e-6
M_HEADS = 4
M_DIM = D_MODEL
M_HD = M_DIM // M_HEADS
CONV_W = 4
G_HEADS = 4
G_KDIM = D_MODEL // 2
G_VDIM = D_MODEL
G_DK = G_KDIM // G_HEADS
G_DV = G_VDIM // G_HEADS
G_RANK = 16
G_TAU = 16.0
N_EXPERTS = 32
TOP_K = 4
D_FF = D_MODEL
SWIGLU_LIMIT = 7.0
SWIGLU_ALPHA = 1.702
PLE_DIM = 256
SPLITS = (M_DIM, M_DIM, M_DIM, M_HEADS, M_HEADS, M_DIM, G_KDIM, G_KDIM, G_VDIM, G_RANK, G_VDIM, D_MODEL, D_MODEL)
_OFF = np.concatenate([[0], np.cumsum(SPLITS)]).tolist()

LANES = 128
SUBLANES = 8
ROW_CHUNKS = D_MODEL // LANES
VMEM_LIMIT = 56 * 1024 * 1024
MIX_CHUNK = 256
MOE_ROWS = 256
NEG_BIG = -1e30


def _sigmoid(x):
    return 1.0 / (1.0 + jnp.exp(-x))


def _log_sigmoid(x):
    return jnp.minimum(x, 0.0) - jnp.log1p(jnp.exp(-jnp.abs(x)))


def _rms(x, g):
    return x * lax.rsqrt(jnp.mean(x * x, axis=-1, keepdims=True) + EPS) * g


def _dot(a, b):
    return jnp.dot(a, b, preferred_element_type=F32)


def _dot_nt(a, b):
    return lax.dot_general(a, b, (((1,), (1,)), ((), ())), preferred_element_type=F32)


def _dot_tn(a, b):
    return lax.dot_general(a, b, (((0,), (0,)), ((), ())), preferred_element_type=F32)


def _params(*sem):
    return pltpu.CompilerParams(dimension_semantics=sem, vmem_limit_bytes=VMEM_LIMIT)


def _inproj_kernel(x_ref, g_ref, w_ref, b_ref, ws_ref, bs_ref, z_ref, zs_ref, h_scr):
    @pl.when(pl.program_id(1) == 0)
    def _():
        hb = _rms(x_ref[...], g_ref[...]).astype(BF16)
        h_scr[...] = hb
        zs_ref[...] = _dot(hb, ws_ref[...]) + bs_ref[...]

    z_ref[...] = (_dot(h_scr[...], w_ref[...]) + b_ref[...]).astype(BF16)


def _inproj(x, g, w_big, b_big, w_small, b_small, tm, tn=1024):
    n = x.shape[0]
    cols = w_big.shape[1]
    return pl.pallas_call(
        _inproj_kernel,
        grid=(n // tm, cols // tn),
        in_specs=[
            pl.BlockSpec((tm, D_MODEL), lambda i, j: (i, 0)),
            pl.BlockSpec((1, D_MODEL), lambda i, j: (0, 0)),
            pl.BlockSpec((D_MODEL, tn), lambda i, j: (0, j)),
            pl.BlockSpec((1, tn), lambda i, j: (0, j)),
            pl.BlockSpec((D_MODEL, LANES), lambda i, j: (0, 0)),
            pl.BlockSpec((1, LANES), lambda i, j: (0, 0)),
        ],
        out_specs=[
            pl.BlockSpec((tm, tn), lambda i, j: (i, j)),
            pl.BlockSpec((tm, LANES), lambda i, j: (i, 0)),
        ],
        out_shape=[jax.ShapeDtypeStruct((n, cols), BF16), jax.ShapeDtypeStruct((n, LANES), F32)],
        scratch_shapes=[pltpu.VMEM((tm, D_MODEL), BF16)],
        compiler_params=_params("parallel", "arbitrary"),
        name="inproj",
    )(x, g, w_big, b_big, w_small, b_small)


def _level_matrix(L):
    t = np.arange(L)[:, None]
    s = np.arange(L)[None, :]
    x = np.maximum(t ^ s, 1)
    lv = np.floor(np.log2(x)).astype(np.int32)
    lv = np.where(t == s, -1, lv)
    return np.where(s > t, -2, lv).astype(np.int32)


def _bcast_row(x, period, r):
    L, W = x.shape
    x3 = x.reshape(L // period, period, W)
    return jnp.broadcast_to(x3[:, r:r + 1, :], x3.shape).reshape(L, W)


def _segmented_scans(la, L):
    W = la.shape[1]
    row = lax.broadcasted_iota(jnp.int32, (L, W), 0)
    fwd = [la]
    rev = [la]
    b = 1
    while b < L:
        per = 2 * b
        res = row & (per - 1)
        f, r = fwd[-1], rev[-1]
        if per < SUBLANES:
            addf = jnp.zeros_like(la)
            addr = jnp.zeros_like(la)
            for j in range(b, per):
                addf = jnp.where(res == j, pltpu.roll(f, j - (b - 1), 0), addf)
            for j in range(0, b):
                addr = jnp.where(res == j, pltpu.roll(r, L - (b - j), 0), addr)
        else:
            addf = jnp.where(res >= b, _bcast_row(f, per, b - 1), 0.0)
            addr = jnp.where(res < b, _bcast_row(r, per, b), 0.0)
        fwd.append(f + addf)
        rev.append(r + addr)
        b = per
    return fwd, rev


def _mixer_kernel(zqk_ref, zv_ref, zo_ref, zgq_ref, zgk_ref, zgv_ref, zgg_ref, zs_ref,
                  conv0_ref, c0_ref, n0_ref, m0_ref, s0_ref,
                  cw_ref, cb_ref, gm_ref, w2_ref, ba2_ref, gg_ref, lv_ref,
                  ya_ref, yb_ref, tail_ref, c_ref, n_ref, m_ref, st_ref, ext_ref, *, L):
    @pl.when(pl.program_id(1) == 0)
    def _():
        tail_ref[...] = conv0_ref[...]
        c_ref[...] = c0_ref[...]
        n_ref[...] = n0_ref[...]
        m_ref[...] = m0_ref[...]
        st_ref[...] = s0_ref[...]

    lv = lv_ref[...]
    causal = lv >= -1

    u = zqk_ref[...].astype(F32)
    ext_ref[0:SUBLANES, :] = tail_ref[0]
    ext_ref[SUBLANES:SUBLANES + L, :] = u
    acc = cb_ref[...] + cw_ref[CONV_W - 1:CONV_W, :] * u
    for w in range(CONV_W - 1):
        acc = acc + cw_ref[w:w + 1, :] * ext_ref[pl.ds(SUBLANES - (CONV_W - 1) + w, L), :]
    tail_ref[0] = ext_ref[pl.ds(L, SUBLANES), :]
    qk = acc * _sigmoid(acc)

    zs = zs_ref[...]
    row = lax.broadcasted_iota(jnp.int32, (L, LANES), 0)
    col = lax.broadcasted_iota(jnp.int32, (L, LANES), 1)
    bcs = _log_sigmoid(zs)
    sh = 1
    while sh < L:
        bcs = bcs + jnp.where(row >= sh, pltpu.roll(bcs, sh, 0), 0.0)
        sh *= 2
    gates = jnp.where(col < M_HEADS, zs, bcs)
    if L < LANES:
        gates = jnp.concatenate([gates, jnp.zeros((LANES - L, LANES), F32)], axis=0)
    gt = gates.T[:, :L]

    for h in range(M_HEADS):
        hs = slice(h * M_HD, (h + 1) * M_HD)
        ig_col = zs[:, h:h + 1]
        b_col = bcs[:, M_HEADS + h:M_HEADS + h + 1]
        ig_row = gt[h:h + 1, :]
        b_row = gt[M_HEADS + h:M_HEADS + h + 1, :]
        m_prev = m_ref[0, h][0:1, 0:1]
        logd = jnp.where(causal, b_col - b_row + ig_row, -jnp.inf)
        inter = b_col + m_prev
        m_t = jnp.maximum(inter, jnp.max(logd, axis=-1, keepdims=True))
        q_f = qk[:, hs]
        k_f = qk[:, M_DIM + h * M_HD:M_DIM + (h + 1) * M_HD] * (M_HD ** -0.5)
        q_b = q_f.astype(BF16)
        v_b = zv_ref[:, hs]
        s = _dot_nt(q_b, k_f.astype(BF16)) * jnp.exp(logd - m_t)
        w_int = jnp.exp(inter - m_t)
        c_old = c_ref[0, h]
        n_old = n_ref[0, h]
        num = _dot(s.astype(BF16), v_b) + w_int * _dot(q_b, c_old.astype(BF16))
        den = jnp.sum(s, axis=-1, keepdims=True) + w_int * jnp.sum(q_f * n_old, axis=-1, keepdims=True)
        hh = num / jnp.maximum(jnp.abs(den), jnp.exp(-m_t))
        m_new = m_t[L - 1:L, :]
        b_last = b_col[L - 1:L, :]
        w_k = jnp.exp(b_last - b_col + ig_col - m_new)
        dec = jnp.exp(b_last + m_prev - m_new)
        k_w = k_f * w_k
        c_ref[0, h] = dec * c_old + _dot_tn(k_w.astype(BF16), v_b)
        n_ref[0, h] = dec * n_old + jnp.sum(k_w, axis=0, keepdims=True)
        m_ref[0, h] = jnp.broadcast_to(m_new, (SUBLANES, LANES))
        hn = _rms(hh, gm_ref[:, hs])
        ya_ref[:, hs] = (hn * _sigmoid(zo_ref[:, hs].astype(F32))).astype(BF16)

    la = _log_sigmoid(_dot(zs.astype(BF16), w2_ref[...]) + ba2_ref[...]) * (1.0 / G_TAU)
    fwd, rev = _segmented_scans(la, L)
    nlev = len(fwd) - 1
    qg = zgq_ref[...].astype(F32) * (G_DK ** -0.5)
    kg = zgk_ref[...].astype(F32)
    qs = [(qg * jnp.exp(fwd[j])).astype(BF16) for j in range(nlev)]
    ks = [(kg * jnp.exp(rev[j] - la)).astype(BF16) for j in range(nlev)]
    q_in = (qg * jnp.exp(fwd[nlev])).astype(BF16)
    k_out = (kg * jnp.exp(rev[nlev] - la)).astype(BF16)
    q_d = qg.astype(BF16)
    k_d = kg.astype(BF16)
    dec_all = jnp.exp(fwd[nlev][L - 1:L, :])
    for h in range(G_HEADS):
        ks_ = slice(h * G_DK, (h + 1) * G_DK)
        vs_ = slice(h * G_DV, (h + 1) * G_DV)
        att = jnp.where(lv == -1, _dot_nt(q_d[:, ks_], k_d[:, ks_]), 0.0)
        for j in range(nlev):
            att = jnp.where(lv == j, _dot_nt(qs[j][:, ks_], ks[j][:, ks_]), att)
        v_b = zgv_ref[:, vs_]
        st_old = st_ref[0, h]
        o = _dot(att.astype(BF16), v_b) + _dot_nt(q_in[:, ks_], st_old.astype(BF16))
        st_ref[0, h] = st_old * dec_all[:, ks_] + _dot_tn(v_b, k_out[:, ks_])
        on = _rms(o, gg_ref[:, vs_])
        g = zgg_ref[:, vs_].astype(F32)
        yb_ref[:, vs_] = (on * (g * _sigmoid(g))).astype(BF16)


def _mixers(z, zs, conv0, c0, n0, m0, st0, cw, cb, gm, w2, ba2, gg, batch, T, L):
    nc = T // L
    lv = jnp.asarray(_level_matrix(L))
    row_blk = lambda width, cidx: pl.BlockSpec((L, width), lambda b, c: (b * nc + c, cidx))
    state = lambda *shape: pl.BlockSpec((1,) + shape, lambda b, c: (b,) + (0,) * len(shape))
    const = lambda *shape: pl.BlockSpec(shape, lambda b, c: (0,) * len(shape))
    n_tok = batch * T
    return pl.pallas_call(
        functools.partial(_mixer_kernel, L=L),
        grid=(batch, nc),
        in_specs=[
            row_blk(2 * M_DIM, 0),
            row_blk(M_DIM, 2),
            row_blk(M_DIM, 3),
            row_blk(G_KDIM, 8),
            row_blk(G_KDIM, 9),
            row_blk(G_VDIM, 5),
            row_blk(G_VDIM, 6),
            pl.BlockSpec((L, LANES), lambda b, c: (b * nc + c, 0)),
            state(SUBLANES, 2 * M_DIM),
            state(M_HEADS, M_HD, M_HD),
            state(M_HEADS, 1, M_HD),
            state(M_HEADS, SUBLANES, LANES),
            state(G_HEADS, G_DV, G_DK),
            const(CONV_W, 2 * M_DIM),
            const(1, 2 * M_DIM),
            const(1, M_DIM),
            const(LANES, G_KDIM),
            const(1, G_KDIM),
            const(1, G_VDIM),
            const(L, L),
        ],
        out_specs=[
            pl.BlockSpec((L, M_DIM), lambda b, c: (b * nc + c, 0)),
            pl.BlockSpec((L, G_VDIM), lambda b, c: (b * nc + c, 0)),
            state(SUBLANES, 2 * M_DIM),
            state(M_HEADS, M_HD, M_HD),
            state(M_HEADS, 1, M_HD),
            state(M_HEADS, SUBLANES, LANES),
            state(G_HEADS, G_DV, G_DK),
        ],
        out_shape=[
            jax.ShapeDtypeStruct((n_tok, M_DIM), BF16),
            jax.ShapeDtypeStruct((n_tok, G_VDIM), BF16),
            jax.ShapeDtypeStruct((batch, SUBLANES, 2 * M_DIM), F32),
            jax.ShapeDtypeStruct((batch, M_HEADS, M_HD, M_HD), F32),
            jax.ShapeDtypeStruct((batch, M_HEADS, 1, M_HD), F32),
            jax.ShapeDtypeStruct((batch, M_HEADS, SUBLANES, LANES), F32),
            jax.ShapeDtypeStruct((batch, G_HEADS, G_DV, G_DK), F32),
        ],
        scratch_shapes=[pltpu.VMEM((L + SUBLANES, 2 * M_DIM), F32)],
        compiler_params=_params("parallel", "arbitrary"),
        name="mixers",
    )(z, z, z, z, z, z, z, zs, conv0, c0, n0, m0, st0, cw, cb, gm, w2, ba2, gg, lv)


def _outproj_kernel(ya_ref, yb_ref, ua_ref, ub_ref, x_ref, wpa_ref, wpb_ref, wout_ref, gffn_ref,
                    wr_ref, br_ref, tril_ref, carry_in_ref,
                    x1_ref, xn_ref, eo_ref, wo_ref, carry_ref, *, tm):
    @pl.when(pl.program_id(0) == 0)
    def _():
        carry_ref[...] = carry_in_ref[...]

    a = _dot(ya_ref[...], wpa_ref[...])
    b = _dot(yb_ref[...], wpb_ref[...])
    merged = _sigmoid(ua_ref[...].astype(F32)) * a + _sigmoid(ub_ref[...].astype(F32)) * b
    x1 = x_ref[...] + _dot(merged.astype(BF16), wout_ref[...])
    x1_ref[...] = x1
    xn = _rms(x1, gffn_ref[...])
    for j in range(ROW_CHUNKS):
        xn_ref[pl.ds(j, tm, stride=ROW_CHUNKS), :] = xn[:, j * LANES:(j + 1) * LANES]

    logits = _dot(xn.astype(BF16), wr_ref[...]) + br_ref[...]
    lane = lax.broadcasted_iota(jnp.int32, (tm, LANES), 1).astype(F32)
    vals, idxs = [], []
    l = logits
    for _ in range(TOP_K):
        mx = jnp.max(l, axis=-1, keepdims=True)
        ix = jnp.min(jnp.where(l == mx, lane, float(LANES)), axis=-1, keepdims=True)
        vals.append(mx)
        idxs.append(ix)
        l = jnp.where(lane == ix, -jnp.inf, l)
    ev = [jnp.exp(v - vals[0]) for v in vals]
    den = ev[0] + ev[1] + ev[2] + ev[3]

    cnt = jnp.zeros((tm, LANES), F32)
    for ix in idxs:
        cnt = cnt + jnp.where(lane == ix, 1.0, 0.0)
    pre = _dot(tril_ref[...], cnt.astype(BF16)) + carry_ref[0:1, :]
    eo = jnp.zeros((tm, LANES), F32)
    wo = jnp.zeros((tm, LANES), F32)
    for k in range(TOP_K):
        rank = jnp.sum(jnp.where(lane == idxs[k], pre, 0.0), axis=-1, keepdims=True)
        eo = jnp.where(lane == float(k), idxs[k], eo)
        eo = jnp.where(lane == float(TOP_K + k), rank, eo)
        wo = jnp.where(lane == float(k), ev[k] / den, wo)
    eo_ref[...] = eo.astype(jnp.int32)
    wo_ref[...] = wo
    carry_ref[...] = carry_ref[...] + jnp.sum(cnt, axis=0, keepdims=True)


def _outproj(ya, yb, z, x, wpa, wpb, wout, gffn, wr, br, carry, tm):
    n = x.shape[0]
    tril = jnp.asarray(np.tril(np.ones((tm, tm), np.float32), -1), BF16)
    tok = lambda width, cidx=0: pl.BlockSpec((tm, width), lambda i: (i, cidx))
    const = lambda *shape: pl.BlockSpec(shape, lambda i: (0,) * len(shape))
    return pl.pallas_call(
        functools.partial(_outproj_kernel, tm=tm),
        grid=(n // tm,),
        in_specs=[
            tok(M_DIM), tok(G_VDIM), tok(D_MODEL, 7), tok(D_MODEL, 8), tok(D_MODEL),
            const(M_DIM, D_MODEL), const(G_VDIM, D_MODEL), const(D_MODEL, D_MODEL), const(1, D_MODEL),
            const(D_MODEL, LANES), const(1, LANES), const(tm, tm), const(SUBLANES, LANES),
        ],
        out_specs=[
            tok(D_MODEL),
            pl.BlockSpec((tm * ROW_CHUNKS, LANES), lambda i: (i, 0)),
            tok(LANES), tok(LANES), const(SUBLANES, LANES),
        ],
        out_shape=[
            jax.ShapeDtypeStruct((n, D_MODEL), F32),
            jax.ShapeDtypeStruct((n * ROW_CHUNKS, LANES), F32),
            jax.ShapeDtypeStruct((n, LANES), jnp.int32),
            jax.ShapeDtypeStruct((n, LANES), F32),
            jax.ShapeDtypeStruct((SUBLANES, LANES), F32),
        ],
        compiler_params=_params("arbitrary"),
        name="outproj_router",
    )(ya, yb, z, z, x, wpa, wpb, wout, gffn, wr, br, tril, carry)


def _row_copy(src, src_row, dst, dst_row, sem):
    return pltpu.make_async_copy(
        src.at[pl.ds(pl.multiple_of(src_row * ROW_CHUNKS, ROW_CHUNKS), ROW_CHUNKS)],
        dst.at[pl.ds(pl.multiple_of(dst_row * ROW_CHUNKS, ROW_CHUNKS), ROW_CHUNKS)], sem)


def _dispatch_kernel(pos_ref, xn_ref, xs_in_ref, xs_ref, sem, *, tm):
    del xs_in_ref
    base = pl.program_id(0) * tm

    def issue(r, carry):
        for k in range(TOP_K):
            _row_copy(xn_ref, r, xs_ref, pos_ref[(base + r) * TOP_K + k], sem).start()
        return carry

    lax.fori_loop(0, tm, issue, 0)
    for _ in range(TOP_K):
        pltpu.make_async_copy(xn_ref, xs_ref.at[pl.ds(0, tm * ROW_CHUNKS)], sem).wait()


def _dispatch(pos_flat, xn, xs, tm):
    n = xn.shape[0] // ROW_CHUNKS
    return pl.pallas_call(
        functools.partial(_dispatch_kernel, tm=tm),
        grid_spec=pltpu.PrefetchScalarGridSpec(
            num_scalar_prefetch=1,
            grid=(n // tm,),
            in_specs=[
                pl.BlockSpec((tm * ROW_CHUNKS, LANES), lambda i, pos: (i, 0)),
                pl.BlockSpec(memory_space=pl.ANY),
            ],
            out_specs=pl.BlockSpec(memory_space=pl.ANY),
            scratch_shapes=[pltpu.SemaphoreType.DMA(())],
        ),
        out_shape=jax.ShapeDtypeStruct(xs.shape, xs.dtype),
        input_output_aliases={2: 0},
        compiler_params=_params("arbitrary"),
        name="dispatch",
    )(pos_flat, xn, xs)


def _expert_kernel(be_ref, nused_ref, xs_ref, wgu_ref, bgu_ref, wd_ref, bd_ref, y_ref, wgu_bf, wd_bf, *, rows):
    i = pl.program_id(0)
    changed = jnp.logical_or(i == 0, be_ref[i] != be_ref[jnp.maximum(i - 1, 0)])

    @pl.when(changed)
    def _():
        wgu_bf[...] = wgu_ref[...].astype(BF16)
        wd_bf[...] = wd_ref[...].astype(BF16)

    @pl.when(i < nused_ref[0])
    def _():
        x = jnp.concatenate([xs_ref[pl.ds(j, rows, stride=ROW_CHUNKS), :] for j in range(ROW_CHUNKS)], axis=-1)
        zz = _dot(x.astype(BF16), wgu_bf[...]) + bgu_ref[...]
        g = jnp.minimum(zz[:, :D_FF], SWIGLU_LIMIT)
        u = jnp.clip(zz[:, D_FF:], -SWIGLU_LIMIT, SWIGLU_LIMIT)
        act = g * _sigmoid(SWIGLU_ALPHA * g) * (u + 1.0)
        y = _dot(act.astype(BF16), wd_bf[...]) + bd_ref[...]
        for j in range(ROW_CHUNKS):
            y_ref[pl.ds(j, rows, stride=ROW_CHUNKS), :] = y[:, j * LANES:(j + 1) * LANES]

    @pl.when(i >= nused_ref[0])
    def _():
        y_ref[...] = jnp.zeros_like(y_ref)


def _experts(block_e, n_used, xs, wgu, bgu, wd, bd, rows):
    nb = xs.shape[0] // (rows * ROW_CHUNKS)
    return pl.pallas_call(
        functools.partial(_expert_kernel, rows=rows),
        grid_spec=pltpu.PrefetchScalarGridSpec(
            num_scalar_prefetch=2,
            grid=(nb,),
            in_specs=[
                pl.BlockSpec((rows * ROW_CHUNKS, LANES), lambda i, be, nu: (i, 0)),
                pl.BlockSpec((None, D_MODEL, 2 * D_FF), lambda i, be, nu: (be[i], 0, 0)),
                pl.BlockSpec((None, 1, 2 * D_FF), lambda i, be, nu: (be[i], 0, 0)),
                pl.BlockSpec((None, D_FF, D_MODEL), lambda i, be, nu: (be[i], 0, 0)),
                pl.BlockSpec((None, 1, D_MODEL), lambda i, be, nu: (be[i], 0, 0)),
            ],
            out_specs=pl.BlockSpec((rows * ROW_CHUNKS, LANES), lambda i, be, nu: (i, 0)),
            scratch_shapes=[pltpu.VMEM((D_MODEL, 2 * D_FF), BF16), pltpu.VMEM((D_FF, D_MODEL), BF16)],
        ),
        out_shape=jax.ShapeDtypeStruct(xs.shape, F32),
        compiler_params=_params("arbitrary"),
        name="experts",
    )(block_e, n_used, xs, wgu, bgu, wd, bd)


def _combine_kernel(pos_ref, ye_ref, x1_ref, wo_ref, p_ref, wproj_ref, wgate_ref, gple_ref, gfin_ref,
                    y_ref, rows_scr, sem, *, tm):
    base = pl.program_id(0) * tm

    def issue(r, carry):
        for k in range(TOP_K):
            _row_copy(ye_ref, pos_ref[(base + r) * TOP_K + k], rows_scr, k * tm + r, sem).start()
        return carry

    lax.fori_loop(0, tm, issue, 0)
    for k in range(TOP_K):
        pltpu.make_async_copy(ye_ref.at[pl.ds(0, tm * ROW_CHUNKS)],
                              rows_scr.at[pl.ds(k * tm * ROW_CHUNKS, tm * ROW_CHUNKS)], sem).wait()

    wo = wo_ref[...]
    moe = jnp.zeros((tm, D_MODEL), F32)
    for k in range(TOP_K):
        rows = jnp.concatenate(
            [rows_scr[pl.ds(k * tm * ROW_CHUNKS + j, tm, stride=ROW_CHUNKS), :] for j in range(ROW_CHUNKS)], axis=-1)
        moe = moe + rows * wo[:, k:k + 1]
    x2 = x1_ref[...] + moe
    gate = _sigmoid(_dot(_rms(x2, gple_ref[...]).astype(BF16), wgate_ref[...]))
    x3 = x2 + _dot(p_ref[...].astype(BF16), wproj_ref[...]) * gate
    y_ref[...] = _rms(x3, gfin_ref[...])


def _combine(pos_flat, ye, x1, wo, p, wproj, wgate, gple, gfin, tm):
    n = x1.shape[0]
    tok = lambda width: pl.BlockSpec((tm, width), lambda i, pos: (i, 0))
    const = lambda *shape: pl.BlockSpec(shape, lambda i, pos: (0,) * len(shape))
    return pl.pallas_call(
        functools.partial(_combine_kernel, tm=tm),
        grid_spec=pltpu.PrefetchScalarGridSpec(
            num_scalar_prefetch=1,
            grid=(n // tm,),
            in_specs=[
                pl.BlockSpec(memory_space=pl.ANY),
                tok(D_MODEL), tok(LANES), tok(PLE_DIM),
                const(PLE_DIM, D_MODEL), const(D_MODEL, D_MODEL), const(1, D_MODEL), const(1, D_MODEL),
            ],
            out_specs=tok(D_MODEL),
            scratch_shapes=[pltpu.VMEM((TOP_K * tm * ROW_CHUNKS, LANES), F32), pltpu.SemaphoreType.DMA(())],
        ),
        out_shape=jax.ShapeDtypeStruct((n, D_MODEL), F32),
        compiler_params=_params("arbitrary"),
        name="combine_ple",
    )(pos_flat, ye, x1, wo, p, wproj, wgate, gple, gfin)


def _tile(n, pref):
    t = pref
    while n % t:
        t //= 2
    return t


def _seg(a, i):
    return a[..., _OFF[i]:_OFF[i + 1]]


def kernel(x_prompt, x_sample, state_conv, state_mlstm_C, state_mlstm_n, state_mlstm_m, state_gla_S, p_prompt, p_sample, g_mix, w_in, b_in, conv_w, conv_b, g_mnorm, w_a2, b_a2, g_gnorm, w_pa, w_pb, w_out, g_ffn, w_router, b_router, w_gate_up, b_gate_up, w_down, b_down, g_ple, w_ple_proj, w_ple_gate, g_final):
    assert g_mix.shape[0] == 1, "single-layer kernel"
    bp, tp, _ = x_prompt.shape
    bs, ts, _ = x_sample.shape

    big = (0, 1, 2, 5, 6, 7, 8, 10, 11, 12)
    w_big = jnp.concatenate([_seg(w_in[0], i) for i in big], axis=-1).astype(BF16)
    b_big = jnp.concatenate([_seg(b_in[0], i) for i in big], axis=-1)[None, :]
    small = (3, 4, 9)
    n_small = sum(SPLITS[i] for i in small)
    w_small = jnp.pad(jnp.concatenate([_seg(w_in[0], i) for i in small], axis=-1),
                      ((0, 0), (0, LANES - n_small))).astype(BF16)
    b_small = jnp.pad(jnp.concatenate([_seg(b_in[0], i) for i in small], axis=-1), (0, LANES - n_small))[None, :]
    w2 = jnp.pad(w_a2[0], ((2 * M_HEADS, LANES - 2 * M_HEADS - G_RANK), (0, 0))).astype(BF16)
    wr = jnp.pad(w_router[0], ((0, 0), (0, LANES - N_EXPERTS))).astype(BF16)
    br = jnp.pad(b_router[0], (0, LANES - N_EXPERTS), constant_values=NEG_BIG)[None, :]
    wpa, wpb, wout = w_pa[0].astype(BF16), w_pb[0].astype(BF16), w_out[0].astype(BF16)
    wproj, wgate = w_ple_proj[0].astype(BF16), w_ple_gate[0].astype(BF16)
    row = lambda a: a.reshape(1, -1)

    def mix_path(x3d, conv0, c0, n0, m0, s0):
        batch, T, _ = x3d.shape
        x = x3d.reshape(batch * T, D_MODEL)
        z, zs = _inproj(x, row(g_mix[0]), w_big, b_big, w_small, b_small, _tile(batch * T, 1024))
        L = min(MIX_CHUNK, T)
        conv0p = jnp.pad(conv0, ((0, 0), (SUBLANES - (CONV_W - 1), 0), (0, 0)))
        m0b = jnp.broadcast_to(m0[:, :, None, None], (batch, M_HEADS, SUBLANES, LANES))
        ya, yb, tail, c1, n1, m1, st1 = _mixers(
            z, zs, conv0p, c0, n0[:, :, None, :], m0b, jnp.swapaxes(s0, -1, -2),
            conv_w[0], row(conv_b[0]), row(g_mnorm[0]), w2, row(b_a2[0]), row(g_gnorm[0]), batch, T, L)
        new_state = (tail[:, SUBLANES - (CONV_W - 1):, :], c1, n1[:, :, 0, :], m1[:, :, 0, 0],
                     jnp.swapaxes(st1, -1, -2))
        return x, z, ya, yb, new_state

    zero_state = (jnp.zeros((bp, CONV_W - 1, 2 * M_DIM), F32), jnp.zeros((bp, M_HEADS, M_HD, M_HD), F32),
                  jnp.zeros((bp, M_HEADS, M_HD), F32), jnp.zeros((bp, M_HEADS), F32),
                  jnp.zeros((bp, G_HEADS, G_DK, G_DV), F32))
    xp, zp, yap, ybp, st_p = mix_path(x_prompt, *zero_state)
    xs_, zs_, yas, ybs, st_s = mix_path(x_sample, state_conv[0], state_mlstm_C[0], state_mlstm_n[0],
                                        state_mlstm_m[0], state_gla_S[0])

    n_p, n_s = xp.shape[0], xs_.shape[0]
    tm_p, tm_s = _tile(n_p, 512), _tile(n_s, 512)
    carry0 = jnp.zeros((SUBLANES, LANES), F32)
    x1p, xnp_, eop, wop, carry1 = _outproj(yap, ybp, zp, xp, wpa, wpb, wout, row(g_ffn[0]), wr, br, carry0, tm_p)
    x1s, xns, eos, wos, carry2 = _outproj(yas, ybs, zs_, xs_, wpa, wpb, wout, row(g_ffn[0]), wr, br, carry1, tm_s)

    counts = carry2[0, :N_EXPERTS].astype(jnp.int32)
    padded = (counts + MOE_ROWS - 1) // MOE_ROWS * MOE_ROWS
    pend = jnp.cumsum(padded)
    pstart = pend - padded
    nk = (n_p + n_s) * TOP_K
    n_blocks = (nk + N_EXPERTS * (MOE_ROWS - 1) + MOE_ROWS - 1) // MOE_ROWS
    block_e = jnp.minimum(jnp.searchsorted(pend, jnp.arange(n_blocks, dtype=jnp.int32) * MOE_ROWS, side='right'),
                          N_EXPERTS - 1).astype(jnp.int32)
    n_used = (pend[-1:] // MOE_ROWS).astype(jnp.int32)

    def dest(eo):
        return (pstart[eo[:, :TOP_K]] + eo[:, TOP_K:2 * TOP_K]).reshape(-1)

    pos_p, pos_s = dest(eop), dest(eos)

    xs_sorted = jnp.zeros((n_blocks * MOE_ROWS * ROW_CHUNKS, LANES), F32)
    xs_sorted = _dispatch(pos_p, xnp_, xs_sorted, _tile(n_p, 256))
    xs_sorted = _dispatch(pos_s, xns, xs_sorted, _tile(n_s, 256))
    ye = _experts(block_e, n_used, xs_sorted, w_gate_up[0], b_gate_up[0][:, None, :], w_down[0],
                  b_down[0][:, None, :], MOE_ROWS)
    fin = lambda pos, x1, wo, p, tm: _combine(pos, ye, x1, wo, p.reshape(-1, PLE_DIM), wproj, wgate,
                                              row(g_ple[0]), row(g_final), tm)
    y_p = fin(pos_p, x1p, wop, p_prompt[0], _tile(n_p, 256)).reshape(bp, tp, D_MODEL)
    y_s = fin(pos_s, x1s, wos, p_sample[0], _tile(n_s, 256)).reshape(bs, ts, D_MODEL)

    lead = lambda a: a[None]
    return (y_p, y_s) + tuple(lead(a) for a in st_p) + tuple(lead(a) for a in st_s)
```

```python
import functools

import numpy as np
import jax
import jax.numpy as jnp
from jax import lax
from jax.experimental import pallas as pl
from jax.experimental.pallas import tpu as pltpu

F32 = jnp.float32
BF16 = jnp.bfloat16

D_MODEL = 1024
EPS = 1e-6
M_HEADS = 4
M_DIM = D_MODEL
M_HD = M_DIM // M_HEADS
CONV_W = 4
G_HEADS = 4
G_KDIM = D_MODEL // 2
G_VDIM = D_MODEL
G_DK = G_KDIM // G_HEADS
G_DV = G_VDIM // G_HEADS
G_RANK = 16
G_TAU = 16.0
N_EXPERTS = 32
TOP_K = 4
D_FF = D_MODEL
SWIGLU_LIMIT = 7.0
SWIGLU_ALPHA = 1.702
PLE_DIM = 256
SPLITS = (M_DIM, M_DIM, M_DIM, M_HEADS, M_HEADS, M_DIM, G_KDIM, G_KDIM, G_VDIM, G_RANK, G_VDIM, D_MODEL, D_MODEL)
_OFF = np.concatenate([[0], np.cumsum(SPLITS)]).tolist()

LANES = 128
SUBLANES = 8
ROW_CHUNKS = D_MODEL // LANES
VMEM_LIMIT = 56 * 1024 * 1024
MIX_CHUNK = 256
MOE_ROWS = 256
NEG_BIG = -1e30


def _sigmoid(x):
    return 1.0 / (1.0 + jnp.exp(-x))


def _log_sigmoid(x):
    return jnp.minimum(x, 0.0) - jnp.log1p(jnp.exp(-jnp.abs(x)))


def _rms(x, g):
    return x * lax.rsqrt(jnp.mean(x * x, axis=-1, keepdims=True) + EPS) * g


def _dot(a, b):
    return jnp.dot(a, b, preferred_element_type=F32)


def _dot_nt(a, b):
    return lax.dot_general(a, b, (((1,), (1,)), ((), ())), preferred_element_type=F32)


def _dot_tn(a, b):
    return lax.dot_general(a, b, (((0,), (0,)), ((), ())), preferred_element_type=F32)


def _params(*sem):
    return pltpu.CompilerParams(dimension_semantics=sem, vmem_limit_bytes=VMEM_LIMIT)


def _inproj_kernel(x_ref, g_ref, w_ref, b_ref, ws_ref, bs_ref, z_ref, zs_ref, h_scr):
    @pl.when(pl.program_id(1) == 0)
    def _():
        hb = _rms(x_ref[...], g_ref[...]).astype(BF16)
        h_scr[...] = hb
        zs_ref[...] = _dot(hb, ws_ref[...]) + bs_ref[...]

    z_ref[...] = (_dot(h_scr[...], w_ref[...]) + b_ref[...]).astype(BF16)


def _inproj(x, g, w_big, b_big, w_small, b_small, tm, tn=1024):
    n = x.shape[0]
    cols = w_big.shape[1]
    return pl.pallas_call(
        _inproj_kernel,
        grid=(n // tm, cols // tn),
        in_specs=[
            pl.BlockSpec((tm, D_MODEL), lambda i, j: (i, 0)),
            pl.BlockSpec((1, D_MODEL), lambda i, j: (0, 0)),
            pl.BlockSpec((D_MODEL, tn), lambda i, j: (0, j)),
            pl.BlockSpec((1, tn), lambda i, j: (0, j)),
            pl.BlockSpec((D_MODEL, LANES), lambda i, j: (0, 0)),
            pl.BlockSpec((1, LANES), lambda i, j: (0, 0)),
        ],
        out_specs=[
            pl.BlockSpec((tm, tn), lambda i, j: (i, j)),
            pl.BlockSpec((tm, LANES), lambda i, j: (i, 0)),
        ],
        out_shape=[jax.ShapeDtypeStruct((n, cols), BF16), jax.ShapeDtypeStruct((n, LANES), F32)],
        scratch_shapes=[pltpu.VMEM((tm, D_MODEL), BF16)],
        compiler_params=_params("parallel", "arbitrary"),
        name="inproj",
    )(x, g, w_big, b_big, w_small, b_small)


def _level_matrix(L):
    t = np.arange(L)[:, None]
    s = np.arange(L)[None, :]
    x = np.maximum(t ^ s, 1)
    lv = np.floor(np.log2(x)).astype(np.int32)
    lv = np.where(t == s, -1, lv)
    return np.where(s > t, -2, lv).astype(np.int32)


def _bcast_row(x, period, r):
    L, W = x.shape
    x3 = x.reshape(L // period, period, W)
    return jnp.broadcast_to(x3[:, r:r + 1, :], x3.shape).reshape(L, W)


def _segmented_scans(la, L):
    W = la.shape[1]
    row = lax.broadcasted_iota(jnp.int32, (L, W), 0)
    fwd = [la]
    rev = [la]
    b = 1
    while b < L:
        per = 2 * b
        res = row & (per - 1)
        f, r = fwd[-1], rev[-1]
        if per < SUBLANES:
            addf = jnp.zeros_like(la)
            addr = jnp.zeros_like(la)
            for j in range(b, per):
                addf = jnp.where(res == j, pltpu.roll(f, j - (b - 1), 0), addf)
            for j in range(0, b):
                addr = jnp.where(res == j, pltpu.roll(r, L - (b - j), 0), addr)
        else:
            addf = jnp.where(res >= b, _bcast_row(f, per, b - 1), 0.0)
            addr = jnp.where(res < b, _bcast_row(r, per, b), 0.0)
        fwd.append(f + addf)
        rev.append(r + addr)
        b = per
    return fwd, rev


def _mixer_kernel(zqk_ref, zv_ref, zo_ref, zgq_ref, zgk_ref, zgv_ref, zgg_ref, zs_ref,
                  conv0_ref, c0_ref, n0_ref, m0_ref, s0_ref,
                  cw_ref, cb_ref, gm_ref, w2_ref, ba2_ref, gg_ref, lv_ref,
                  ya_ref, yb_ref, tail_ref, c_ref, n_ref, m_ref, st_ref, ext_ref, *, L):
    @pl.when(pl.program_id(1) == 0)
    def _():
        tail_ref[...] = conv0_ref[...]
        c_ref[...] = c0_ref[...]
        n_ref[...] = n0_ref[...]
        m_ref[...] = m0_ref[...]
        st_ref[...] = s0_ref[...]

    lv = lv_ref[...]
    causal = lv >= -1

    u = zqk_ref[...].astype(F32)
    ext_ref[0:SUBLANES, :] = tail_ref[0]
    ext_ref[SUBLANES:SUBLANES + L, :] = u
    acc = cb_ref[...] + cw_ref[CONV_W - 1:CONV_W, :] * u
    for w in range(CONV_W - 1):
        acc = acc + cw_ref[w:w + 1, :] * ext_ref[pl.ds(SUBLANES - (CONV_W - 1) + w, L), :]
    tail_ref[0] = ext_ref[pl.ds(L, SUBLANES), :]
    qk = acc * _sigmoid(acc)

    zs = zs_ref[...]
    row = lax.broadcasted_iota(jnp.int32, (L, LANES), 0)
    col = lax.broadcasted_iota(jnp.int32, (L, LANES), 1)
    bcs = _log_sigmoid(zs)
    sh = 1
    while sh < L:
        bcs = bcs + jnp.where(row >= sh, pltpu.roll(bcs, sh, 0), 0.0)
        sh *= 2
    gates = jnp.where(col < M_HEADS, zs, bcs)
    if L < LANES:
        gates = jnp.concatenate([gates, jnp.zeros((LANES - L, LANES), F32)], axis=0)
    gt = gates.T[:, :L]

    for h in range(M_HEADS):
        hs = slice(h * M_HD, (h + 1) * M_HD)
        ig_col = zs[:, h:h + 1]
        b_col = bcs[:, M_HEADS + h:M_HEADS + h + 1]
        ig_row = gt[h:h + 1, :]
        b_row = gt[M_HEADS + h:M_HEADS + h + 1, :]
        m_prev = m_ref[0, h][0:1, 0:1]
        logd = jnp.where(causal, b_col - b_row + ig_row, -jnp.inf)
        inter = b_col + m_prev
        m_t = jnp.maximum(inter, jnp.max(logd, axis=-1, keepdims=True))
        q_f = qk[:, hs]
        k_f = qk[:, M_DIM + h * M_HD:M_DIM + (h + 1) * M_HD] * (M_HD ** -0.5)
        q_b = q_f.astype(BF16)
        v_b = zv_ref[:, hs]
        s = _dot_nt(q_b, k_f.astype(BF16)) * jnp.exp(logd - m_t)
        w_int = jnp.exp(inter - m_t)
        c_old = c_ref[0, h]
        n_old = n_ref[0, h]
        num = _dot(s.astype(BF16), v_b) + w_int * _dot(q_b, c_old.astype(BF16))
        den = jnp.sum(s, axis=-1, keepdims=True) + w_int * jnp.sum(q_f * n_old, axis=-1, keepdims=True)
        hh = num / jnp.maximum(jnp.abs(den), jnp.exp(-m_t))
        m_new = m_t[L - 1:L, :]
        b_last = b_col[L - 1:L, :]
        w_k = jnp.exp(b_last - b_col + ig_col - m_new)
        dec = jnp.exp(b_last + m_prev - m_new)
        k_w = k_f * w_k
        c_ref[0, h] = dec * c_old + _dot_tn(k_w.astype(BF16), v_b)
        n_ref[0, h] = dec * n_old + jnp.sum(k_w, axis=0, keepdims=True)
        m_ref[0, h] = jnp.broadcast_to(m_new, (SUBLANES, LANES))
        hn = _rms(hh, gm_ref[:, hs])
        ya_ref[:, hs] = (hn * _sigmoid(zo_ref[:, hs].astype(F32))).astype(BF16)

    la = _log_sigmoid(_dot(zs.astype(BF16), w2_ref[...]) + ba2_ref[...]) * (1.0 / G_TAU)
    fwd, rev = _segmented_scans(la, L)
    nlev = len(fwd) - 1
    qg = zgq_ref[...].astype(F32) * (G_DK ** -0.5)
    kg = zgk_ref[...].astype(F32)
    qs = [(qg * jnp.exp(fwd[j])).astype(BF16) for j in range(nlev)]
    ks = [(kg * jnp.exp(rev[j] - la)).astype(BF16) for j in range(nlev)]
    q_in = (qg * jnp.exp(fwd[nlev])).astype(BF16)
    k_out = (kg * jnp.exp(rev[nlev] - la)).astype(BF16)
    q_d = qg.astype(BF16)
    k_d = kg.astype(BF16)
    dec_all = jnp.exp(fwd[nlev][L - 1:L, :])
    for h in range(G_HEADS):
        ks_ = slice(h * G_DK, (h + 1) * G_DK)
        vs_ = slice(h * G_DV, (h + 1) * G_DV)
        att = jnp.where(lv == -1, _dot_nt(q_d[:, ks_], k_d[:, ks_]), 0.0)
        for j in range(nlev):
            att = jnp.where(lv == j, _dot_nt(qs[j][:, ks_], ks[j][:, ks_]), att)
        v_b = zgv_ref[:, vs_]
        st_old = st_ref[0, h]
        o = _dot(att.astype(BF16), v_b) + _dot_nt(q_in[:, ks_], st_old.astype(BF16))
        st_ref[0, h] = st_old * dec_all[:, ks_] + _dot_tn(v_b, k_out[:, ks_])
        on = _rms(o, gg_ref[:, vs_])
        g = zgg_ref[:, vs_].astype(F32)
        yb_ref[:, vs_] = (on * (g * _sigmoid(g))).astype(BF16)


def _mixers(z, zs, conv0, c0, n0, m0, st0, cw, cb, gm, w2, ba2, gg, batch, T, L):
    nc = T // L
    lv = jnp.asarray(_level_matrix(L))
    row_blk = lambda width, cidx: pl.BlockSpec((L, width), lambda b, c: (b * nc + c, cidx))
    state = lambda *shape: pl.BlockSpec((1,) + shape, lambda b, c: (b,) + (0,) * len(shape))
    const = lambda *shape: pl.BlockSpec(shape, lambda b, c: (0,) * len(shape))
    n_tok = batch * T
    return pl.pallas_call(
        functools.partial(_mixer_kernel, L=L),
        grid=(batch, nc),
        in_specs=[
            row_blk(2 * M_DIM, 0),
            row_blk(M_DIM, 2),
            row_blk(M_DIM, 3),
            row_blk(G_KDIM, 8),
            row_blk(G_KDIM, 9),
            row_blk(G_VDIM, 5),
            row_blk(G_VDIM, 6),
            pl.BlockSpec((L, LANES), lambda b, c: (b * nc + c, 0)),
            state(SUBLANES, 2 * M_DIM),
            state(M_HEADS, M_HD, M_HD),
            state(M_HEADS, 1, M_HD),
            state(M_HEADS, SUBLANES, LANES),
            state(G_HEADS, G_DV, G_DK),
            const(CONV_W, 2 * M_DIM),
            const(1, 2 * M_DIM),
            const(1, M_DIM),
            const(LANES, G_KDIM),
            const(1, G_KDIM),
            const(1, G_VDIM),
            const(L, L),
        ],
        out_specs=[
            pl.BlockSpec((L, M_DIM), lambda b, c: (b * nc + c, 0)),
            pl.BlockSpec((L, G_VDIM), lambda b, c: (b * nc + c, 0)),
            state(SUBLANES, 2 * M_DIM),
            state(M_HEADS, M_HD, M_HD),
            state(M_HEADS, 1, M_HD),
            state(M_HEADS, SUBLANES, LANES),
            state(G_HEADS, G_DV, G_DK),
        ],
        out_shape=[
            jax.ShapeDtypeStruct((n_tok, M_DIM), BF16),
            jax.ShapeDtypeStruct((n_tok, G_VDIM), BF16),
            jax.ShapeDtypeStruct((batch, SUBLANES, 2 * M_DIM), F32),
            jax.ShapeDtypeStruct((batch, M_HEADS, M_HD, M_HD), F32),
            jax.ShapeDtypeStruct((batch, M_HEADS, 1, M_HD), F32),
            jax.ShapeDtypeStruct((batch, M_HEADS, SUBLANES, LANES), F32),
            jax.ShapeDtypeStruct((batch, G_HEADS, G_DV, G_DK), F32),
        ],
        scratch_shapes=[pltpu.VMEM((L + SUBLANES, 2 * M_DIM), F32)],
        compiler_params=_params("parallel", "arbitrary"),
        name="mixers",
    )(z, z, z, z, z, z, z, zs, conv0, c0, n0, m0, st0, cw, cb, gm, w2, ba2, gg, lv)


def _outproj_kernel(ya_ref, yb_ref, ua_ref, ub_ref, x_ref, wpa_ref, wpb_ref, wout_ref, gffn_ref,
                    wr_ref, br_ref, tril_ref, carry_in_ref,
                    x1_ref, xn_ref, eo_ref, wo_ref, carry_ref, *, tm):
    @pl.when(pl.program_id(0) == 0)
    def _():
        carry_ref[...] = carry_in_ref[...]

    a = _dot(ya_ref[...], wpa_ref[...])
    b = _dot(yb_ref[...], wpb_ref[...])
    merged = _sigmoid(ua_ref[...].astype(F32)) * a + _sigmoid(ub_ref[...].astype(F32)) * b
    x1 = x_ref[...] + _dot(merged.astype(BF16), wout_ref[...])
    x1_ref[...] = x1
    xn = _rms(x1, gffn_ref[...])
    for j in range(ROW_CHUNKS):
        xn_ref[pl.ds(j, tm, stride=ROW_CHUNKS), :] = xn[:, j * LANES:(j + 1) * LANES]

    logits = _dot(xn.astype(BF16), wr_ref[...]) + br_ref[...]
    lane = lax.broadcasted_iota(jnp.int32, (tm, LANES), 1).astype(F32)
    vals, idxs = [], []
    l = logits
    for _ in range(TOP_K):
        mx = jnp.max(l, axis=-1, keepdims=True)
        ix = jnp.min(jnp.where(l == mx, lane, float(LANES)), axis=-1, keepdims=True)
        vals.append(mx)
        idxs.append(ix)
        l = jnp.where(lane == ix, -jnp.inf, l)
    ev = [jnp.exp(v - vals[0]) for v in vals]
    den = ev[0] + ev[1] + ev[2] + ev[3]

    cnt = jnp.zeros((tm, LANES), F32)
    for ix in idxs:
        cnt = cnt + jnp.where(lane == ix, 1.0, 0.0)
    pre = _dot(tril_ref[...], cnt.astype(BF16)) + carry_ref[0:1, :]
    eo = jnp.zeros((tm, LANES), F32)
    wo = jnp.zeros((tm, LANES), F32)
    for k in range(TOP_K):
        rank = jnp.sum(jnp.where(lane == idxs[k], pre, 0.0), axis=-1, keepdims=True)
        eo = jnp.where(lane == float(k), idxs[k], eo)
        eo = jnp.where(lane == float(TOP_K + k), rank, eo)
        wo = jnp.where(lane == float(k), ev[k] / den, wo)
    eo_ref[...] = eo.T[0:SUBLANES, :].astype(jnp.int32)
    wo_ref[...] = wo
    carry_ref[...] = carry_ref[...] + jnp.sum(cnt, axis=0, keepdims=True)


def _outproj(ya, yb, z, x, wpa, wpb, wout, gffn, wr, br, carry, tm):
    n = x.shape[0]
    tril = jnp.asarray(np.tril(np.ones((tm, tm), np.float32), -1), BF16)
    tok = lambda width, cidx=0: pl.BlockSpec((tm, width), lambda i: (i, cidx))
    const = lambda *shape: pl.BlockSpec(shape, lambda i: (0,) * len(shape))
    return pl.pallas_call(
        functools.partial(_outproj_kernel, tm=tm),
        grid=(n // tm,),
        in_specs=[
            tok(M_DIM), tok(G_VDIM), tok(D_MODEL, 7), tok(D_MODEL, 8), tok(D_MODEL),
            const(M_DIM, D_MODEL), const(G_VDIM, D_MODEL), const(D_MODEL, D_MODEL), const(1, D_MODEL),
            const(D_MODEL, LANES), const(1, LANES), const(tm, tm), const(SUBLANES, LANES),
        ],
        out_specs=[
            tok(D_MODEL),
            pl.BlockSpec((tm * ROW_CHUNKS, LANES), lambda i: (i, 0)),
            pl.BlockSpec((SUBLANES, tm), lambda i: (0, i)), tok(LANES), const(SUBLANES, LANES),
        ],
        out_shape=[
            jax.ShapeDtypeStruct((n, D_MODEL), F32),
            jax.ShapeDtypeStruct((n * ROW_CHUNKS, LANES), F32),
            jax.ShapeDtypeStruct((SUBLANES, n), jnp.int32),
            jax.ShapeDtypeStruct((n, LANES), F32),
            jax.ShapeDtypeStruct((SUBLANES, LANES), F32),
        ],
        compiler_params=_params("arbitrary"),
        name="outproj_router",
    )(ya, yb, z, z, x, wpa, wpb, wout, gffn, wr, br, tril, carry)


def _row_copy(src, src_row, dst, dst_row, sem):
    return pltpu.make_async_copy(
        src.at[pl.ds(pl.multiple_of(src_row * ROW_CHUNKS, ROW_CHUNKS), ROW_CHUNKS)],
        dst.at[pl.ds(pl.multiple_of(dst_row * ROW_CHUNKS, ROW_CHUNKS), ROW_CHUNKS)], sem)


def _dispatch_kernel(pos_p_ref, pos_s_ref, fill_ref, xn_p_ref, xn_s_ref, xs_ref, zeros_ref, zsem, sem,
                     *, tm, n_p, n_s, n_blocks):
    i = pl.program_id(0)
    tiles_p = n_p // tm

    def scatter(xn_ref, pos_ref, base, n):
        def issue(r, carry):
            for k in range(TOP_K):
                _row_copy(xn_ref, r, xs_ref, pos_ref[k * n + base + r], sem).start()
            return carry

        lax.fori_loop(0, tm, issue, 0)
        for _ in range(TOP_K):
            pltpu.make_async_copy(xn_ref, xs_ref.at[pl.ds(0, tm * ROW_CHUNKS)], sem).wait()

    @pl.when(i < tiles_p)
    def _():
        scatter(xn_p_ref, pos_p_ref, i * tm, n_p)

    @pl.when(i >= tiles_p)
    def _():
        scatter(xn_s_ref, pos_s_ref, (i - tiles_p) * tm, n_s)

    @pl.when(i == 0)
    def _():
        zeros_ref[...] = jnp.zeros_like(zeros_ref)
        bits = [1 << b for b in range(MOE_ROWS.bit_length() - 1)]

        def zero_copy(off, nrows):
            return pltpu.make_async_copy(
                zeros_ref.at[pl.ds(0, nrows * ROW_CHUNKS)],
                xs_ref.at[pl.ds(pl.multiple_of(off * ROW_CHUNKS, ROW_CHUNKS), nrows * ROW_CHUNKS)], zsem)

        def expert(e, carry):
            npad = fill_ref[N_EXPERTS + e]
            for wait in (False, True):
                off = fill_ref[e]
                for bit in bits:
                    @pl.when((npad & bit) != 0)
                    def _():
                        zero_copy(off, bit).wait() if wait else zero_copy(off, bit).start()
                    off = off + (npad & bit)
            return carry

        lax.fori_loop(0, N_EXPERTS, expert, 0)

        def block(b, carry):
            zero_copy(b * MOE_ROWS, MOE_ROWS).start()
            zero_copy(b * MOE_ROWS, MOE_ROWS).wait()
            return carry

        lax.fori_loop(fill_ref[2 * N_EXPERTS], n_blocks, block, 0)


def _dispatch(pos_p, pos_s, fill, xn_p, xn_s, n_blocks, tm):
    n_p, n_s = xn_p.shape[0] // ROW_CHUNKS, xn_s.shape[0] // ROW_CHUNKS
    tiles_p, tiles_s = n_p // tm, n_s // tm
    blk = (tm * ROW_CHUNKS, LANES)
    return pl.pallas_call(
        functools.partial(_dispatch_kernel, tm=tm, n_p=n_p, n_s=n_s, n_blocks=n_blocks),
        grid_spec=pltpu.PrefetchScalarGridSpec(
            num_scalar_prefetch=3,
            grid=(tiles_p + tiles_s,),
            in_specs=[
                pl.BlockSpec(blk, lambda i, *_: (jnp.minimum(i, tiles_p - 1), 0)),
                pl.BlockSpec(blk, lambda i, *_: (jnp.maximum(i - tiles_p, 0), 0)),
            ],
            out_specs=pl.BlockSpec(memory_space=pl.ANY),
            scratch_shapes=[pltpu.VMEM((MOE_ROWS * ROW_CHUNKS, LANES), F32), pltpu.SemaphoreType.DMA(()),
                            pltpu.SemaphoreType.DMA(())],
        ),
        out_shape=jax.ShapeDtypeStruct((n_blocks * MOE_ROWS * ROW_CHUNKS, LANES), F32),
        compiler_params=_params("arbitrary"),
        name="dispatch",
    )(pos_p, pos_s, fill, xn_p, xn_s)


def _expert_kernel(be_ref, nused_ref, xs_ref, wgu_ref, bgu_ref, wd_ref, bd_ref, y_ref, wgu_bf, wd_bf, *, rows):
    i = pl.program_id(0)
    changed = jnp.logical_or(i == 0, be_ref[i] != be_ref[jnp.maximum(i - 1, 0)])

    @pl.when(changed)
    def _():
        wgu_bf[...] = wgu_ref[...].astype(BF16)
        wd_bf[...] = wd_ref[...].astype(BF16)

    @pl.when(i < nused_ref[0])
    def _():
        x = jnp.concatenate([xs_ref[pl.ds(j, rows, stride=ROW_CHUNKS), :] for j in range(ROW_CHUNKS)], axis=-1)
        zz = _dot(x.astype(BF16), wgu_bf[...]) + bgu_ref[...]
        g = jnp.minimum(zz[:, :D_FF], SWIGLU_LIMIT)
        u = jnp.clip(zz[:, D_FF:], -SWIGLU_LIMIT, SWIGLU_LIMIT)
        act = g * _sigmoid(SWIGLU_ALPHA * g) * (u + 1.0)
        y = _dot(act.astype(BF16), wd_bf[...]) + bd_ref[...]
        for j in range(ROW_CHUNKS):
            y_ref[pl.ds(j, rows, stride=ROW_CHUNKS), :] = y[:, j * LANES:(j + 1) * LANES]

    @pl.when(i >= nused_ref[0])
    def _():
        y_ref[...] = jnp.zeros_like(y_ref)


def _experts(block_e, n_used, xs, wgu, bgu, wd, bd, rows):
    nb = xs.shape[0] // (rows * ROW_CHUNKS)
    return pl.pallas_call(
        functools.partial(_expert_kernel, rows=rows),
        grid_spec=pltpu.PrefetchScalarGridSpec(
            num_scalar_prefetch=2,
            grid=(nb,),
            in_specs=[
                pl.BlockSpec((rows * ROW_CHUNKS, LANES), lambda i, be, nu: (i, 0)),
                pl.BlockSpec((None, D_MODEL, 2 * D_FF), lambda i, be, nu: (be[i], 0, 0)),
                pl.BlockSpec((None, 1, 2 * D_FF), lambda i, be, nu: (be[i], 0, 0)),
                pl.BlockSpec((None, D_FF, D_MODEL), lambda i, be, nu: (be[i], 0, 0)),
                pl.BlockSpec((None, 1, D_MODEL), lambda i, be, nu: (be[i], 0, 0)),
            ],
            out_specs=pl.BlockSpec((rows * ROW_CHUNKS, LANES), lambda i, be, nu: (i, 0)),
            scratch_shapes=[pltpu.VMEM((D_MODEL, 2 * D_FF), BF16), pltpu.VMEM((D_FF, D_MODEL), BF16)],
        ),
        out_shape=jax.ShapeDtypeStruct(xs.shape, F32),
        compiler_params=_params("arbitrary"),
        name="experts",
    )(block_e, n_used, xs, wgu, bgu, wd, bd)


def _combine_kernel(pos_ref, ye_ref, x1_ref, wo_ref, p_ref, wproj_ref, wgate_ref, gple_ref, gfin_ref,
                    y_ref, rows_scr, sem, *, tm, n):
    i = pl.program_id(0)
    nt = n // tm

    def gather(tile, slot):
        def issue(r, carry):
            for k in range(TOP_K):
                _row_copy(ye_ref, pos_ref[k * n + tile * tm + r], rows_scr, (slot * TOP_K + k) * tm + r,
                          sem.at[slot]).start()
            return carry
        lax.fori_loop(0, tm, issue, 0)

    @pl.when(i == 0)
    def _():
        gather(0, 0)

    @pl.when(i + 1 < nt)
    def _():
        gather(i + 1, (i + 1) % 2)

    slot = i % 2
    seg = tm * ROW_CHUNKS
    for k in range(TOP_K):
        off = pl.multiple_of((slot * TOP_K + k) * seg, seg)
        pltpu.make_async_copy(ye_ref.at[pl.ds(0, seg)], rows_scr.at[pl.ds(off, seg)], sem.at[slot]).wait()

    wo = wo_ref[...]
    moe = jnp.zeros((tm, D_MODEL), F32)
    for k in range(TOP_K):
        off = pl.multiple_of((slot * TOP_K + k) * seg, seg)
        rows = jnp.concatenate(
            [rows_scr[pl.ds(off + j, tm, stride=ROW_CHUNKS), :] for j in range(ROW_CHUNKS)], axis=-1)
        moe = moe + rows * wo[:, k:k + 1]
    x2 = x1_ref[...] + moe
    gate = _sigmoid(_dot(_rms(x2, gple_ref[...]).astype(BF16), wgate_ref[...]))
    x3 = x2 + _dot(p_ref[...].astype(BF16), wproj_ref[...]) * gate
    y_ref[...] = _rms(x3, gfin_ref[...])


def _combine(pos_flat, ye, x1, wo, p, wproj, wgate, gple, gfin, tm):
    n = x1.shape[0]
    tok = lambda width: pl.BlockSpec((tm, width), lambda i, pos: (i, 0))
    const = lambda *shape: pl.BlockSpec(shape, lambda i, pos: (0,) * len(shape))
    return pl.pallas_call(
        functools.partial(_combine_kernel, tm=tm, n=n),
        grid_spec=pltpu.PrefetchScalarGridSpec(
            num_scalar_prefetch=1,
            grid=(n // tm,),
            in_specs=[
                pl.BlockSpec(memory_space=pl.ANY),
                tok(D_MODEL), tok(LANES), tok(PLE_DIM),
                const(PLE_DIM, D_MODEL), const(D_MODEL, D_MODEL), const(1, D_MODEL), const(1, D_MODEL),
            ],
            out_specs=tok(D_MODEL),
            scratch_shapes=[pltpu.VMEM((2 * TOP_K * tm * ROW_CHUNKS, LANES), F32), pltpu.SemaphoreType.DMA((2,))],
        ),
        out_shape=jax.ShapeDtypeStruct((n, D_MODEL), F32),
        compiler_params=_params("arbitrary"),
        name="combine_ple",
    )(pos_flat, ye, x1, wo, p, wproj, wgate, gple, gfin)


def _tile(n, pref):
    t = pref
    while n % t:
        t //= 2
    return t


def _seg(a, i):
    return a[..., _OFF[i]:_OFF[i + 1]]


def kernel(x_prompt, x_sample, state_conv, state_mlstm_C, state_mlstm_n, state_mlstm_m, state_gla_S, p_prompt, p_sample, g_mix, w_in, b_in, conv_w, conv_b, g_mnorm, w_a2, b_a2, g_gnorm, w_pa, w_pb, w_out, g_ffn, w_router, b_router, w_gate_up, b_gate_up, w_down, b_down, g_ple, w_ple_proj, w_ple_gate, g_final):
    assert g_mix.shape[0] == 1, "single-layer kernel"
    bp, tp, _ = x_prompt.shape
    bs, ts, _ = x_sample.shape

    big = (0, 1, 2, 5, 6, 7, 8, 10, 11, 12)
    w_big = jnp.concatenate([_seg(w_in[0], i) for i in big], axis=-1).astype(BF16)
    b_big = jnp.concatenate([_seg(b_in[0], i) for i in big], axis=-1)[None, :]
    small = (3, 4, 9)
    n_small = sum(SPLITS[i] for i in small)
    w_small = jnp.pad(jnp.concatenate([_seg(w_in[0], i) for i in small], axis=-1),
                      ((0, 0), (0, LANES - n_small))).astype(BF16)
    b_small = jnp.pad(jnp.concatenate([_seg(b_in[0], i) for i in small], axis=-1), (0, LANES - n_small))[None, :]
    w2 = jnp.pad(w_a2[0], ((2 * M_HEADS, LANES - 2 * M_HEADS - G_RANK), (0, 0))).astype(BF16)
    wr = jnp.pad(w_router[0], ((0, 0), (0, LANES - N_EXPERTS))).astype(BF16)
    br = jnp.pad(b_router[0], (0, LANES - N_EXPERTS), constant_values=NEG_BIG)[None, :]
    wpa, wpb, wout = w_pa[0].astype(BF16), w_pb[0].astype(BF16), w_out[0].astype(BF16)
    wproj, wgate = w_ple_proj[0].astype(BF16), w_ple_gate[0].astype(BF16)
    row = lambda a: a.reshape(1, -1)

    def mix_path(x3d, conv0, c0, n0, m0, s0):
        batch, T, _ = x3d.shape
        x = x3d.reshape(batch * T, D_MODEL)
        z, zs = _inproj(x, row(g_mix[0]), w_big, b_big, w_small, b_small, _tile(batch * T, 1024))
        L = min(MIX_CHUNK, T)
        conv0p = jnp.pad(conv0, ((0, 0), (SUBLANES - (CONV_W - 1), 0), (0, 0)))
        m0b = jnp.broadcast_to(m0[:, :, None, None], (batch, M_HEADS, SUBLANES, LANES))
        ya, yb, tail, c1, n1, m1, st1 = _mixers(
            z, zs, conv0p, c0, n0[:, :, None, :], m0b, jnp.swapaxes(s0, -1, -2),
            conv_w[0], row(conv_b[0]), row(g_mnorm[0]), w2, row(b_a2[0]), row(g_gnorm[0]), batch, T, L)
        new_state = (tail[:, SUBLANES - (CONV_W - 1):, :], c1, n1[:, :, 0, :], m1[:, :, 0, 0],
                     jnp.swapaxes(st1, -1, -2))
        return x, z, ya, yb, new_state

    zero_state = (jnp.zeros((bp, CONV_W - 1, 2 * M_DIM), F32), jnp.zeros((bp, M_HEADS, M_HD, M_HD), F32),
                  jnp.zeros((bp, M_HEADS, M_HD), F32), jnp.zeros((bp, M_HEADS), F32),
                  jnp.zeros((bp, G_HEADS, G_DK, G_DV), F32))
    xp, zp, yap, ybp, st_p = mix_path(x_prompt, *zero_state)
    xs_, zs_, yas, ybs, st_s = mix_path(x_sample, state_conv[0], state_mlstm_C[0], state_mlstm_n[0],
                                        state_mlstm_m[0], state_gla_S[0])

    n_p, n_s = xp.shape[0], xs_.shape[0]
    tm_p, tm_s = _tile(n_p, 512), _tile(n_s, 512)
    carry0 = jnp.zeros((SUBLANES, LANES), F32)
    x1p, xnp_, eop, wop, carry1 = _outproj(yap, ybp, zp, xp, wpa, wpb, wout, row(g_ffn[0]), wr, br, carry0, tm_p)
    x1s, xns, eos, wos, carry2 = _outproj(yas, ybs, zs_, xs_, wpa, wpb, wout, row(g_ffn[0]), wr, br, carry1, tm_s)

    counts = carry2[0, :N_EXPERTS].astype(jnp.int32)
    padded = (counts + MOE_ROWS - 1) // MOE_ROWS * MOE_ROWS
    pend = jnp.cumsum(padded)
    pstart = pend - padded
    nk = (n_p + n_s) * TOP_K
    n_blocks = (nk + N_EXPERTS * (MOE_ROWS - 1) + MOE_ROWS - 1) // MOE_ROWS
    starts = jnp.arange(n_blocks, dtype=jnp.int32) * MOE_ROWS
    block_e = jnp.minimum(jnp.sum((pend[None, :] <= starts[:, None]).astype(jnp.int32), axis=1), N_EXPERTS - 1)
    n_used = (pend[-1:] // MOE_ROWS).astype(jnp.int32)
    fill = jnp.concatenate([pstart + counts, padded - counts, n_used]).astype(jnp.int32)

    def dest(eo):
        e, rank = eo[:TOP_K], eo[TOP_K:2 * TOP_K]
        onehot = e[..., None] == jnp.arange(N_EXPERTS, dtype=jnp.int32)
        return (jnp.sum(jnp.where(onehot, pstart, 0), axis=-1) + rank).reshape(-1)

    pos_p, pos_s = dest(eop), dest(eos)

    xs_sorted = _dispatch(pos_p, pos_s, fill, xnp_, xns, n_blocks, _tile(n_s, 256))
    ye = _experts(block_e, n_used, xs_sorted, w_gate_up[0], b_gate_up[0][:, None, :], w_down[0],
                  b_down[0][:, None, :], MOE_ROWS)
    fin = lambda pos, x1, wo, p, tm: _combine(pos, ye, x1, wo, p.reshape(-1, PLE_DIM), wproj, wgate,
                                              row(g_ple[0]), row(g_final), tm)
    y_p = fin(pos_p, x1p, wop, p_prompt[0], _tile(n_p, 256)).reshape(bp, tp, D_MODEL)
    y_s = fin(pos_s, x1s, wos, p_sample[0], _tile(n_s, 256)).reshape(bs, ts, D_MODEL)

    lead = lambda a: a[None]
    return (y_p, y_s) + tuple(lead(a) for a in st_p) + tuple(lead(a) for a in st_s)
```

```python
import functools

import numpy as np
import jax
import jax.numpy as jnp
from jax import lax
from jax.experimental import pallas as pl
from jax.experimental.pallas import tpu as pltpu

F32 = jnp.float32
BF16 = jnp.bfloat16

D_MODEL = 1024
EPS = 1e-6
M_HEADS = 4
M_DIM = D_MODEL
M_HD = M_DIM // M_HEADS
CONV_W = 4
G_HEADS = 4
G_KDIM = D_MODEL // 2
G_VDIM = D_MODEL
G_DK = G_KDIM // G_HEADS
G_DV = G_VDIM // G_HEADS
G_RANK = 16
G_TAU = 16.0
N_EXPERTS = 32
TOP_K = 4
D_FF = D_MODEL
SWIGLU_LIMIT = 7.0
SWIGLU_ALPHA = 1.702
PLE_DIM = 256
SPLITS = (M_DIM, M_DIM, M_DIM, M_HEADS, M_HEADS, M_DIM, G_KDIM, G_KDIM, G_VDIM, G_RANK, G_VDIM, D_MODEL, D_MODEL)
_OFF = np.concatenate([[0], np.cumsum(SPLITS)]).tolist()

LANES = 128
SUBLANES = 8
ROW_CHUNKS = D_MODEL // LANES
VMEM_LIMIT = 56 * 1024 * 1024
MIX_CHUNK = 256
MOE_ROWS = 256
NEG_BIG = -1e30


def _sigmoid(x):
    return 1.0 / (1.0 + jnp.exp(-x))


def _log_sigmoid(x):
    return jnp.minimum(x, 0.0) - jnp.log1p(jnp.exp(-jnp.abs(x)))


def _rms(x, g):
    return x * lax.rsqrt(jnp.mean(x * x, axis=-1, keepdims=True) + EPS) * g


def _dot(a, b):
    return jnp.dot(a, b, preferred_element_type=F32)


def _dot_nt(a, b):
    return lax.dot_general(a, b, (((1,), (1,)), ((), ())), preferred_element_type=F32)


def _dot_tn(a, b):
    return lax.dot_general(a, b, (((0,), (0,)), ((), ())), preferred_element_type=F32)


def _params(*sem):
    return pltpu.CompilerParams(dimension_semantics=sem, vmem_limit_bytes=VMEM_LIMIT)


def _inproj_kernel(x_ref, g_ref, w_ref, b_ref, ws_ref, bs_ref, z_ref, zs_ref, h_scr):
    @pl.when(pl.program_id(1) == 0)
    def _():
        hb = _rms(x_ref[...], g_ref[...]).astype(BF16)
        h_scr[...] = hb
        zs_ref[...] = _dot(hb, ws_ref[...]) + bs_ref[...]

    z_ref[...] = (_dot(h_scr[...], w_ref[...]) + b_ref[...]).astype(BF16)


def _inproj(x, g, w_big, b_big, w_small, b_small, tm, tn=1024):
    n = x.shape[0]
    cols = w_big.shape[1]
    return pl.pallas_call(
        _inproj_kernel,
        grid=(n // tm, cols // tn),
        in_specs=[
            pl.BlockSpec((tm, D_MODEL), lambda i, j: (i, 0)),
            pl.BlockSpec((1, D_MODEL), lambda i, j: (0, 0)),
            pl.BlockSpec((D_MODEL, tn), lambda i, j: (0, j)),
            pl.BlockSpec((1, tn), lambda i, j: (0, j)),
            pl.BlockSpec((D_MODEL, LANES), lambda i, j: (0, 0)),
            pl.BlockSpec((1, LANES), lambda i, j: (0, 0)),
        ],
        out_specs=[
            pl.BlockSpec((tm, tn), lambda i, j: (i, j)),
            pl.BlockSpec((tm, LANES), lambda i, j: (i, 0)),
        ],
        out_shape=[jax.ShapeDtypeStruct((n, cols), BF16), jax.ShapeDtypeStruct((n, LANES), F32)],
        scratch_shapes=[pltpu.VMEM((tm, D_MODEL), BF16)],
        compiler_params=_params("parallel", "arbitrary"),
        name="inproj",
    )(x, g, w_big, b_big, w_small, b_small)


def _level_matrix(L):
    t = np.arange(L)[:, None]
    s = np.arange(L)[None, :]
    x = np.maximum(t ^ s, 1)
    lv = np.floor(np.log2(x)).astype(np.int32)
    lv = np.where(t == s, -1, lv)
    return np.where(s > t, -2, lv).astype(np.int32)


def _bcast_row(x, period, r):
    L, W = x.shape
    x3 = x.reshape(L // period, period, W)
    return jnp.broadcast_to(x3[:, r:r + 1, :], x3.shape).reshape(L, W)


def _segmented_scans(la, L):
    W = la.shape[1]
    row = lax.broadcasted_iota(jnp.int32, (L, W), 0)
    fwd = [la]
    rev = [la]
    b = 1
    while b < L:
        per = 2 * b
        res = row & (per - 1)
        f, r = fwd[-1], rev[-1]
        if per < SUBLANES:
            addf = jnp.zeros_like(la)
            addr = jnp.zeros_like(la)
            for j in range(b, per):
                addf = jnp.where(res == j, pltpu.roll(f, j - (b - 1), 0), addf)
            for j in range(0, b):
                addr = jnp.where(res == j, pltpu.roll(r, L - (b - j), 0), addr)
        else:
            addf = jnp.where(res >= b, _bcast_row(f, per, b - 1), 0.0)
            addr = jnp.where(res < b, _bcast_row(r, per, b), 0.0)
        fwd.append(f + addf)
        rev.append(r + addr)
        b = per
    return fwd, rev


def _mixer_kernel(zqk_ref, zv_ref, zo_ref, zgq_ref, zgk_ref, zgv_ref, zgg_ref, zs_ref,
                  conv0_ref, c0_ref, n0_ref, m0_ref, s0_ref,
                  cw_ref, cb_ref, gm_ref, w2_ref, ba2_ref, gg_ref, lv_ref,
                  ya_ref, yb_ref, tail_ref, c_ref, n_ref, m_ref, st_ref, ext_ref, *, L):
    @pl.when(pl.program_id(1) == 0)
    def _():
        tail_ref[...] = conv0_ref[...]
        c_ref[...] = c0_ref[...]
        n_ref[...] = n0_ref[...]
        m_ref[...] = m0_ref[...]
        st_ref[...] = s0_ref[...]

    lv = lv_ref[...]
    causal = lv >= -1

    u = zqk_ref[...].astype(F32)
    ext_ref[0:SUBLANES, :] = tail_ref[0]
    ext_ref[SUBLANES:SUBLANES + L, :] = u
    acc = cb_ref[...] + cw_ref[CONV_W - 1:CONV_W, :] * u
    for w in range(CONV_W - 1):
        acc = acc + cw_ref[w:w + 1, :] * ext_ref[pl.ds(SUBLANES - (CONV_W - 1) + w, L), :]
    tail_ref[0] = ext_ref[pl.ds(L, SUBLANES), :]
    qk = acc * _sigmoid(acc)

    zs = zs_ref[...]
    row = lax.broadcasted_iota(jnp.int32, (L, LANES), 0)
    col = lax.broadcasted_iota(jnp.int32, (L, LANES), 1)
    bcs = _log_sigmoid(zs)
    sh = 1
    while sh < L:
        bcs = bcs + jnp.where(row >= sh, pltpu.roll(bcs, sh, 0), 0.0)
        sh *= 2
    gates = jnp.where(col < M_HEADS, zs, bcs)
    if L < LANES:
        gates = jnp.concatenate([gates, jnp.zeros((LANES - L, LANES), F32)], axis=0)
    gt = gates.T[:, :L]

    for h in range(M_HEADS):
        hs = slice(h * M_HD, (h + 1) * M_HD)
        ig_col = zs[:, h:h + 1]
        b_col = bcs[:, M_HEADS + h:M_HEADS + h + 1]
        ig_row = gt[h:h + 1, :]
        b_row = gt[M_HEADS + h:M_HEADS + h + 1, :]
        m_prev = m_ref[0, h][0:1, 0:1]
        logd = jnp.where(causal, b_col - b_row + ig_row, -jnp.inf)
        inter = b_col + m_prev
        m_t = jnp.maximum(inter, jnp.max(logd, axis=-1, keepdims=True))
        q_f = qk[:, hs]
        k_f = qk[:, M_DIM + h * M_HD:M_DIM + (h + 1) * M_HD] * (M_HD ** -0.5)
        q_b = q_f.astype(BF16)
        v_b = zv_ref[:, hs]
        s = _dot_nt(q_b, k_f.astype(BF16)) * jnp.exp(logd - m_t)
        w_int = jnp.exp(inter - m_t)
        c_old = c_ref[0, h]
        n_old = n_ref[0, h]
        num = _dot(s.astype(BF16), v_b) + w_int * _dot(q_b, c_old.astype(BF16))
        den = jnp.sum(s, axis=-1, keepdims=True) + w_int * jnp.sum(q_f * n_old, axis=-1, keepdims=True)
        hh = num / jnp.maximum(jnp.abs(den), jnp.exp(-m_t))
        m_new = m_t[L - 1:L, :]
        b_last = b_col[L - 1:L, :]
        w_k = jnp.exp(b_last - b_col + ig_col - m_new)
        dec = jnp.exp(b_last + m_prev - m_new)
        k_w = k_f * w_k
        c_ref[0, h] = dec * c_old + _dot_tn(k_w.astype(BF16), v_b)
        n_ref[0, h] = dec * n_old + jnp.sum(k_w, axis=0, keepdims=True)
        m_ref[0, h] = jnp.broadcast_to(m_new, (SUBLANES, LANES))
        hn = _rms(hh, gm_ref[:, hs])
        ya_ref[:, hs] = (hn * _sigmoid(zo_ref[:, hs].astype(F32))).astype(BF16)

    la = _log_sigmoid(_dot(zs.astype(BF16), w2_ref[...]) + ba2_ref[...]) * (1.0 / G_TAU)
    fwd, rev = _segmented_scans(la, L)
    nlev = len(fwd) - 1
    qg = zgq_ref[...].astype(F32) * (G_DK ** -0.5)
    kg = zgk_ref[...].astype(F32)
    qs = [(qg * jnp.exp(fwd[j])).astype(BF16) for j in range(nlev)]
    ks = [(kg * jnp.exp(rev[j] - la)).astype(BF16) for j in range(nlev)]
    q_in = (qg * jnp.exp(fwd[nlev])).astype(BF16)
    k_out = (kg * jnp.exp(rev[nlev] - la)).astype(BF16)
    q_d = qg.astype(BF16)
    k_d = kg.astype(BF16)
    dec_all = jnp.exp(fwd[nlev][L - 1:L, :])
    for h in range(G_HEADS):
        ks_ = slice(h * G_DK, (h + 1) * G_DK)
        vs_ = slice(h * G_DV, (h + 1) * G_DV)
        att = jnp.where(lv == -1, _dot_nt(q_d[:, ks_], k_d[:, ks_]), 0.0)
        for j in range(nlev):
            att = jnp.where(lv == j, _dot_nt(qs[j][:, ks_], ks[j][:, ks_]), att)
        v_b = zgv_ref[:, vs_]
        st_old = st_ref[0, h]
        o = _dot(att.astype(BF16), v_b) + _dot_nt(q_in[:, ks_], st_old.astype(BF16))
        st_ref[0, h] = st_old * dec_all[:, ks_] + _dot_tn(v_b, k_out[:, ks_])
        on = _rms(o, gg_ref[:, vs_])
        g = zgg_ref[:, vs_].astype(F32)
        yb_ref[:, vs_] = (on * (g * _sigmoid(g))).astype(BF16)


def _mixers(z, zs, conv0, c0, n0, m0, st0, cw, cb, gm, w2, ba2, gg, batch, T, L):
    nc = T // L
    lv = jnp.asarray(_level_matrix(L))
    row_blk = lambda width, cidx: pl.BlockSpec((L, width), lambda b, c: (b * nc + c, cidx))
    state = lambda *shape: pl.BlockSpec((1,) + shape, lambda b, c: (b,) + (0,) * len(shape))
    const = lambda *shape: pl.BlockSpec(shape, lambda b, c: (0,) * len(shape))
    n_tok = batch * T
    return pl.pallas_call(
        functools.partial(_mixer_kernel, L=L),
        grid=(batch, nc),
        in_specs=[
            row_blk(2 * M_DIM, 0),
            row_blk(M_DIM, 2),
            row_blk(M_DIM, 3),
            row_blk(G_KDIM, 8),
            row_blk(G_KDIM, 9),
            row_blk(G_VDIM, 5),
            row_blk(G_VDIM, 6),
            pl.BlockSpec((L, LANES), lambda b, c: (b * nc + c, 0)),
            state(SUBLANES, 2 * M_DIM),
            state(M_HEADS, M_HD, M_HD),
            state(M_HEADS, 1, M_HD),
            state(M_HEADS, SUBLANES, LANES),
            state(G_HEADS, G_DV, G_DK),
            const(CONV_W, 2 * M_DIM),
            const(1, 2 * M_DIM),
            const(1, M_DIM),
            const(LANES, G_KDIM),
            const(1, G_KDIM),
            const(1, G_VDIM),
            const(L, L),
        ],
        out_specs=[
            pl.BlockSpec((L, M_DIM), lambda b, c: (b * nc + c, 0)),
            pl.BlockSpec((L, G_VDIM), lambda b, c: (b * nc + c, 0)),
            state(SUBLANES, 2 * M_DIM),
            state(M_HEADS, M_HD, M_HD),
            state(M_HEADS, 1, M_HD),
            state(M_HEADS, SUBLANES, LANES),
            state(G_HEADS, G_DV, G_DK),
        ],
        out_shape=[
            jax.ShapeDtypeStruct((n_tok, M_DIM), BF16),
            jax.ShapeDtypeStruct((n_tok, G_VDIM), BF16),
            jax.ShapeDtypeStruct((batch, SUBLANES, 2 * M_DIM), F32),
            jax.ShapeDtypeStruct((batch, M_HEADS, M_HD, M_HD), F32),
            jax.ShapeDtypeStruct((batch, M_HEADS, 1, M_HD), F32),
            jax.ShapeDtypeStruct((batch, M_HEADS, SUBLANES, LANES), F32),
            jax.ShapeDtypeStruct((batch, G_HEADS, G_DV, G_DK), F32),
        ],
        scratch_shapes=[pltpu.VMEM((L + SUBLANES, 2 * M_DIM), F32)],
        compiler_params=_params("parallel", "arbitrary"),
        name="mixers",
    )(z, z, z, z, z, z, z, zs, conv0, c0, n0, m0, st0, cw, cb, gm, w2, ba2, gg, lv)


def _outproj_kernel(ya_p, yb_p, ua_p, ub_p, x_p, ya_s, yb_s, ua_s, ub_s, x_s,
                    wpa_ref, wpb_ref, wout_ref, gffn_ref, wr_ref, br_ref, tril_ref,
                    x1_ref, xn_ref, eo_ref, wo_ref, cnt_ref, *, tm, tiles_p, tiles_s):
    i = pl.program_id(0)

    @pl.when(i == 0)
    def _():
        cnt_ref[...] = jnp.zeros_like(cnt_ref)

    def body(ya_ref, yb_ref, ua_ref, ub_ref, x_ref):
        a = _dot(ya_ref[...], wpa_ref[...])
        b = _dot(yb_ref[...], wpb_ref[...])
        merged = _sigmoid(ua_ref[...].astype(F32)) * a + _sigmoid(ub_ref[...].astype(F32)) * b
        x1 = x_ref[...] + _dot(merged.astype(BF16), wout_ref[...])
        x1_ref[...] = x1
        xn = _rms(x1, gffn_ref[...])
        for j in range(ROW_CHUNKS):
            xn_ref[pl.ds(j, tm, stride=ROW_CHUNKS), :] = xn[:, j * LANES:(j + 1) * LANES]

        logits = _dot(xn.astype(BF16), wr_ref[...]) + br_ref[...]
        lane = lax.broadcasted_iota(jnp.int32, (tm, LANES), 1).astype(F32)
        vals, idxs = [], []
        l = logits
        for _ in range(TOP_K):
            mx = jnp.max(l, axis=-1, keepdims=True)
            ix = jnp.min(jnp.where(l == mx, lane, float(LANES)), axis=-1, keepdims=True)
            vals.append(mx)
            idxs.append(ix)
            l = jnp.where(lane == ix, -jnp.inf, l)
        ev = [jnp.exp(v - vals[0]) for v in vals]
        den = ev[0] + ev[1] + ev[2] + ev[3]

        cnt = jnp.zeros((tm, LANES), F32)
        for ix in idxs:
            cnt = cnt + jnp.where(lane == ix, 1.0, 0.0)
        pre = _dot(tril_ref[...], cnt.astype(BF16)) + cnt_ref[0:1, :]
        eo = jnp.zeros((tm, LANES), F32)
        wo = jnp.zeros((tm, LANES), F32)
        for k in range(TOP_K):
            rank = jnp.sum(jnp.where(lane == idxs[k], pre, 0.0), axis=-1, keepdims=True)
            eo = jnp.where(lane == float(k), idxs[k], eo)
            eo = jnp.where(lane == float(TOP_K + k), rank, eo)
            wo = jnp.where(lane == float(k), ev[k] / den, wo)
        eo_ref[...] = eo.T[0:SUBLANES, :].astype(jnp.int32)
        wo_ref[...] = wo
        cnt_ref[...] = cnt_ref[...] + jnp.sum(cnt, axis=0, keepdims=True)

    @pl.when(i < tiles_p)
    def _():
        body(ya_p, yb_p, ua_p, ub_p, x_p)

    @pl.when(jnp.logical_and(i >= tiles_p, i < tiles_p + tiles_s))
    def _():
        body(ya_s, yb_s, ua_s, ub_s, x_s)

    @pl.when(i == tiles_p + tiles_s)
    def _():
        xn_ref[...] = jnp.zeros_like(xn_ref)


def _outproj(prompt, sample, wpa, wpb, wout, gffn, wr, br, tm):
    n_p, n_s = prompt[3].shape[0], sample[3].shape[0]
    tiles_p, tiles_s = n_p // tm, n_s // tm
    tiles = tiles_p + tiles_s
    n = n_p + n_s
    tril = jnp.asarray(np.tril(np.ones((tm, tm), np.float32), -1), BF16)
    pidx = lambda i: jnp.minimum(i, tiles_p - 1)
    sidx = lambda i: jnp.clip(i - tiles_p, 0, tiles_s - 1)
    oidx = lambda i: jnp.minimum(i, tiles - 1)
    src = lambda idx: [pl.BlockSpec((tm, M_DIM), lambda i: (idx(i), 0)),
                       pl.BlockSpec((tm, G_VDIM), lambda i: (idx(i), 0)),
                       pl.BlockSpec((tm, D_MODEL), lambda i: (idx(i), 7)),
                       pl.BlockSpec((tm, D_MODEL), lambda i: (idx(i), 8)),
                       pl.BlockSpec((tm, D_MODEL), lambda i: (idx(i), 0))]
    const = lambda *shape: pl.BlockSpec(shape, lambda i: (0,) * len(shape))
    ya_p, yb_p, z_p, x_p = prompt
    ya_s, yb_s, z_s, x_s = sample
    return pl.pallas_call(
        functools.partial(_outproj_kernel, tm=tm, tiles_p=tiles_p, tiles_s=tiles_s),
        grid=(tiles + 1,),
        in_specs=src(pidx) + src(sidx) + [
            const(M_DIM, D_MODEL), const(G_VDIM, D_MODEL), const(D_MODEL, D_MODEL), const(1, D_MODEL),
            const(D_MODEL, LANES), const(1, LANES), const(tm, tm),
        ],
        out_specs=[
            pl.BlockSpec((tm, D_MODEL), lambda i: (oidx(i), 0)),
            pl.BlockSpec((tm * ROW_CHUNKS, LANES), lambda i: (i, 0)),
            pl.BlockSpec((SUBLANES, tm), lambda i: (0, oidx(i))),
            pl.BlockSpec((tm, LANES), lambda i: (oidx(i), 0)),
            const(SUBLANES, LANES),
        ],
        out_shape=[
            jax.ShapeDtypeStruct((n, D_MODEL), F32),
            jax.ShapeDtypeStruct(((n + tm) * ROW_CHUNKS, LANES), F32),
            jax.ShapeDtypeStruct((SUBLANES, n), jnp.int32),
            jax.ShapeDtypeStruct((n, LANES), F32),
            jax.ShapeDtypeStruct((SUBLANES, LANES), F32),
        ],
        compiler_params=_params("arbitrary"),
        name="outproj_router",
    )(ya_p, yb_p, z_p, z_p, x_p, ya_s, yb_s, z_s, z_s, x_s, wpa, wpb, wout, gffn, wr, br, tril)


def _row_copy(src, src_row, dst, dst_row, sem):
    return pltpu.make_async_copy(
        src.at[pl.ds(pl.multiple_of(src_row * ROW_CHUNKS, ROW_CHUNKS), ROW_CHUNKS)],
        dst.at[pl.ds(pl.multiple_of(dst_row * ROW_CHUNKS, ROW_CHUNKS), ROW_CHUNKS)], sem)


def _expert_kernel(be_ref, nused_ref, src_ref, xn_ref, wgu_ref, bgu_ref, wd_ref, bd_ref, y_ref,
                   wgu_bf, wd_bf, xbuf, gsem, *, rows):
    i = pl.program_id(0)
    n_used = nused_ref[0]
    idx_rows = rows // LANES
    changed = jnp.logical_or(i == 0, be_ref[i] != be_ref[jnp.maximum(i - 1, 0)])

    def wait_rows(slot):
        seg = rows * ROW_CHUNKS
        pltpu.make_async_copy(xn_ref.at[pl.ds(0, seg)], xbuf.at[pl.ds(pl.multiple_of(slot * seg, seg), seg)],
                              gsem.at[slot]).wait()

    @pl.when(i == 0)
    def _():
        def issue(r, carry):
            _row_copy(xn_ref, src_ref[r // LANES, r % LANES], xbuf, r, gsem.at[0]).start()
            return carry
        lax.fori_loop(0, rows, issue, 0)

    @pl.when(changed)
    def _():
        wgu_bf[...] = wgu_ref[...].astype(BF16)
        wd_bf[...] = wd_ref[...].astype(BF16)

    @pl.when(i < n_used)
    def _():
        slot = i % 2
        wait_rows(slot)
        nxt = jnp.minimum(i + 1, n_used - 1) * idx_rows
        for r in range(rows):
            _row_copy(xn_ref, src_ref[nxt + r // LANES, r % LANES], xbuf, (1 - slot) * rows + r,
                      gsem.at[1 - slot]).start()
        base = pl.multiple_of(slot * rows * ROW_CHUNKS, rows * ROW_CHUNKS)
        x = jnp.concatenate([xbuf[pl.ds(base + j, rows, stride=ROW_CHUNKS), :] for j in range(ROW_CHUNKS)], axis=-1)
        zz = _dot(x.astype(BF16), wgu_bf[...]) + bgu_ref[...]
        g = jnp.minimum(zz[:, :D_FF], SWIGLU_LIMIT)
        u = jnp.clip(zz[:, D_FF:], -SWIGLU_LIMIT, SWIGLU_LIMIT)
        act = g * _sigmoid(SWIGLU_ALPHA * g) * (u + 1.0)
        y = _dot(act.astype(BF16), wd_bf[...]) + bd_ref[...]
        for j in range(ROW_CHUNKS):
            y_ref[pl.ds(j, rows, stride=ROW_CHUNKS), :] = y[:, j * LANES:(j + 1) * LANES]

        @pl.when(i == n_used - 1)
        def _():
            wait_rows(1 - slot)

    @pl.when(i >= n_used)
    def _():
        y_ref[...] = jnp.zeros_like(y_ref)


def _experts(block_e, n_used, src, xn, wgu, bgu, wd, bd, rows):
    nb = block_e.shape[0]
    return pl.pallas_call(
        functools.partial(_expert_kernel, rows=rows),
        grid_spec=pltpu.PrefetchScalarGridSpec(
            num_scalar_prefetch=3,
            grid=(nb,),
            in_specs=[
                pl.BlockSpec(memory_space=pl.ANY),
                pl.BlockSpec((None, D_MODEL, 2 * D_FF), lambda i, be, *_: (be[i], 0, 0)),
                pl.BlockSpec((None, 1, 2 * D_FF), lambda i, be, *_: (be[i], 0, 0)),
                pl.BlockSpec((None, D_FF, D_MODEL), lambda i, be, *_: (be[i], 0, 0)),
                pl.BlockSpec((None, 1, D_MODEL), lambda i, be, *_: (be[i], 0, 0)),
            ],
            out_specs=pl.BlockSpec((rows * ROW_CHUNKS, LANES), lambda i, *_: (i, 0)),
            scratch_shapes=[pltpu.VMEM((D_MODEL, 2 * D_FF), BF16), pltpu.VMEM((D_FF, D_MODEL), BF16),
                            pltpu.VMEM((2 * rows * ROW_CHUNKS, LANES), F32), pltpu.SemaphoreType.DMA((2,))],
        ),
        out_shape=jax.ShapeDtypeStruct((nb * rows * ROW_CHUNKS, LANES), F32),
        compiler_params=_params("arbitrary"),
        name="experts",
    )(block_e, n_used, src, xn, wgu, bgu, wd, bd)


def _combine_kernel(pos_ref, ye_ref, x1_ref, wo_ref, p_ref, wproj_ref, wgate_ref, gple_ref, gfin_ref,
                    y_ref, rows_scr, sem, *, tm, nt, n_all, tok0):
    i = pl.program_id(0)

    def gather(tile, slot):
        def issue(r, carry):
            for k in range(TOP_K):
                _row_copy(ye_ref, pos_ref[k * n_all + tok0 + tile * tm + r], rows_scr, (slot * TOP_K + k) * tm + r,
                          sem.at[slot]).start()
            return carry
        lax.fori_loop(0, tm, issue, 0)

    @pl.when(i == 0)
    def _():
        gather(0, 0)

    @pl.when(i + 1 < nt)
    def _():
        gather(i + 1, (i + 1) % 2)

    slot = i % 2
    seg = tm * ROW_CHUNKS
    for k in range(TOP_K):
        off = pl.multiple_of((slot * TOP_K + k) * seg, seg)
        pltpu.make_async_copy(ye_ref.at[pl.ds(0, seg)], rows_scr.at[pl.ds(off, seg)], sem.at[slot]).wait()

    wo = wo_ref[...]
    moe = jnp.zeros((tm, D_MODEL), F32)
    for k in range(TOP_K):
        off = pl.multiple_of((slot * TOP_K + k) * seg, seg)
        rows = jnp.concatenate(
            [rows_scr[pl.ds(off + j, tm, stride=ROW_CHUNKS), :] for j in range(ROW_CHUNKS)], axis=-1)
        moe = moe + rows * wo[:, k:k + 1]
    x2 = x1_ref[...] + moe
    gate = _sigmoid(_dot(_rms(x2, gple_ref[...]).astype(BF16), wgate_ref[...]))
    x3 = x2 + _dot(p_ref[...].astype(BF16), wproj_ref[...]) * gate
    y_ref[...] = _rms(x3, gfin_ref[...])


def _combine(pos_flat, ye, x1, wo, p, wproj, wgate, gple, gfin, tm, tok0):
    n = p.shape[0]
    n_all = x1.shape[0]
    blk0 = tok0 // tm
    tok = lambda width: pl.BlockSpec((tm, width), lambda i, pos: (blk0 + i, 0))
    const = lambda *shape: pl.BlockSpec(shape, lambda i, pos: (0,) * len(shape))
    return pl.pallas_call(
        functools.partial(_combine_kernel, tm=tm, nt=n // tm, n_all=n_all, tok0=tok0),
        grid_spec=pltpu.PrefetchScalarGridSpec(
            num_scalar_prefetch=1,
            grid=(n // tm,),
            in_specs=[
                pl.BlockSpec(memory_space=pl.ANY),
                tok(D_MODEL), tok(LANES), pl.BlockSpec((tm, PLE_DIM), lambda i, pos: (i, 0)),
                const(PLE_DIM, D_MODEL), const(D_MODEL, D_MODEL), const(1, D_MODEL), const(1, D_MODEL),
            ],
            out_specs=pl.BlockSpec((tm, D_MODEL), lambda i, pos: (i, 0)),
            scratch_shapes=[pltpu.VMEM((2 * TOP_K * tm * ROW_CHUNKS, LANES), F32), pltpu.SemaphoreType.DMA((2,))],
        ),
        out_shape=jax.ShapeDtypeStruct((n, D_MODEL), F32),
        compiler_params=_params("arbitrary"),
        name="combine_ple",
    )(pos_flat, ye, x1, wo, p, wproj, wgate, gple, gfin)


def _tile(n, pref):
    t = pref
    while n % t:
        t //= 2
    return t


def _seg(a, i):
    return a[..., _OFF[i]:_OFF[i + 1]]


def kernel(x_prompt, x_sample, state_conv, state_mlstm_C, state_mlstm_n, state_mlstm_m, state_gla_S, p_prompt, p_sample, g_mix, w_in, b_in, conv_w, conv_b, g_mnorm, w_a2, b_a2, g_gnorm, w_pa, w_pb, w_out, g_ffn, w_router, b_router, w_gate_up, b_gate_up, w_down, b_down, g_ple, w_ple_proj, w_ple_gate, g_final):
    assert g_mix.shape[0] == 1, "single-layer kernel"
    bp, tp, _ = x_prompt.shape
    bs, ts, _ = x_sample.shape

    big = (0, 1, 2, 5, 6, 7, 8, 10, 11, 12)
    w_big = jnp.concatenate([_seg(w_in[0], i) for i in big], axis=-1).astype(BF16)
    b_big = jnp.concatenate([_seg(b_in[0], i) for i in big], axis=-1)[None, :]
    small = (3, 4, 9)
    n_small = sum(SPLITS[i] for i in small)
    w_small = jnp.pad(jnp.concatenate([_seg(w_in[0], i) for i in small], axis=-1),
                      ((0, 0), (0, LANES - n_small))).astype(BF16)
    b_small = jnp.pad(jnp.concatenate([_seg(b_in[0], i) for i in small], axis=-1), (0, LANES - n_small))[None, :]
    w2 = jnp.pad(w_a2[0], ((2 * M_HEADS, LANES - 2 * M_HEADS - G_RANK), (0, 0))).astype(BF16)
    wr = jnp.pad(w_router[0], ((0, 0), (0, LANES - N_EXPERTS))).astype(BF16)
    br = jnp.pad(b_router[0], (0, LANES - N_EXPERTS), constant_values=NEG_BIG)[None, :]
    wpa, wpb, wout = w_pa[0].astype(BF16), w_pb[0].astype(BF16), w_out[0].astype(BF16)
    wproj, wgate = w_ple_proj[0].astype(BF16), w_ple_gate[0].astype(BF16)
    row = lambda a: a.reshape(1, -1)

    def mix_path(x3d, conv0, c0, n0, m0, s0):
        batch, T, _ = x3d.shape
        x = x3d.reshape(batch * T, D_MODEL)
        z, zs = _inproj(x, row(g_mix[0]), w_big, b_big, w_small, b_small, _tile(batch * T, 1024))
        L = min(MIX_CHUNK, T)
        conv0p = jnp.pad(conv0, ((0, 0), (SUBLANES - (CONV_W - 1), 0), (0, 0)))
        m0b = jnp.broadcast_to(m0[:, :, None, None], (batch, M_HEADS, SUBLANES, LANES))
        ya, yb, tail, c1, n1, m1, st1 = _mixers(
            z, zs, conv0p, c0, n0[:, :, None, :], m0b, jnp.swapaxes(s0, -1, -2),
            conv_w[0], row(conv_b[0]), row(g_mnorm[0]), w2, row(b_a2[0]), row(g_gnorm[0]), batch, T, L)
        new_state = (tail[:, SUBLANES - (CONV_W - 1):, :], c1, n1[:, :, 0, :], m1[:, :, 0, 0],
                     jnp.swapaxes(st1, -1, -2))
        return x, z, ya, yb, new_state

    zero_state = (jnp.zeros((bp, CONV_W - 1, 2 * M_DIM), F32), jnp.zeros((bp, M_HEADS, M_HD, M_HD), F32),
                  jnp.zeros((bp, M_HEADS, M_HD), F32), jnp.zeros((bp, M_HEADS), F32),
                  jnp.zeros((bp, G_HEADS, G_DK, G_DV), F32))
    xp, zp, yap, ybp, st_p = mix_path(x_prompt, *zero_state)
    xs_, zs_, yas, ybs, st_s = mix_path(x_sample, state_conv[0], state_mlstm_C[0], state_mlstm_n[0],
                                        state_mlstm_m[0], state_gla_S[0])

    n_p, n_s = xp.shape[0], xs_.shape[0]
    n_all = n_p + n_s
    tm = _tile(n_s, 256)
    assert n_p % tm == 0
    x1, xn, eo, wo, cnt = _outproj((yap, ybp, zp, xp), (yas, ybs, zs_, xs_), wpa, wpb, wout, row(g_ffn[0]), wr, br, tm)

    counts = cnt[0, :N_EXPERTS].astype(jnp.int32)
    padded = (counts + MOE_ROWS - 1) // MOE_ROWS * MOE_ROWS
    pend = jnp.cumsum(padded)
    pstart = pend - padded
    nk = n_all * TOP_K
    n_blocks = (nk + N_EXPERTS * (MOE_ROWS - 1) + MOE_ROWS - 1) // MOE_ROWS
    starts = jnp.arange(n_blocks, dtype=jnp.int32) * MOE_ROWS
    block_e = jnp.minimum(jnp.sum((pend[None, :] <= starts[:, None]).astype(jnp.int32), axis=1), N_EXPERTS - 1)
    n_used = (pend[-1:] // MOE_ROWS).astype(jnp.int32)
    e, rank = eo[:TOP_K], eo[TOP_K:2 * TOP_K]
    onehot = e[..., None] == jnp.arange(N_EXPERTS, dtype=jnp.int32)
    pos = (jnp.sum(jnp.where(onehot, pstart, 0), axis=-1) + rank).reshape(-1)
    tok_ids = jnp.tile(jnp.arange(n_all, dtype=jnp.int32), TOP_K)
    src = jnp.full((n_blocks * MOE_ROWS,), n_all, jnp.int32).at[pos].set(tok_ids)
    src = src.reshape(-1, LANES)

    ye = _experts(block_e, n_used, src, xn, w_gate_up[0], b_gate_up[0][:, None, :], w_down[0],
                  b_down[0][:, None, :], MOE_ROWS)
    fin = lambda p, tok0: _combine(pos, ye, x1, wo, p.reshape(-1, PLE_DIM), wproj, wgate,
                                   row(g_ple[0]), row(g_final), tm, tok0)
    y_p = fin(p_prompt[0], 0).reshape(bp, tp, D_MODEL)
    y_s = fin(p_sample[0], n_p).reshape(bs, ts, D_MODEL)

    lead = lambda a: a[None]
    return (y_p, y_s) + tuple(lead(a) for a in st_p) + tuple(lead(a) for a in st_s)
```

```python
import functools

import numpy as np
import jax
import jax.numpy as jnp
from jax import lax
from jax.experimental import pallas as pl
from jax.experimental.pallas import tpu as pltpu

F32 = jnp.float32
BF16 = jnp.bfloat16

D_MODEL = 1024
EPS = 1e-6
M_HEADS = 4
M_DIM = D_MODEL
M_HD = M_DIM // M_HEADS
CONV_W = 4
G_HEADS = 4
G_KDIM = D_MODEL // 2
G_VDIM = D_MODEL
G_DK = G_KDIM // G_HEADS
G_DV = G_VDIM // G_HEADS
G_RANK = 16
G_TAU = 16.0
N_EXPERTS = 32
TOP_K = 4
D_FF = D_MODEL
SWIGLU_LIMIT = 7.0
SWIGLU_ALPHA = 1.702
PLE_DIM = 256
SPLITS = (M_DIM, M_DIM, M_DIM, M_HEADS, M_HEADS, M_DIM, G_KDIM, G_KDIM, G_VDIM, G_RANK, G_VDIM, D_MODEL, D_MODEL)
_OFF = np.concatenate([[0], np.cumsum(SPLITS)]).tolist()

LANES = 128
SUBLANES = 8
ROW_CHUNKS = D_MODEL // LANES
VMEM_LIMIT = 56 * 1024 * 1024
MIX_CHUNK = 256
MOE_ROWS = 256
NEG_BIG = -1e30


def _sigmoid(x):
    return 1.0 / (1.0 + jnp.exp(-x))


def _log_sigmoid(x):
    return jnp.minimum(x, 0.0) - jnp.log1p(jnp.exp(-jnp.abs(x)))


def _rms(x, g):
    return x * lax.rsqrt(jnp.mean(x * x, axis=-1, keepdims=True) + EPS) * g


def _dot(a, b):
    return jnp.dot(a, b, preferred_element_type=F32)


def _dot_nt(a, b):
    return lax.dot_general(a, b, (((1,), (1,)), ((), ())), preferred_element_type=F32)


def _dot_tn(a, b):
    return lax.dot_general(a, b, (((0,), (0,)), ((), ())), preferred_element_type=F32)


def _params(*sem):
    return pltpu.CompilerParams(dimension_semantics=sem, vmem_limit_bytes=VMEM_LIMIT)


def _inproj_kernel(x_ref, g_ref, w_ref, b_ref, ws_ref, bs_ref, z_ref, zs_ref, h_scr):
    @pl.when(pl.program_id(1) == 0)
    def _():
        hb = _rms(x_ref[...], g_ref[...]).astype(BF16)
        h_scr[...] = hb
        zs_ref[...] = _dot(hb, ws_ref[...]) + bs_ref[...]

    z_ref[...] = (_dot(h_scr[...], w_ref[...]) + b_ref[...]).astype(BF16)


def _inproj(x, g, w_big, b_big, w_small, b_small, tm, tn=1024):
    n = x.shape[0]
    cols = w_big.shape[1]
    return pl.pallas_call(
        _inproj_kernel,
        grid=(n // tm, cols // tn),
        in_specs=[
            pl.BlockSpec((tm, D_MODEL), lambda i, j: (i, 0)),
            pl.BlockSpec((1, D_MODEL), lambda i, j: (0, 0)),
            pl.BlockSpec((D_MODEL, tn), lambda i, j: (0, j)),
            pl.BlockSpec((1, tn), lambda i, j: (0, j)),
            pl.BlockSpec((D_MODEL, LANES), lambda i, j: (0, 0)),
            pl.BlockSpec((1, LANES), lambda i, j: (0, 0)),
        ],
        out_specs=[
            pl.BlockSpec((tm, tn), lambda i, j: (i, j)),
            pl.BlockSpec((tm, LANES), lambda i, j: (i, 0)),
        ],
        out_shape=[jax.ShapeDtypeStruct((n, cols), BF16), jax.ShapeDtypeStruct((n, LANES), F32)],
        scratch_shapes=[pltpu.VMEM((tm, D_MODEL), BF16)],
        compiler_params=_params("parallel", "arbitrary"),
        name="inproj",
    )(x, g, w_big, b_big, w_small, b_small)


def _level_matrix(L):
    t = np.arange(L)[:, None]
    s = np.arange(L)[None, :]
    x = np.maximum(t ^ s, 1)
    lv = np.floor(np.log2(x)).astype(np.int32)
    lv = np.where(t == s, -1, lv)
    return np.where(s > t, -2, lv).astype(np.int32)


def _bcast_row(x, period, r):
    L, W = x.shape
    x3 = x.reshape(L // period, period, W)
    return jnp.broadcast_to(x3[:, r:r + 1, :], x3.shape).reshape(L, W)


def _segmented_scans(la, L):
    W = la.shape[1]
    row = lax.broadcasted_iota(jnp.int32, (L, W), 0)
    fwd = [la]
    rev = [la]
    b = 1
    while b < L:
        per = 2 * b
        res = row & (per - 1)
        f, r = fwd[-1], rev[-1]
        if per < SUBLANES:
            addf = jnp.zeros_like(la)
            addr = jnp.zeros_like(la)
            for j in range(b, per):
                addf = jnp.where(res == j, pltpu.roll(f, j - (b - 1), 0), addf)
            for j in range(0, b):
                addr = jnp.where(res == j, pltpu.roll(r, L - (b - j), 0), addr)
        else:
            addf = jnp.where(res >= b, _bcast_row(f, per, b - 1), 0.0)
            addr = jnp.where(res < b, _bcast_row(r, per, b), 0.0)
        fwd.append(f + addf)
        rev.append(r + addr)
        b = per
    return fwd, rev


def _mixer_kernel(zqk_ref, zv_ref, zo_ref, zgq_ref, zgk_ref, zgv_ref, zgg_ref, zs_ref,
                  conv0_ref, c0_ref, n0_ref, m0_ref, s0_ref,
                  cw_ref, cb_ref, gm_ref, w2_ref, ba2_ref, gg_ref, lv_ref,
                  ya_ref, yb_ref, tail_ref, c_ref, n_ref, m_ref, st_ref, ext_ref, *, L):
    @pl.when(pl.program_id(1) == 0)
    def _():
        tail_ref[...] = conv0_ref[...]
        c_ref[...] = c0_ref[...]
        n_ref[...] = n0_ref[...]
        m_ref[...] = m0_ref[...]
        st_ref[...] = s0_ref[...]

    lv = lv_ref[...]
    causal = lv >= -1

    u = zqk_ref[...].astype(F32)
    ext_ref[0:SUBLANES, :] = tail_ref[0]
    ext_ref[SUBLANES:SUBLANES + L, :] = u
    acc = cb_ref[...] + cw_ref[CONV_W - 1:CONV_W, :] * u
    for w in range(CONV_W - 1):
        acc = acc + cw_ref[w:w + 1, :] * ext_ref[pl.ds(SUBLANES - (CONV_W - 1) + w, L), :]
    tail_ref[0] = ext_ref[pl.ds(L, SUBLANES), :]
    qk = acc * _sigmoid(acc)

    zs = zs_ref[...]
    row = lax.broadcasted_iota(jnp.int32, (L, LANES), 0)
    col = lax.broadcasted_iota(jnp.int32, (L, LANES), 1)
    bcs = _log_sigmoid(zs)
    sh = 1
    while sh < L:
        bcs = bcs + jnp.where(row >= sh, pltpu.roll(bcs, sh, 0), 0.0)
        sh *= 2
    gates = jnp.where(col < M_HEADS, zs, bcs)
    if L < LANES:
        gates = jnp.concatenate([gates, jnp.zeros((LANES - L, LANES), F32)], axis=0)
    gt = gates.T[:, :L]

    for h in range(M_HEADS):
        hs = slice(h * M_HD, (h + 1) * M_HD)
        ig_col = zs[:, h:h + 1]
        b_col = bcs[:, M_HEADS + h:M_HEADS + h + 1]
        ig_row = gt[h:h + 1, :]
        b_row = gt[M_HEADS + h:M_HEADS + h + 1, :]
        m_prev = m_ref[0, h][0:1, 0:1]
        logd = jnp.where(causal, b_col - b_row + ig_row, -jnp.inf)
        inter = b_col + m_prev
        m_t = jnp.maximum(inter, jnp.max(logd, axis=-1, keepdims=True))
        q_f = qk[:, hs]
        k_f = qk[:, M_DIM + h * M_HD:M_DIM + (h + 1) * M_HD] * (M_HD ** -0.5)
        q_b = q_f.astype(BF16)
        v_b = zv_ref[:, hs]
        s = _dot_nt(q_b, k_f.astype(BF16)) * jnp.exp(logd - m_t)
        w_int = jnp.exp(inter - m_t)
        c_old = c_ref[0, h]
        n_old = n_ref[0, h]
        num = _dot(s.astype(BF16), v_b) + w_int * _dot(q_b, c_old.astype(BF16))
        den = jnp.sum(s, axis=-1, keepdims=True) + w_int * jnp.sum(q_f * n_old, axis=-1, keepdims=True)
        hh = num / jnp.maximum(jnp.abs(den), jnp.exp(-m_t))
        m_new = m_t[L - 1:L, :]
        b_last = b_col[L - 1:L, :]
        w_k = jnp.exp(b_last - b_col + ig_col - m_new)
        dec = jnp.exp(b_last + m_prev - m_new)
        k_w = k_f * w_k
        c_ref[0, h] = dec * c_old + _dot_tn(k_w.astype(BF16), v_b)
        n_ref[0, h] = dec * n_old + jnp.sum(k_w, axis=0, keepdims=True)
        m_ref[0, h] = jnp.broadcast_to(m_new, (SUBLANES, LANES))
        hn = _rms(hh, gm_ref[:, hs])
        ya_ref[:, hs] = (hn * _sigmoid(zo_ref[:, hs].astype(F32))).astype(BF16)

    la = _log_sigmoid(_dot(zs.astype(BF16), w2_ref[...]) + ba2_ref[...]) * (1.0 / G_TAU)
    fwd, rev = _segmented_scans(la, L)
    nlev = len(fwd) - 1
    qg = zgq_ref[...].astype(F32) * (G_DK ** -0.5)
    kg = zgk_ref[...].astype(F32)
    qs = [(qg * jnp.exp(fwd[j])).astype(BF16) for j in range(nlev)]
    ks = [(kg * jnp.exp(rev[j] - la)).astype(BF16) for j in range(nlev)]
    q_in = (qg * jnp.exp(fwd[nlev])).astype(BF16)
    k_out = (kg * jnp.exp(rev[nlev] - la)).astype(BF16)
    q_d = qg.astype(BF16)
    k_d = kg.astype(BF16)
    dec_all = jnp.exp(fwd[nlev][L - 1:L, :])
    for h in range(G_HEADS):
        ks_ = slice(h * G_DK, (h + 1) * G_DK)
        vs_ = slice(h * G_DV, (h + 1) * G_DV)
        att = jnp.where(lv == -1, _dot_nt(q_d[:, ks_], k_d[:, ks_]), 0.0)
        for j in range(nlev):
            att = jnp.where(lv == j, _dot_nt(qs[j][:, ks_], ks[j][:, ks_]), att)
        v_b = zgv_ref[:, vs_]
        st_old = st_ref[0, h]
        o = _dot(att.astype(BF16), v_b) + _dot_nt(q_in[:, ks_], st_old.astype(BF16))
        st_ref[0, h] = st_old * dec_all[:, ks_] + _dot_tn(v_b, k_out[:, ks_])
        on = _rms(o, gg_ref[:, vs_])
        g = zgg_ref[:, vs_].astype(F32)
        yb_ref[:, vs_] = (on * (g * _sigmoid(g))).astype(BF16)


def _mixers(z, zs, conv0, c0, n0, m0, st0, cw, cb, gm, w2, ba2, gg, batch, T, L):
    nc = T // L
    lv = jnp.asarray(_level_matrix(L))
    row_blk = lambda width, cidx: pl.BlockSpec((L, width), lambda b, c: (b * nc + c, cidx))
    state = lambda *shape: pl.BlockSpec((1,) + shape, lambda b, c: (b,) + (0,) * len(shape))
    const = lambda *shape: pl.BlockSpec(shape, lambda b, c: (0,) * len(shape))
    n_tok = batch * T
    return pl.pallas_call(
        functools.partial(_mixer_kernel, L=L),
        grid=(batch, nc),
        in_specs=[
            row_blk(2 * M_DIM, 0),
            row_blk(M_DIM, 2),
            row_blk(M_DIM, 3),
            row_blk(G_KDIM, 8),
            row_blk(G_KDIM, 9),
            row_blk(G_VDIM, 5),
            row_blk(G_VDIM, 6),
            pl.BlockSpec((L, LANES), lambda b, c: (b * nc + c, 0)),
            state(SUBLANES, 2 * M_DIM),
            state(M_HEADS, M_HD, M_HD),
            state(M_HEADS, 1, M_HD),
            state(M_HEADS, SUBLANES, LANES),
            state(G_HEADS, G_DV, G_DK),
            const(CONV_W, 2 * M_DIM),
            const(1, 2 * M_DIM),
            const(1, M_DIM),
            const(LANES, G_KDIM),
            const(1, G_KDIM),
            const(1, G_VDIM),
            const(L, L),
        ],
        out_specs=[
            pl.BlockSpec((L, M_DIM), lambda b, c: (b * nc + c, 0)),
            pl.BlockSpec((L, G_VDIM), lambda b, c: (b * nc + c, 0)),
            state(SUBLANES, 2 * M_DIM),
            state(M_HEADS, M_HD, M_HD),
            state(M_HEADS, 1, M_HD),
            state(M_HEADS, SUBLANES, LANES),
            state(G_HEADS, G_DV, G_DK),
        ],
        out_shape=[
            jax.ShapeDtypeStruct((n_tok, M_DIM), BF16),
            jax.ShapeDtypeStruct((n_tok, G_VDIM), BF16),
            jax.ShapeDtypeStruct((batch, SUBLANES, 2 * M_DIM), F32),
            jax.ShapeDtypeStruct((batch, M_HEADS, M_HD, M_HD), F32),
            jax.ShapeDtypeStruct((batch, M_HEADS, 1, M_HD), F32),
            jax.ShapeDtypeStruct((batch, M_HEADS, SUBLANES, LANES), F32),
            jax.ShapeDtypeStruct((batch, G_HEADS, G_DV, G_DK), F32),
        ],
        scratch_shapes=[pltpu.VMEM((L + SUBLANES, 2 * M_DIM), F32)],
        compiler_params=_params("parallel", "arbitrary"),
        name="mixers",
    )(z, z, z, z, z, z, z, zs, conv0, c0, n0, m0, st0, cw, cb, gm, w2, ba2, gg, lv)


def _outproj_kernel(ya_p, yb_p, ua_p, ub_p, x_p, ya_s, yb_s, ua_s, ub_s, x_s,
                    wpa_ref, wpb_ref, wout_ref, gffn_ref, wr_ref, br_ref, tril_ref,
                    x1_ref, xn_ref, eo_ref, wo_ref, cnt_ref, *, tm, tiles_p, tiles_s):
    i = pl.program_id(0)

    @pl.when(i == 0)
    def _():
        cnt_ref[...] = jnp.zeros_like(cnt_ref)

    def body(ya_ref, yb_ref, ua_ref, ub_ref, x_ref):
        a = _dot(ya_ref[...], wpa_ref[...])
        b = _dot(yb_ref[...], wpb_ref[...])
        merged = _sigmoid(ua_ref[...].astype(F32)) * a + _sigmoid(ub_ref[...].astype(F32)) * b
        x1 = x_ref[...] + _dot(merged.astype(BF16), wout_ref[...])
        x1_ref[...] = x1
        xn = _rms(x1, gffn_ref[...])
        for j in range(ROW_CHUNKS):
            xn_ref[pl.ds(j, tm, stride=ROW_CHUNKS), :] = xn[:, j * LANES:(j + 1) * LANES]

        logits = _dot(xn.astype(BF16), wr_ref[...]) + br_ref[...]
        lane = lax.broadcasted_iota(jnp.int32, (tm, LANES), 1).astype(F32)
        vals, idxs = [], []
        l = logits
        for _ in range(TOP_K):
            mx = jnp.max(l, axis=-1, keepdims=True)
            ix = jnp.min(jnp.where(l == mx, lane, float(LANES)), axis=-1, keepdims=True)
            vals.append(mx)
            idxs.append(ix)
            l = jnp.where(lane == ix, -jnp.inf, l)
        ev = [jnp.exp(v - vals[0]) for v in vals]
        den = ev[0] + ev[1] + ev[2] + ev[3]

        cnt = jnp.zeros((tm, LANES), F32)
        for ix in idxs:
            cnt = cnt + jnp.where(lane == ix, 1.0, 0.0)
        pre = _dot(tril_ref[...], cnt.astype(BF16)) + cnt_ref[0:1, :]
        eo = jnp.zeros((tm, LANES), F32)
        wo = jnp.zeros((tm, LANES), F32)
        for k in range(TOP_K):
            rank = jnp.sum(jnp.where(lane == idxs[k], pre, 0.0), axis=-1, keepdims=True)
            eo = jnp.where(lane == float(k), idxs[k], eo)
            eo = jnp.where(lane == float(TOP_K + k), rank, eo)
            wo = jnp.where(lane == float(k), ev[k] / den, wo)
        eo_ref[...] = eo.T[0:SUBLANES, :].astype(jnp.int32)
        wo_ref[...] = wo
        cnt_ref[...] = cnt_ref[...] + jnp.sum(cnt, axis=0, keepdims=True)

    @pl.when(i < tiles_p)
    def _():
        body(ya_p, yb_p, ua_p, ub_p, x_p)

    @pl.when(jnp.logical_and(i >= tiles_p, i < tiles_p + tiles_s))
    def _():
        body(ya_s, yb_s, ua_s, ub_s, x_s)

    @pl.when(i == tiles_p + tiles_s)
    def _():
        xn_ref[...] = jnp.zeros_like(xn_ref)


def _outproj(prompt, sample, wpa, wpb, wout, gffn, wr, br, tm):
    n_p, n_s = prompt[3].shape[0], sample[3].shape[0]
    tiles_p, tiles_s = n_p // tm, n_s // tm
    tiles = tiles_p + tiles_s
    n = n_p + n_s
    tril = jnp.asarray(np.tril(np.ones((tm, tm), np.float32), -1), BF16)
    pidx = lambda i: jnp.minimum(i, tiles_p - 1)
    sidx = lambda i: jnp.clip(i - tiles_p, 0, tiles_s - 1)
    oidx = lambda i: jnp.minimum(i, tiles - 1)
    src = lambda idx: [pl.BlockSpec((tm, M_DIM), lambda i: (idx(i), 0)),
                       pl.BlockSpec((tm, G_VDIM), lambda i: (idx(i), 0)),
                       pl.BlockSpec((tm, D_MODEL), lambda i: (idx(i), 7)),
                       pl.BlockSpec((tm, D_MODEL), lambda i: (idx(i), 8)),
                       pl.BlockSpec((tm, D_MODEL), lambda i: (idx(i), 0))]
    const = lambda *shape: pl.BlockSpec(shape, lambda i: (0,) * len(shape))
    ya_p, yb_p, z_p, x_p = prompt
    ya_s, yb_s, z_s, x_s = sample
    return pl.pallas_call(
        functools.partial(_outproj_kernel, tm=tm, tiles_p=tiles_p, tiles_s=tiles_s),
        grid=(tiles + 1,),
        in_specs=src(pidx) + src(sidx) + [
            const(M_DIM, D_MODEL), const(G_VDIM, D_MODEL), const(D_MODEL, D_MODEL), const(1, D_MODEL),
            const(D_MODEL, LANES), const(1, LANES), const(tm, tm),
        ],
        out_specs=[
            pl.BlockSpec((tm, D_MODEL), lambda i: (oidx(i), 0)),
            pl.BlockSpec((tm * ROW_CHUNKS, LANES), lambda i: (i, 0)),
            pl.BlockSpec((SUBLANES, tm), lambda i: (0, oidx(i))),
            pl.BlockSpec((tm, LANES), lambda i: (oidx(i), 0)),
            const(SUBLANES, LANES),
        ],
        out_shape=[
            jax.ShapeDtypeStruct((n, D_MODEL), F32),
            jax.ShapeDtypeStruct(((n + tm) * ROW_CHUNKS, LANES), F32),
            jax.ShapeDtypeStruct((SUBLANES, n), jnp.int32),
            jax.ShapeDtypeStruct((n, LANES), F32),
            jax.ShapeDtypeStruct((SUBLANES, LANES), F32),
        ],
        compiler_params=_params("arbitrary"),
        name="outproj_router",
    )(ya_p, yb_p, z_p, z_p, x_p, ya_s, yb_s, z_s, z_s, x_s, wpa, wpb, wout, gffn, wr, br, tril)


def _row_copy(src, src_row, dst, dst_row, sem):
    return pltpu.make_async_copy(
        src.at[pl.ds(pl.multiple_of(src_row * ROW_CHUNKS, ROW_CHUNKS), ROW_CHUNKS)],
        dst.at[pl.ds(pl.multiple_of(dst_row * ROW_CHUNKS, ROW_CHUNKS), ROW_CHUNKS)], sem)


def _dispatch_kernel(pos_ref, fill_ref, xn_ref, xs_ref, zeros_ref, zsem, sem, *, tm, n, n_blocks):
    i = pl.program_id(0)
    base = i * tm

    def issue(r, carry):
        for k in range(TOP_K):
            _row_copy(xn_ref, r, xs_ref, pos_ref[k * n + base + r], sem).start(priority=k % 2)
        return carry

    lax.fori_loop(0, tm, issue, 0)
    for _ in range(TOP_K):
        pltpu.make_async_copy(xn_ref, xs_ref.at[pl.ds(0, tm * ROW_CHUNKS)], sem).wait()

    @pl.when(i == 0)
    def _():
        zeros_ref[...] = jnp.zeros_like(zeros_ref)
        bits = [1 << b for b in range(MOE_ROWS.bit_length() - 1)]

        def zero_copy(off, nrows):
            return pltpu.make_async_copy(
                zeros_ref.at[pl.ds(0, nrows * ROW_CHUNKS)],
                xs_ref.at[pl.ds(pl.multiple_of(off * ROW_CHUNKS, ROW_CHUNKS), nrows * ROW_CHUNKS)], zsem)

        def expert(e, carry):
            npad = fill_ref[N_EXPERTS + e]
            for wait in (False, True):
                off = fill_ref[e]
                for bit in bits:
                    @pl.when((npad & bit) != 0)
                    def _():
                        zero_copy(off, bit).wait() if wait else zero_copy(off, bit).start()
                    off = off + (npad & bit)
            return carry

        lax.fori_loop(0, N_EXPERTS, expert, 0)

        def block(b, carry):
            zero_copy(b * MOE_ROWS, MOE_ROWS).start()
            zero_copy(b * MOE_ROWS, MOE_ROWS).wait()
            return carry

        lax.fori_loop(fill_ref[2 * N_EXPERTS], n_blocks, block, 0)


def _dispatch(pos, fill, xn, n, n_blocks, tm):
    return pl.pallas_call(
        functools.partial(_dispatch_kernel, tm=tm, n=n, n_blocks=n_blocks),
        grid_spec=pltpu.PrefetchScalarGridSpec(
            num_scalar_prefetch=2,
            grid=(n // tm,),
            in_specs=[pl.BlockSpec((tm * ROW_CHUNKS, LANES), lambda i, *_: (i, 0))],
            out_specs=pl.BlockSpec(memory_space=pl.ANY),
            scratch_shapes=[pltpu.VMEM((MOE_ROWS * ROW_CHUNKS, LANES), F32), pltpu.SemaphoreType.DMA(()),
                            pltpu.SemaphoreType.DMA(())],
        ),
        out_shape=jax.ShapeDtypeStruct((n_blocks * MOE_ROWS * ROW_CHUNKS, LANES), F32),
        compiler_params=_params("arbitrary"),
        name="dispatch",
    )(pos, fill, xn)


def _expert_kernel(be_ref, nused_ref, xs_ref, wgu_ref, bgu_ref, wd_ref, bd_ref, y_ref, wgu_bf, wd_bf, *, rows):
    i = pl.program_id(0)
    changed = jnp.logical_or(i == 0, be_ref[i] != be_ref[jnp.maximum(i - 1, 0)])

    @pl.when(changed)
    def _():
        wgu_bf[...] = wgu_ref[...].astype(BF16)
        wd_bf[...] = wd_ref[...].astype(BF16)

    @pl.when(i < nused_ref[0])
    def _():
        x = jnp.concatenate([xs_ref[pl.ds(j, rows, stride=ROW_CHUNKS), :] for j in range(ROW_CHUNKS)], axis=-1)
        zz = _dot(x.astype(BF16), wgu_bf[...]) + bgu_ref[...]
        g = jnp.minimum(zz[:, :D_FF], SWIGLU_LIMIT)
        u = jnp.clip(zz[:, D_FF:], -SWIGLU_LIMIT, SWIGLU_LIMIT)
        act = g * _sigmoid(SWIGLU_ALPHA * g) * (u + 1.0)
        y = _dot(act.astype(BF16), wd_bf[...]) + bd_ref[...]
        for j in range(ROW_CHUNKS):
            y_ref[pl.ds(j, rows, stride=ROW_CHUNKS), :] = y[:, j * LANES:(j + 1) * LANES]

    @pl.when(i >= nused_ref[0])
    def _():
        y_ref[...] = jnp.zeros_like(y_ref)


def _experts(block_e, n_used, xs, wgu, bgu, wd, bd, rows):
    nb = block_e.shape[0]
    return pl.pallas_call(
        functools.partial(_expert_kernel, rows=rows),
        grid_spec=pltpu.PrefetchScalarGridSpec(
            num_scalar_prefetch=2,
            grid=(nb,),
            in_specs=[
                pl.BlockSpec((rows * ROW_CHUNKS, LANES), lambda i, be, nu: (i, 0)),
                pl.BlockSpec((None, D_MODEL, 2 * D_FF), lambda i, be, nu: (be[i], 0, 0)),
                pl.BlockSpec((None, 1, 2 * D_FF), lambda i, be, nu: (be[i], 0, 0)),
                pl.BlockSpec((None, D_FF, D_MODEL), lambda i, be, nu: (be[i], 0, 0)),
                pl.BlockSpec((None, 1, D_MODEL), lambda i, be, nu: (be[i], 0, 0)),
            ],
            out_specs=pl.BlockSpec((rows * ROW_CHUNKS, LANES), lambda i, be, nu: (i, 0)),
            scratch_shapes=[pltpu.VMEM((D_MODEL, 2 * D_FF), BF16), pltpu.VMEM((D_FF, D_MODEL), BF16)],
        ),
        out_shape=jax.ShapeDtypeStruct(xs.shape, F32),
        compiler_params=_params("arbitrary"),
        name="experts",
    )(block_e, n_used, xs, wgu, bgu, wd, bd)


def _combine_kernel(pos_ref, ye_ref, x1_ref, wo_ref, p_ref, wproj_ref, wgate_ref, gple_ref, gfin_ref,
                    y_ref, rows_scr, sem, *, tm, nt, n_all, tok0):
    i = pl.program_id(0)

    def gather(tile, slot):
        def issue(r, carry):
            for k in range(TOP_K):
                _row_copy(ye_ref, pos_ref[k * n_all + tok0 + tile * tm + r], rows_scr, (slot * TOP_K + k) * tm + r,
                          sem.at[slot]).start(priority=k % 2)
            return carry
        lax.fori_loop(0, tm, issue, 0)

    @pl.when(i == 0)
    def _():
        gather(0, 0)

    @pl.when(i + 1 < nt)
    def _():
        gather(i + 1, (i + 1) % 2)

    slot = i % 2
    seg = tm * ROW_CHUNKS
    for k in range(TOP_K):
        off = pl.multiple_of((slot * TOP_K + k) * seg, seg)
        pltpu.make_async_copy(ye_ref.at[pl.ds(0, seg)], rows_scr.at[pl.ds(off, seg)], sem.at[slot]).wait()

    wo = wo_ref[...]
    moe = jnp.zeros((tm, D_MODEL), F32)
    for k in range(TOP_K):
        off = pl.multiple_of((slot * TOP_K + k) * seg, seg)
        rows = jnp.concatenate(
            [rows_scr[pl.ds(off + j, tm, stride=ROW_CHUNKS), :] for j in range(ROW_CHUNKS)], axis=-1)
        moe = moe + rows * wo[:, k:k + 1]
    x2 = x1_ref[...] + moe
    gate = _sigmoid(_dot(_rms(x2, gple_ref[...]).astype(BF16), wgate_ref[...]))
    x3 = x2 + _dot(p_ref[...].astype(BF16), wproj_ref[...]) * gate
    y_ref[...] = _rms(x3, gfin_ref[...])


def _combine(pos_flat, ye, x1, wo, p, wproj, wgate, gple, gfin, tm, tok0):
    n = p.shape[0]
    n_all = x1.shape[0]
    blk0 = tok0 // tm
    tok = lambda width: pl.BlockSpec((tm, width), lambda i, pos: (blk0 + i, 0))
    const = lambda *shape: pl.BlockSpec(shape, lambda i, pos: (0,) * len(shape))
    return pl.pallas_call(
        functools.partial(_combine_kernel, tm=tm, nt=n // tm, n_all=n_all, tok0=tok0),
        grid_spec=pltpu.PrefetchScalarGridSpec(
            num_scalar_prefetch=1,
            grid=(n // tm,),
            in_specs=[
                pl.BlockSpec(memory_space=pl.ANY),
                tok(D_MODEL), tok(LANES), pl.BlockSpec((tm, PLE_DIM), lambda i, pos: (i, 0)),
                const(PLE_DIM, D_MODEL), const(D_MODEL, D_MODEL), const(1, D_MODEL), const(1, D_MODEL),
            ],
            out_specs=pl.BlockSpec((tm, D_MODEL), lambda i, pos: (i, 0)),
            scratch_shapes=[pltpu.VMEM((2 * TOP_K * tm * ROW_CHUNKS, LANES), F32), pltpu.SemaphoreType.DMA((2,))],
        ),
        out_shape=jax.ShapeDtypeStruct((n, D_MODEL), F32),
        compiler_params=_params("arbitrary"),
        name="combine_ple",
    )(pos_flat, ye, x1, wo, p, wproj, wgate, gple, gfin)


def _tile(n, pref):
    t = pref
    while n % t:
        t //= 2
    return t


def _seg(a, i):
    return a[..., _OFF[i]:_OFF[i + 1]]


def kernel(x_prompt, x_sample, state_conv, state_mlstm_C, state_mlstm_n, state_mlstm_m, state_gla_S, p_prompt, p_sample, g_mix, w_in, b_in, conv_w, conv_b, g_mnorm, w_a2, b_a2, g_gnorm, w_pa, w_pb, w_out, g_ffn, w_router, b_router, w_gate_up, b_gate_up, w_down, b_down, g_ple, w_ple_proj, w_ple_gate, g_final):
    assert g_mix.shape[0] == 1, "single-layer kernel"
    bp, tp, _ = x_prompt.shape
    bs, ts, _ = x_sample.shape

    big = (0, 1, 2, 5, 6, 7, 8, 10, 11, 12)
    w_big = jnp.concatenate([_seg(w_in[0], i) for i in big], axis=-1).astype(BF16)
    b_big = jnp.concatenate([_seg(b_in[0], i) for i in big], axis=-1)[None, :]
    small = (3, 4, 9)
    n_small = sum(SPLITS[i] for i in small)
    w_small = jnp.pad(jnp.concatenate([_seg(w_in[0], i) for i in small], axis=-1),
                      ((0, 0), (0, LANES - n_small))).astype(BF16)
    b_small = jnp.pad(jnp.concatenate([_seg(b_in[0], i) for i in small], axis=-1), (0, LANES - n_small))[None, :]
    w2 = jnp.pad(w_a2[0], ((2 * M_HEADS, LANES - 2 * M_HEADS - G_RANK), (0, 0))).astype(BF16)
    wr = jnp.pad(w_router[0], ((0, 0), (0, LANES - N_EXPERTS))).astype(BF16)
    br = jnp.pad(b_router[0], (0, LANES - N_EXPERTS), constant_values=NEG_BIG)[None, :]
    wpa, wpb, wout = w_pa[0].astype(BF16), w_pb[0].astype(BF16), w_out[0].astype(BF16)
    wproj, wgate = w_ple_proj[0].astype(BF16), w_ple_gate[0].astype(BF16)
    row = lambda a: a.reshape(1, -1)

    def mix_path(x3d, conv0, c0, n0, m0, s0):
        batch, T, _ = x3d.shape
        x = x3d.reshape(batch * T, D_MODEL)
        z, zs = _inproj(x, row(g_mix[0]), w_big, b_big, w_small, b_small, _tile(batch * T, 1024))
        L = min(MIX_CHUNK, T)
        conv0p = jnp.pad(conv0, ((0, 0), (SUBLANES - (CONV_W - 1), 0), (0, 0)))
        m0b = jnp.broadcast_to(m0[:, :, None, None], (batch, M_HEADS, SUBLANES, LANES))
        ya, yb, tail, c1, n1, m1, st1 = _mixers(
            z, zs, conv0p, c0, n0[:, :, None, :], m0b, jnp.swapaxes(s0, -1, -2),
            conv_w[0], row(conv_b[0]), row(g_mnorm[0]), w2, row(b_a2[0]), row(g_gnorm[0]), batch, T, L)
        new_state = (tail[:, SUBLANES - (CONV_W - 1):, :], c1, n1[:, :, 0, :], m1[:, :, 0, 0],
                     jnp.swapaxes(st1, -1, -2))
        return x, z, ya, yb, new_state

    zero_state = (jnp.zeros((bp, CONV_W - 1, 2 * M_DIM), F32), jnp.zeros((bp, M_HEADS, M_HD, M_HD), F32),
                  jnp.zeros((bp, M_HEADS, M_HD), F32), jnp.zeros((bp, M_HEADS), F32),
                  jnp.zeros((bp, G_HEADS, G_DK, G_DV), F32))
    xp, zp, yap, ybp, st_p = mix_path(x_prompt, *zero_state)
    xs_, zs_, yas, ybs, st_s = mix_path(x_sample, state_conv[0], state_mlstm_C[0], state_mlstm_n[0],
                                        state_mlstm_m[0], state_gla_S[0])

    n_p, n_s = xp.shape[0], xs_.shape[0]
    n_all = n_p + n_s
    tm = _tile(n_s, 256)
    assert n_p % tm == 0
    x1, xn, eo, wo, cnt = _outproj((yap, ybp, zp, xp), (yas, ybs, zs_, xs_), wpa, wpb, wout, row(g_ffn[0]), wr, br, tm)

    counts = cnt[0, :N_EXPERTS].astype(jnp.int32)
    padded = (counts + MOE_ROWS - 1) // MOE_ROWS * MOE_ROWS
    pend = jnp.cumsum(padded)
    pstart = pend - padded
    nk = n_all * TOP_K
    n_blocks = (nk + N_EXPERTS * (MOE_ROWS - 1) + MOE_ROWS - 1) // MOE_ROWS
    starts = jnp.arange(n_blocks, dtype=jnp.int32) * MOE_ROWS
    block_e = jnp.minimum(jnp.sum((pend[None, :] <= starts[:, None]).astype(jnp.int32), axis=1), N_EXPERTS - 1)
    n_used = (pend[-1:] // MOE_ROWS).astype(jnp.int32)
    e, rank = eo[:TOP_K], eo[TOP_K:2 * TOP_K]
    onehot = e[..., None] == jnp.arange(N_EXPERTS, dtype=jnp.int32)
    pos = (jnp.sum(jnp.where(onehot, pstart, 0), axis=-1) + rank).reshape(-1)
    fill = jnp.concatenate([pstart + counts, padded - counts, n_used]).astype(jnp.int32)

    xs_sorted = _dispatch(pos, fill, xn, n_all, n_blocks, tm)
    ye = _experts(block_e, n_used, xs_sorted, w_gate_up[0], b_gate_up[0][:, None, :], w_down[0],
                  b_down[0][:, None, :], MOE_ROWS)
    fin = lambda p, tok0: _combine(pos, ye, x1, wo, p.reshape(-1, PLE_DIM), wproj, wgate,
                                   row(g_ple[0]), row(g_final), tm, tok0)
    y_p = fin(p_prompt[0], 0).reshape(bp, tp, D_MODEL)
    y_s = fin(p_sample[0], n_p).reshape(bs, ts, D_MODEL)

    lead = lambda a: a[None]
    return (y_p, y_s) + tuple(lead(a) for a in st_p) + tuple(lead(a) for a in st_s)
```

```python
import functools

import numpy as np
import jax
import jax.numpy as jnp
from jax import lax
from jax.experimental import pallas as pl
from jax.experimental.pallas import tpu as pltpu

F32 = jnp.float32
BF16 = jnp.bfloat16

D_MODEL = 1024
EPS = 1e-6
M_HEADS = 4
M_DIM = D_MODEL
M_HD = M_DIM // M_HEADS
CONV_W = 4
G_HEADS = 4
G_KDIM = D_MODEL // 2
G_VDIM = D_MODEL
G_DK = G_KDIM // G_HEADS
G_DV = G_VDIM // G_HEADS
G_RANK = 16
G_TAU = 16.0
N_EXPERTS = 32
TOP_K = 4
D_FF = D_MODEL
SWIGLU_LIMIT = 7.0
SWIGLU_ALPHA = 1.702
PLE_DIM = 256
SPLITS = (M_DIM, M_DIM, M_DIM, M_HEADS, M_HEADS, M_DIM, G_KDIM, G_KDIM, G_VDIM, G_RANK, G_VDIM, D_MODEL, D_MODEL)
_OFF = np.concatenate([[0], np.cumsum(SPLITS)]).tolist()

LANES = 128
SUBLANES = 8
ROW_CHUNKS = D_MODEL // LANES
VMEM_LIMIT = 56 * 1024 * 1024
MIX_CHUNK = 256
MOE_ROWS = 256
NEG_BIG = -1e30


def _sigmoid(x):
    return 0.5 * jnp.tanh(0.5 * x) + 0.5


def _log_sigmoid(x):
    return jnp.minimum(x, 0.0) - jnp.log(1.0 + jnp.exp(-jnp.abs(x)))


def _rms(x, g):
    return x * lax.rsqrt(jnp.mean(x * x, axis=-1, keepdims=True) + EPS) * g


def _dot(a, b):
    return jnp.dot(a, b, preferred_element_type=F32)


def _dot_nt(a, b):
    return lax.dot_general(a, b, (((1,), (1,)), ((), ())), preferred_element_type=F32)


def _dot_tn(a, b):
    return lax.dot_general(a, b, (((0,), (0,)), ((), ())), preferred_element_type=F32)


def _params(*sem):
    return pltpu.CompilerParams(dimension_semantics=sem, vmem_limit_bytes=VMEM_LIMIT)


def _inproj_kernel(x_ref, g_ref, w_ref, b_ref, ws_ref, bs_ref, z_ref, zs_ref, h_scr):
    @pl.when(pl.program_id(1) == 0)
    def _():
        hb = _rms(x_ref[...], g_ref[...]).astype(BF16)
        h_scr[...] = hb
        zs_ref[...] = _dot(hb, ws_ref[...]) + bs_ref[...]

    z_ref[...] = (_dot(h_scr[...], w_ref[...]) + b_ref[...]).astype(BF16)


def _inproj(x, g, w_big, b_big, w_small, b_small, tm, tn=1024):
    n = x.shape[0]
    cols = w_big.shape[1]
    return pl.pallas_call(
        _inproj_kernel,
        grid=(n // tm, cols // tn),
        in_specs=[
            pl.BlockSpec((tm, D_MODEL), lambda i, j: (i, 0)),
            pl.BlockSpec((1, D_MODEL), lambda i, j: (0, 0)),
            pl.BlockSpec((D_MODEL, tn), lambda i, j: (0, j)),
            pl.BlockSpec((1, tn), lambda i, j: (0, j)),
            pl.BlockSpec((D_MODEL, LANES), lambda i, j: (0, 0)),
            pl.BlockSpec((1, LANES), lambda i, j: (0, 0)),
        ],
        out_specs=[
            pl.BlockSpec((tm, tn), lambda i, j: (i, j)),
            pl.BlockSpec((tm, LANES), lambda i, j: (i, 0)),
        ],
        out_shape=[jax.ShapeDtypeStruct((n, cols), BF16), jax.ShapeDtypeStruct((n, LANES), F32)],
        scratch_shapes=[pltpu.VMEM((tm, D_MODEL), BF16)],
        compiler_params=_params("parallel", "arbitrary"),
        name="inproj",
    )(x, g, w_big, b_big, w_small, b_small)


def _level_matrix(L):
    t = np.arange(L)[:, None]
    s = np.arange(L)[None, :]
    x = np.maximum(t ^ s, 1)
    lv = np.floor(np.log2(x)).astype(np.int32)
    lv = np.where(t == s, -1, lv)
    return np.where(s > t, -2, lv).astype(np.int32)


def _bcast_row(x, period, r):
    L, W = x.shape
    x3 = x.reshape(L // period, period, W)
    return jnp.broadcast_to(x3[:, r:r + 1, :], x3.shape).reshape(L, W)


def _segmented_scans(la, L):
    W = la.shape[1]
    row = lax.broadcasted_iota(jnp.int32, (L, W), 0)
    fwd = [la]
    rev = [la]
    b = 1
    while b < L:
        per = 2 * b
        res = row & (per - 1)
        f, r = fwd[-1], rev[-1]
        if per < SUBLANES:
            addf = jnp.zeros_like(la)
            addr = jnp.zeros_like(la)
            for j in range(b, per):
                addf = jnp.where(res == j, pltpu.roll(f, j - (b - 1), 0), addf)
            for j in range(0, b):
                addr = jnp.where(res == j, pltpu.roll(r, L - (b - j), 0), addr)
        else:
            addf = jnp.where(res >= b, _bcast_row(f, per, b - 1), 0.0)
            addr = jnp.where(res < b, _bcast_row(r, per, b), 0.0)
        fwd.append(f + addf)
        rev.append(r + addr)
        b = per
    return fwd, rev


def _mixer_kernel(zqk_ref, zv_ref, zo_ref, zgq_ref, zgk_ref, zgv_ref, zgg_ref, zs_ref,
                  conv0_ref, c0_ref, n0_ref, m0_ref, s0_ref,
                  cw_ref, cb_ref, gm_ref, w2_ref, ba2_ref, gg_ref, lv_ref,
                  ya_ref, yb_ref, tail_ref, c_ref, n_ref, m_ref, st_ref, ext_ref, *, L):
    @pl.when(pl.program_id(1) == 0)
    def _():
        tail_ref[...] = conv0_ref[...]
        c_ref[...] = c0_ref[...]
        n_ref[...] = n0_ref[...]
        m_ref[...] = m0_ref[...]
        st_ref[...] = s0_ref[...]

    lv = lv_ref[...]
    causal = lv >= -1

    u = zqk_ref[...].astype(F32)
    ext_ref[0:SUBLANES, :] = tail_ref[0]
    ext_ref[SUBLANES:SUBLANES + L, :] = u
    acc = cb_ref[...] + cw_ref[CONV_W - 1:CONV_W, :] * u
    for w in range(CONV_W - 1):
        acc = acc + cw_ref[w:w + 1, :] * ext_ref[pl.ds(SUBLANES - (CONV_W - 1) + w, L), :]
    tail_ref[0] = ext_ref[pl.ds(L, SUBLANES), :]
    qk = acc * _sigmoid(acc)

    zs = zs_ref[...]
    row = lax.broadcasted_iota(jnp.int32, (L, LANES), 0)
    col = lax.broadcasted_iota(jnp.int32, (L, LANES), 1)
    bcs = _log_sigmoid(zs)
    sh = 1
    while sh < L:
        bcs = bcs + jnp.where(row >= sh, pltpu.roll(bcs, sh, 0), 0.0)
        sh *= 2
    gates = jnp.where(col < M_HEADS, zs, bcs)
    if L < LANES:
        gates = jnp.concatenate([gates, jnp.zeros((LANES - L, LANES), F32)], axis=0)
    gt = gates.T[:, :L]

    for h in range(M_HEADS):
        hs = slice(h * M_HD, (h + 1) * M_HD)
        ig_col = zs[:, h:h + 1]
        b_col = bcs[:, M_HEADS + h:M_HEADS + h + 1]
        ig_row = gt[h:h + 1, :]
        b_row = gt[M_HEADS + h:M_HEADS + h + 1, :]
        m_prev = m_ref[0, h][0:1, 0:1]
        logd = jnp.where(causal, b_col - b_row + ig_row, -jnp.inf)
        inter = b_col + m_prev
        m_t = jnp.maximum(inter, jnp.max(logd, axis=-1, keepdims=True))
        q_f = qk[:, hs]
        k_f = qk[:, M_DIM + h * M_HD:M_DIM + (h + 1) * M_HD] * (M_HD ** -0.5)
        q_b = q_f.astype(BF16)
        v_b = zv_ref[:, hs]
        s = _dot_nt(q_b, k_f.astype(BF16)) * jnp.exp(logd - m_t)
        w_int = jnp.exp(inter - m_t)
        c_old = c_ref[0, h]
        n_old = n_ref[0, h]
        num = _dot(s.astype(BF16), v_b) + w_int * _dot(q_b, c_old.astype(BF16))
        den = jnp.sum(s, axis=-1, keepdims=True) + w_int * jnp.sum(q_f * n_old, axis=-1, keepdims=True)
        hh = num * (1.0 / jnp.maximum(jnp.abs(den), jnp.exp(-m_t)))
        m_new = m_t[L - 1:L, :]
        b_last = b_col[L - 1:L, :]
        w_k = jnp.exp(b_last - b_col + ig_col - m_new)
        dec = jnp.exp(b_last + m_prev - m_new)
        k_w = k_f * w_k
        c_ref[0, h] = dec * c_old + _dot_tn(k_w.astype(BF16), v_b)
        n_ref[0, h] = dec * n_old + jnp.sum(k_w, axis=0, keepdims=True)
        m_ref[0, h] = jnp.broadcast_to(m_new, (SUBLANES, LANES))
        hn = _rms(hh, gm_ref[:, hs])
        ya_ref[:, hs] = (hn * _sigmoid(zo_ref[:, hs].astype(F32))).astype(BF16)

    la = _log_sigmoid(_dot(zs.astype(BF16), w2_ref[...]) + ba2_ref[...]) * (1.0 / G_TAU)
    fwd, rev = _segmented_scans(la, L)
    nlev = len(fwd) - 1
    qg = zgq_ref[...].astype(F32) * (G_DK ** -0.5)
    kg = zgk_ref[...].astype(F32)
    qs = [(qg * jnp.exp(fwd[j])).astype(BF16) for j in range(nlev)]
    ks = [(kg * jnp.exp(rev[j] - la)).astype(BF16) for j in range(nlev)]
    q_in = (qg * jnp.exp(fwd[nlev])).astype(BF16)
    k_out = (kg * jnp.exp(rev[nlev] - la)).astype(BF16)
    q_d = qg.astype(BF16)
    k_d = kg.astype(BF16)
    dec_all = jnp.exp(fwd[nlev][L - 1:L, :])
    for h in range(G_HEADS):
        ks_ = slice(h * G_DK, (h + 1) * G_DK)
        vs_ = slice(h * G_DV, (h + 1) * G_DV)
        att = jnp.where(lv == -1, _dot_nt(q_d[:, ks_], k_d[:, ks_]), 0.0)
        for j in range(nlev):
            att = jnp.where(lv == j, _dot_nt(qs[j][:, ks_], ks[j][:, ks_]), att)
        v_b = zgv_ref[:, vs_]
        st_old = st_ref[0, h]
        o = _dot(att.astype(BF16), v_b) + _dot_nt(q_in[:, ks_], st_old.astype(BF16))
        st_ref[0, h] = st_old * dec_all[:, ks_] + _dot_tn(v_b, k_out[:, ks_])
        on = _rms(o, gg_ref[:, vs_])
        g = zgg_ref[:, vs_].astype(F32)
        yb_ref[:, vs_] = (on * (g * _sigmoid(g))).astype(BF16)


def _mixers(z, zs, conv0, c0, n0, m0, st0, cw, cb, gm, w2, ba2, gg, batch, T, L):
    nc = T // L
    lv = jnp.asarray(_level_matrix(L))
    row_blk = lambda width, cidx: pl.BlockSpec((L, width), lambda b, c: (b * nc + c, cidx))
    state = lambda *shape: pl.BlockSpec((1,) + shape, lambda b, c: (b,) + (0,) * len(shape))
    const = lambda *shape: pl.BlockSpec(shape, lambda b, c: (0,) * len(shape))
    n_tok = batch * T
    return pl.pallas_call(
        functools.partial(_mixer_kernel, L=L),
        grid=(batch, nc),
        in_specs=[
            row_blk(2 * M_DIM, 0),
            row_blk(M_DIM, 2),
            row_blk(M_DIM, 3),
            row_blk(G_KDIM, 8),
            row_blk(G_KDIM, 9),
            row_blk(G_VDIM, 5),
            row_blk(G_VDIM, 6),
            pl.BlockSpec((L, LANES), lambda b, c: (b * nc + c, 0)),
            state(SUBLANES, 2 * M_DIM),
            state(M_HEADS, M_HD, M_HD),
            state(M_HEADS, 1, M_HD),
            state(M_HEADS, SUBLANES, LANES),
            state(G_HEADS, G_DV, G_DK),
            const(CONV_W, 2 * M_DIM),
            const(1, 2 * M_DIM),
            const(1, M_DIM),
            const(LANES, G_KDIM),
            const(1, G_KDIM),
            const(1, G_VDIM),
            const(L, L),
        ],
        out_specs=[
            pl.BlockSpec((L, M_DIM), lambda b, c: (b * nc + c, 0)),
            pl.BlockSpec((L, G_VDIM), lambda b, c: (b * nc + c, 0)),
            state(SUBLANES, 2 * M_DIM),
            state(M_HEADS, M_HD, M_HD),
            state(M_HEADS, 1, M_HD),
            state(M_HEADS, SUBLANES, LANES),
            state(G_HEADS, G_DV, G_DK),
        ],
        out_shape=[
            jax.ShapeDtypeStruct((n_tok, M_DIM), BF16),
            jax.ShapeDtypeStruct((n_tok, G_VDIM), BF16),
            jax.ShapeDtypeStruct((batch, SUBLANES, 2 * M_DIM), F32),
            jax.ShapeDtypeStruct((batch, M_HEADS, M_HD, M_HD), F32),
            jax.ShapeDtypeStruct((batch, M_HEADS, 1, M_HD), F32),
            jax.ShapeDtypeStruct((batch, M_HEADS, SUBLANES, LANES), F32),
            jax.ShapeDtypeStruct((batch, G_HEADS, G_DV, G_DK), F32),
        ],
        scratch_shapes=[pltpu.VMEM((L + SUBLANES, 2 * M_DIM), F32)],
        compiler_params=_params("parallel", "arbitrary"),
        name="mixers",
    )(z, z, z, z, z, z, z, zs, conv0, c0, n0, m0, st0, cw, cb, gm, w2, ba2, gg, lv)


def _outproj_kernel(ya_p, yb_p, ua_p, ub_p, x_p, ya_s, yb_s, ua_s, ub_s, x_s,
                    wpa_ref, wpb_ref, wout_ref, gffn_ref, wr_ref, br_ref, tril_ref,
                    x1_ref, xn_ref, eo_ref, wo_ref, cnt_ref, *, tm, tiles_p, tiles_s):
    i = pl.program_id(0)

    @pl.when(i == 0)
    def _():
        cnt_ref[...] = jnp.zeros_like(cnt_ref)

    def body(ya_ref, yb_ref, ua_ref, ub_ref, x_ref):
        a = _dot(ya_ref[...], wpa_ref[...])
        b = _dot(yb_ref[...], wpb_ref[...])
        merged = _sigmoid(ua_ref[...].astype(F32)) * a + _sigmoid(ub_ref[...].astype(F32)) * b
        x1 = x_ref[...] + _dot(merged.astype(BF16), wout_ref[...])
        x1_ref[...] = x1
        xn = _rms(x1, gffn_ref[...])
        for j in range(ROW_CHUNKS):
            xn_ref[pl.ds(j, tm, stride=ROW_CHUNKS), :] = xn[:, j * LANES:(j + 1) * LANES]

        logits = _dot(xn.astype(BF16), wr_ref[...]) + br_ref[...]
        lane = lax.broadcasted_iota(jnp.int32, (tm, LANES), 1).astype(F32)
        vals, idxs = [], []
        l = logits
        for _ in range(TOP_K):
            mx = jnp.max(l, axis=-1, keepdims=True)
            ix = jnp.min(jnp.where(l == mx, lane, float(LANES)), axis=-1, keepdims=True)
            vals.append(mx)
            idxs.append(ix)
            l = jnp.where(lane == ix, -jnp.inf, l)
        ev = [jnp.exp(v - vals[0]) for v in vals]
        den = ev[0] + ev[1] + ev[2] + ev[3]

        cnt = jnp.zeros((tm, LANES), F32)
        for ix in idxs:
            cnt = cnt + jnp.where(lane == ix, 1.0, 0.0)
        pre = _dot(tril_ref[...], cnt.astype(BF16)) + cnt_ref[0:1, :]
        eo = jnp.zeros((tm, LANES), F32)
        wo = jnp.zeros((tm, LANES), F32)
        for k in range(TOP_K):
            rank = jnp.sum(jnp.where(lane == idxs[k], pre, 0.0), axis=-1, keepdims=True)
            eo = jnp.where(lane == float(k), idxs[k], eo)
            eo = jnp.where(lane == float(TOP_K + k), rank, eo)
            wo = jnp.where(lane == float(k), ev[k] / den, wo)
        eo_ref[...] = eo.T[0:SUBLANES, :].astype(jnp.int32)
        wo_ref[...] = wo
        cnt_ref[...] = cnt_ref[...] + jnp.sum(cnt, axis=0, keepdims=True)

    @pl.when(i < tiles_p)
    def _():
        body(ya_p, yb_p, ua_p, ub_p, x_p)

    @pl.when(jnp.logical_and(i >= tiles_p, i < tiles_p + tiles_s))
    def _():
        body(ya_s, yb_s, ua_s, ub_s, x_s)

    @pl.when(i == tiles_p + tiles_s)
    def _():
        xn_ref[...] = jnp.zeros_like(xn_ref)


def _outproj(prompt, sample, wpa, wpb, wout, gffn, wr, br, tm):
    n_p, n_s = prompt[3].shape[0], sample[3].shape[0]
    tiles_p, tiles_s = n_p // tm, n_s // tm
    tiles = tiles_p + tiles_s
    n = n_p + n_s
    tril = jnp.asarray(np.tril(np.ones((tm, tm), np.float32), -1), BF16)
    pidx = lambda i: jnp.minimum(i, tiles_p - 1)
    sidx = lambda i: jnp.clip(i - tiles_p, 0, tiles_s - 1)
    oidx = lambda i: jnp.minimum(i, tiles - 1)
    src = lambda idx: [pl.BlockSpec((tm, M_DIM), lambda i: (idx(i), 0)),
                       pl.BlockSpec((tm, G_VDIM), lambda i: (idx(i), 0)),
                       pl.BlockSpec((tm, D_MODEL), lambda i: (idx(i), 7)),
                       pl.BlockSpec((tm, D_MODEL), lambda i: (idx(i), 8)),
                       pl.BlockSpec((tm, D_MODEL), lambda i: (idx(i), 0))]
    const = lambda *shape: pl.BlockSpec(shape, lambda i: (0,) * len(shape))
    ya_p, yb_p, z_p, x_p = prompt
    ya_s, yb_s, z_s, x_s = sample
    return pl.pallas_call(
        functools.partial(_outproj_kernel, tm=tm, tiles_p=tiles_p, tiles_s=tiles_s),
        grid=(tiles + 1,),
        in_specs=src(pidx) + src(sidx) + [
            const(M_DIM, D_MODEL), const(G_VDIM, D_MODEL), const(D_MODEL, D_MODEL), const(1, D_MODEL),
            const(D_MODEL, LANES), const(1, LANES), const(tm, tm),
        ],
        out_specs=[
            pl.BlockSpec((tm, D_MODEL), lambda i: (oidx(i), 0)),
            pl.BlockSpec((tm * ROW_CHUNKS, LANES), lambda i: (i, 0)),
            pl.BlockSpec((SUBLANES, tm), lambda i: (0, oidx(i))),
            pl.BlockSpec((tm, LANES), lambda i: (oidx(i), 0)),
            const(SUBLANES, LANES),
        ],
        out_shape=[
            jax.ShapeDtypeStruct((n, D_MODEL), F32),
            jax.ShapeDtypeStruct(((n + tm) * ROW_CHUNKS, LANES), F32),
            jax.ShapeDtypeStruct((SUBLANES, n), jnp.int32),
            jax.ShapeDtypeStruct((n, LANES), F32),
            jax.ShapeDtypeStruct((SUBLANES, LANES), F32),
        ],
        compiler_params=_params("arbitrary"),
        name="outproj_router",
    )(ya_p, yb_p, z_p, z_p, x_p, ya_s, yb_s, z_s, z_s, x_s, wpa, wpb, wout, gffn, wr, br, tril)


def _row_copy(src, src_row, dst, dst_row, sem):
    return pltpu.make_async_copy(
        src.at[pl.ds(pl.multiple_of(src_row * ROW_CHUNKS, ROW_CHUNKS), ROW_CHUNKS)],
        dst.at[pl.ds(pl.multiple_of(dst_row * ROW_CHUNKS, ROW_CHUNKS), ROW_CHUNKS)], sem)


def _dispatch_kernel(pos_ref, fill_ref, xn_ref, xs_ref, zeros_ref, zsem, sem, *, tm, n, n_blocks):
    i = pl.program_id(0)
    base = i * tm

    def issue(r, carry):
        for k in range(TOP_K):
            _row_copy(xn_ref, r, xs_ref, pos_ref[k * n + base + r], sem).start(priority=k % 2)
        return carry

    lax.fori_loop(0, tm, issue, 0)
    for _ in range(TOP_K):
        pltpu.make_async_copy(xn_ref, xs_ref.at[pl.ds(0, tm * ROW_CHUNKS)], sem).wait()

    @pl.when(i == 0)
    def _():
        zeros_ref[...] = jnp.zeros_like(zeros_ref)
        bits = [1 << b for b in range(MOE_ROWS.bit_length() - 1)]

        def zero_copy(off, nrows):
            return pltpu.make_async_copy(
                zeros_ref.at[pl.ds(0, nrows * ROW_CHUNKS)],
                xs_ref.at[pl.ds(pl.multiple_of(off * ROW_CHUNKS, ROW_CHUNKS), nrows * ROW_CHUNKS)], zsem)

        def expert(e, carry):
            npad = fill_ref[N_EXPERTS + e]
            for wait in (False, True):
                off = fill_ref[e]
                for bit in bits:
                    @pl.when((npad & bit) != 0)
                    def _():
                        zero_copy(off, bit).wait() if wait else zero_copy(off, bit).start()
                    off = off + (npad & bit)
            return carry

        lax.fori_loop(0, N_EXPERTS, expert, 0)

        def block(b, carry):
            zero_copy(b * MOE_ROWS, MOE_ROWS).start()
            zero_copy(b * MOE_ROWS, MOE_ROWS).wait()
            return carry

        lax.fori_loop(fill_ref[2 * N_EXPERTS], n_blocks, block, 0)


def _dispatch(pos, fill, xn, n, n_blocks, tm):
    return pl.pallas_call(
        functools.partial(_dispatch_kernel, tm=tm, n=n, n_blocks=n_blocks),
        grid_spec=pltpu.PrefetchScalarGridSpec(
            num_scalar_prefetch=2,
            grid=(n // tm,),
            in_specs=[pl.BlockSpec((tm * ROW_CHUNKS, LANES), lambda i, *_: (i, 0))],
            out_specs=pl.BlockSpec(memory_space=pl.ANY),
            scratch_shapes=[pltpu.VMEM((MOE_ROWS * ROW_CHUNKS, LANES), F32), pltpu.SemaphoreType.DMA(()),
                            pltpu.SemaphoreType.DMA(())],
        ),
        out_shape=jax.ShapeDtypeStruct((n_blocks * MOE_ROWS * ROW_CHUNKS, LANES), F32),
        compiler_params=_params("arbitrary"),
        name="dispatch",
    )(pos, fill, xn)


def _expert_kernel(plan_ref, nused_ref, xs_ref, wgu_hbm, bgu_ref, wd_hbm, bd_ref, y_ref,
                   wgu_f32, wd_f32, wgu_bf, wd_bf, wsem, *, rows):
    i = pl.program_id(0)
    active = i < nused_ref[0]
    expert, next_expert, slot = plan_ref[0, i], plan_ref[1, i], plan_ref[2, i]
    first = jnp.logical_or(i == 0, expert != plan_ref[0, jnp.maximum(i - 1, 0)])

    def weight_copies(e, s):
        return (pltpu.make_async_copy(wgu_hbm.at[e], wgu_f32.at[s], wsem.at[0, s]),
                pltpu.make_async_copy(wd_hbm.at[e], wd_f32.at[s], wsem.at[1, s]))

    @pl.when(i == 0)
    def _():
        for c in weight_copies(expert, slot):
            c.start()

    @pl.when(jnp.logical_and(active, first))
    def _():
        for c in weight_copies(expert, slot):
            c.wait()
        wgu_bf[...] = wgu_f32[slot].astype(BF16)
        wd_bf[...] = wd_f32[slot].astype(BF16)

        @pl.when(next_expert < N_EXPERTS)
        def _():
            for c in weight_copies(next_expert, 1 - slot):
                c.start()

    @pl.when(active)
    def _():
        x = jnp.concatenate([xs_ref[pl.ds(j, rows, stride=ROW_CHUNKS), :] for j in range(ROW_CHUNKS)], axis=-1)
        zz = _dot(x.astype(BF16), wgu_bf[...]) + bgu_ref[...]
        g = jnp.minimum(zz[:, :D_FF], SWIGLU_LIMIT)
        u = jnp.clip(zz[:, D_FF:], -SWIGLU_LIMIT, SWIGLU_LIMIT)
        act = g * _sigmoid(SWIGLU_ALPHA * g) * (u + 1.0)
        y = _dot(act.astype(BF16), wd_bf[...]) + bd_ref[...]
        for j in range(ROW_CHUNKS):
            y_ref[pl.ds(j, rows, stride=ROW_CHUNKS), :] = y[:, j * LANES:(j + 1) * LANES]

    @pl.when(jnp.logical_not(active))
    def _():
        y_ref[...] = jnp.zeros_like(y_ref)


def _experts(plan, n_used, xs, wgu, bgu, wd, bd, rows):
    nb = plan.shape[1]
    return pl.pallas_call(
        functools.partial(_expert_kernel, rows=rows),
        grid_spec=pltpu.PrefetchScalarGridSpec(
            num_scalar_prefetch=2,
            grid=(nb,),
            in_specs=[
                pl.BlockSpec((rows * ROW_CHUNKS, LANES), lambda i, plan, nu: (i, 0)),
                pl.BlockSpec(memory_space=pl.ANY),
                pl.BlockSpec((None, 1, 2 * D_FF), lambda i, plan, nu: (plan[0, i], 0, 0)),
                pl.BlockSpec(memory_space=pl.ANY),
                pl.BlockSpec((None, 1, D_MODEL), lambda i, plan, nu: (plan[0, i], 0, 0)),
            ],
            out_specs=pl.BlockSpec((rows * ROW_CHUNKS, LANES), lambda i, plan, nu: (i, 0)),
            scratch_shapes=[pltpu.VMEM((2, D_MODEL, 2 * D_FF), F32), pltpu.VMEM((2, D_FF, D_MODEL), F32),
                            pltpu.VMEM((D_MODEL, 2 * D_FF), BF16), pltpu.VMEM((D_FF, D_MODEL), BF16),
                            pltpu.SemaphoreType.DMA((2, 2))],
        ),
        out_shape=jax.ShapeDtypeStruct(xs.shape, F32),
        compiler_params=_params("arbitrary"),
        name="experts",
    )(plan, n_used, xs, wgu, bgu, wd, bd)


def _combine_kernel(pos_ref, ye_ref, x1_ref, wo_ref, p_ref, wproj_ref, wgate_ref, gple_ref, gfin_ref,
                    y_ref, rows_scr, sem, *, tm, nt, n_all, tok0):
    i = pl.program_id(0)

    def gather(tile, slot):
        def issue(r, carry):
            for k in range(TOP_K):
                _row_copy(ye_ref, pos_ref[k * n_all + tok0 + tile * tm + r], rows_scr, (slot * TOP_K + k) * tm + r,
                          sem.at[slot]).start(priority=k % 2)
            return carry
        lax.fori_loop(0, tm, issue, 0)

    @pl.when(i == 0)
    def _():
        gather(0, 0)

    @pl.when(i + 1 < nt)
    def _():
        gather(i + 1, (i + 1) % 2)

    slot = i % 2
    seg = tm * ROW_CHUNKS
    for k in range(TOP_K):
        off = pl.multiple_of((slot * TOP_K + k) * seg, seg)
        pltpu.make_async_copy(ye_ref.at[pl.ds(0, seg)], rows_scr.at[pl.ds(off, seg)], sem.at[slot]).wait()

    wo = wo_ref[...]
    moe = jnp.zeros((tm, D_MODEL), F32)
    for k in range(TOP_K):
        off = pl.multiple_of((slot * TOP_K + k) * seg, seg)
        rows = jnp.concatenate(
            [rows_scr[pl.ds(off + j, tm, stride=ROW_CHUNKS), :] for j in range(ROW_CHUNKS)], axis=-1)
        moe = moe + rows * wo[:, k:k + 1]
    x2 = x1_ref[...] + moe
    gate = _sigmoid(_dot(_rms(x2, gple_ref[...]).astype(BF16), wgate_ref[...]))
    x3 = x2 + _dot(p_ref[...].astype(BF16), wproj_ref[...]) * gate
    y_ref[...] = _rms(x3, gfin_ref[...])


def _combine(pos_flat, ye, x1, wo, p, wproj, wgate, gple, gfin, tm, tok0):
    n = p.shape[0]
    n_all = x1.shape[0]
    blk0 = tok0 // tm
    tok = lambda width: pl.BlockSpec((tm, width), lambda i, pos: (blk0 + i, 0))
    const = lambda *shape: pl.BlockSpec(shape, lambda i, pos: (0,) * len(shape))
    return pl.pallas_call(
        functools.partial(_combine_kernel, tm=tm, nt=n // tm, n_all=n_all, tok0=tok0),
        grid_spec=pltpu.PrefetchScalarGridSpec(
            num_scalar_prefetch=1,
            grid=(n // tm,),
            in_specs=[
                pl.BlockSpec(memory_space=pl.ANY),
                tok(D_MODEL), tok(LANES), pl.BlockSpec((tm, PLE_DIM), lambda i, pos: (i, 0)),
                const(PLE_DIM, D_MODEL), const(D_MODEL, D_MODEL), const(1, D_MODEL), const(1, D_MODEL),
            ],
            out_specs=pl.BlockSpec((tm, D_MODEL), lambda i, pos: (i, 0)),
            scratch_shapes=[pltpu.VMEM((2 * TOP_K * tm * ROW_CHUNKS, LANES), F32), pltpu.SemaphoreType.DMA((2,))],
        ),
        out_shape=jax.ShapeDtypeStruct((n, D_MODEL), F32),
        compiler_params=_params("arbitrary"),
        name="combine_ple",
    )(pos_flat, ye, x1, wo, p, wproj, wgate, gple, gfin)


def _tile(n, pref):
    t = pref
    while n % t:
        t //= 2
    return t


def _seg(a, i):
    return a[..., _OFF[i]:_OFF[i + 1]]


def kernel(x_prompt, x_sample, state_conv, state_mlstm_C, state_mlstm_n, state_mlstm_m, state_gla_S, p_prompt, p_sample, g_mix, w_in, b_in, conv_w, conv_b, g_mnorm, w_a2, b_a2, g_gnorm, w_pa, w_pb, w_out, g_ffn, w_router, b_router, w_gate_up, b_gate_up, w_down, b_down, g_ple, w_ple_proj, w_ple_gate, g_final):
    assert g_mix.shape[0] == 1, "single-layer kernel"
    bp, tp, _ = x_prompt.shape
    bs, ts, _ = x_sample.shape

    big = (0, 1, 2, 5, 6, 7, 8, 10, 11, 12)
    w_big = jnp.concatenate([_seg(w_in[0], i) for i in big], axis=-1).astype(BF16)
    b_big = jnp.concatenate([_seg(b_in[0], i) for i in big], axis=-1)[None, :]
    small = (3, 4, 9)
    n_small = sum(SPLITS[i] for i in small)
    w_small = jnp.pad(jnp.concatenate([_seg(w_in[0], i) for i in small], axis=-1),
                      ((0, 0), (0, LANES - n_small))).astype(BF16)
    b_small = jnp.pad(jnp.concatenate([_seg(b_in[0], i) for i in small], axis=-1), (0, LANES - n_small))[None, :]
    w2 = jnp.pad(w_a2[0], ((2 * M_HEADS, LANES - 2 * M_HEADS - G_RANK), (0, 0))).astype(BF16)
    wr = jnp.pad(w_router[0], ((0, 0), (0, LANES - N_EXPERTS))).astype(BF16)
    br = jnp.pad(b_router[0], (0, LANES - N_EXPERTS), constant_values=NEG_BIG)[None, :]
    wpa, wpb, wout = w_pa[0].astype(BF16), w_pb[0].astype(BF16), w_out[0].astype(BF16)
    wproj, wgate = w_ple_proj[0].astype(BF16), w_ple_gate[0].astype(BF16)
    row = lambda a: a.reshape(1, -1)

    def mix_path(x3d, conv0, c0, n0, m0, s0):
        batch, T, _ = x3d.shape
        x = x3d.reshape(batch * T, D_MODEL)
        z, zs = _inproj(x, row(g_mix[0]), w_big, b_big, w_small, b_small, _tile(batch * T, 1024))
        L = min(MIX_CHUNK, T)
        conv0p = jnp.pad(conv0, ((0, 0), (SUBLANES - (CONV_W - 1), 0), (0, 0)))
        m0b = jnp.broadcast_to(m0[:, :, None, None], (batch, M_HEADS, SUBLANES, LANES))
        ya, yb, tail, c1, n1, m1, st1 = _mixers(
            z, zs, conv0p, c0, n0[:, :, None, :], m0b, jnp.swapaxes(s0, -1, -2),
            conv_w[0], row(conv_b[0]), row(g_mnorm[0]), w2, row(b_a2[0]), row(g_gnorm[0]), batch, T, L)
        new_state = (tail[:, SUBLANES - (CONV_W - 1):, :], c1, n1[:, :, 0, :], m1[:, :, 0, 0],
                     jnp.swapaxes(st1, -1, -2))
        return x, z, ya, yb, new_state

    zero_state = (jnp.zeros((bp, CONV_W - 1, 2 * M_DIM), F32), jnp.zeros((bp, M_HEADS, M_HD, M_HD), F32),
                  jnp.zeros((bp, M_HEADS, M_HD), F32), jnp.zeros((bp, M_HEADS), F32),
                  jnp.zeros((bp, G_HEADS, G_DK, G_DV), F32))
    xp, zp, yap, ybp, st_p = mix_path(x_prompt, *zero_state)
    xs_, zs_, yas, ybs, st_s = mix_path(x_sample, state_conv[0], state_mlstm_C[0], state_mlstm_n[0],
                                        state_mlstm_m[0], state_gla_S[0])

    n_p, n_s = xp.shape[0], xs_.shape[0]
    n_all = n_p + n_s
    tm = _tile(n_s, 256)
    assert n_p % tm == 0
    x1, xn, eo, wo, cnt = _outproj((yap, ybp, zp, xp), (yas, ybs, zs_, xs_), wpa, wpb, wout, row(g_ffn[0]), wr, br, tm)

    counts = cnt[0, :N_EXPERTS].astype(jnp.int32)
    padded = (counts + MOE_ROWS - 1) // MOE_ROWS * MOE_ROWS
    pend = jnp.cumsum(padded)
    pstart = pend - padded
    nk = n_all * TOP_K
    n_blocks = (nk + N_EXPERTS * (MOE_ROWS - 1) + MOE_ROWS - 1) // MOE_ROWS
    starts = jnp.arange(n_blocks, dtype=jnp.int32) * MOE_ROWS
    block_e = jnp.minimum(jnp.sum((pend[None, :] <= starts[:, None]).astype(jnp.int32), axis=1), N_EXPERTS - 1)
    ids = jnp.arange(N_EXPERTS, dtype=jnp.int32)
    later = jnp.where((ids[None, :] > ids[:, None]) & (counts[None, :] > 0), ids[None, :], N_EXPERTS)
    next_e = jnp.min(later, axis=1)
    group = jnp.cumsum(jnp.concatenate([jnp.zeros((1,), jnp.int32),
                                        (block_e[1:] != block_e[:-1]).astype(jnp.int32)]))
    plan = jnp.stack([block_e, next_e[block_e], group % 2]).astype(jnp.int32)
    n_used = (pend[-1:] // MOE_ROWS).astype(jnp.int32)
    e, rank = eo[:TOP_K], eo[TOP_K:2 * TOP_K]
    onehot = e[..., None] == jnp.arange(N_EXPERTS, dtype=jnp.int32)
    pos = (jnp.sum(jnp.where(onehot, pstart, 0), axis=-1) + rank).reshape(-1)
    fill = jnp.concatenate([pstart + counts, padded - counts, n_used]).astype(jnp.int32)

    xs_sorted = _dispatch(pos, fill, xn, n_all, n_blocks, tm)
    ye = _experts(plan, n_used, xs_sorted, w_gate_up[0], b_gate_up[0][:, None, :], w_down[0],
                  b_down[0][:, None, :], MOE_ROWS)
    fin = lambda p, tok0: _combine(pos, ye, x1, wo, p.reshape(-1, PLE_DIM), wproj, wgate,
                                   row(g_ple[0]), row(g_final), tm, tok0)
    y_p = fin(p_prompt[0], 0).reshape(bp, tp, D_MODEL)
    y_s = fin(p_sample[0], n_p).reshape(bs, ts, D_MODEL)

    lead = lambda a: a[None]
    return (y_p, y_s) + tuple(lead(a) for a in st_p) + tuple(lead(a) for a in st_s)
```

```python
import functools

import numpy as np
import jax
import jax.numpy as jnp
from jax import lax
from jax.experimental import pallas as pl
from jax.experimental.pallas import tpu as pltpu

F32 = jnp.float32
BF16 = jnp.bfloat16

D_MODEL = 1024
EPS = 1e-6
M_HEADS = 4
M_DIM = D_MODEL
M_HD = M_DIM // M_HEADS
CONV_W = 4
G_HEADS = 4
G_KDIM = D_MODEL // 2
G_VDIM = D_MODEL
G_DK = G_KDIM // G_HEADS
G_DV = G_VDIM // G_HEADS
G_RANK = 16
G_TAU = 16.0
N_EXPERTS = 32
TOP_K = 4
D_FF = D_MODEL
SWIGLU_LIMIT = 7.0
SWIGLU_ALPHA = 1.702
PLE_DIM = 256
SPLITS = (M_DIM, M_DIM, M_DIM, M_HEADS, M_HEADS, M_DIM, G_KDIM, G_KDIM, G_VDIM, G_RANK, G_VDIM, D_MODEL, D_MODEL)
_OFF = np.concatenate([[0], np.cumsum(SPLITS)]).tolist()

LANES = 128
SUBLANES = 8
ROW_CHUNKS = D_MODEL // LANES
VMEM_LIMIT = 56 * 1024 * 1024
MIX_CHUNK = 256
MOE_ROWS = 256
GATHER_UNROLL = 4
NEG_BIG = -1e30


def _sigmoid(x):
    return 0.5 * jnp.tanh(0.5 * x) + 0.5


def _log_sigmoid(x):
    return jnp.minimum(x, 0.0) - jnp.log(1.0 + jnp.exp(-jnp.abs(x)))


def _rms(x, g):
    return x * lax.rsqrt(jnp.mean(x * x, axis=-1, keepdims=True) + EPS) * g


def _dot(a, b):
    return jnp.dot(a, b, preferred_element_type=F32)


def _dot_nt(a, b):
    return lax.dot_general(a, b, (((1,), (1,)), ((), ())), preferred_element_type=F32)


def _dot_tn(a, b):
    return lax.dot_general(a, b, (((0,), (0,)), ((), ())), preferred_element_type=F32)


def _params(*sem):
    return pltpu.CompilerParams(dimension_semantics=sem, vmem_limit_bytes=VMEM_LIMIT)


def _inproj_kernel(x_ref, g_ref, w_ref, b_ref, ws_ref, bs_ref, z_ref, zs_ref, h_scr):
    @pl.when(pl.program_id(1) == 0)
    def _():
        hb = _rms(x_ref[...], g_ref[...]).astype(BF16)
        h_scr[...] = hb
        zs_ref[...] = _dot(hb, ws_ref[...]) + bs_ref[...]

    z_ref[...] = (_dot(h_scr[...], w_ref[...]) + b_ref[...]).astype(BF16)


def _inproj(x, g, w_big, b_big, w_small, b_small, tm, tn=1024):
    n = x.shape[0]
    cols = w_big.shape[1]
    return pl.pallas_call(
        _inproj_kernel,
        grid=(n // tm, cols // tn),
        in_specs=[
            pl.BlockSpec((tm, D_MODEL), lambda i, j: (i, 0)),
            pl.BlockSpec((1, D_MODEL), lambda i, j: (0, 0)),
            pl.BlockSpec((D_MODEL, tn), lambda i, j: (0, j)),
            pl.BlockSpec((1, tn), lambda i, j: (0, j)),
            pl.BlockSpec((D_MODEL, LANES), lambda i, j: (0, 0)),
            pl.BlockSpec((1, LANES), lambda i, j: (0, 0)),
        ],
        out_specs=[
            pl.BlockSpec((tm, tn), lambda i, j: (i, j)),
            pl.BlockSpec((tm, LANES), lambda i, j: (i, 0)),
        ],
        out_shape=[jax.ShapeDtypeStruct((n, cols), BF16), jax.ShapeDtypeStruct((n, LANES), F32)],
        scratch_shapes=[pltpu.VMEM((tm, D_MODEL), BF16)],
        compiler_params=_params("parallel", "arbitrary"),
        name="inproj",
    )(x, g, w_big, b_big, w_small, b_small)


def _level_matrix(L):
    t = np.arange(L)[:, None]
    s = np.arange(L)[None, :]
    x = np.maximum(t ^ s, 1)
    lv = np.floor(np.log2(x)).astype(np.int32)
    lv = np.where(t == s, -1, lv)
    return np.where(s > t, -2, lv).astype(np.int32)


def _bcast_row(x, period, r):
    L, W = x.shape
    x3 = x.reshape(L // period, period, W)
    return jnp.broadcast_to(x3[:, r:r + 1, :], x3.shape).reshape(L, W)


def _segmented_scans(la, L):
    W = la.shape[1]
    row = lax.broadcasted_iota(jnp.int32, (L, W), 0)
    fwd = [la]
    rev = [la]
    b = 1
    while b < L:
        per = 2 * b
        res = row & (per - 1)
        f, r = fwd[-1], rev[-1]
        if per < SUBLANES:
            addf = jnp.zeros_like(la)
            addr = jnp.zeros_like(la)
            for j in range(b, per):
                addf = jnp.where(res == j, pltpu.roll(f, j - (b - 1), 0), addf)
            for j in range(0, b):
                addr = jnp.where(res == j, pltpu.roll(r, L - (b - j), 0), addr)
            fwd.append(f + addf)
            rev.append(r + addr)
        elif b < SUBLANES:
            fwd.append(f + jnp.where(res >= b, _bcast_row(f, per, b - 1), 0.0))
            rev.append(r + jnp.where(res < b, _bcast_row(r, per, b), 0.0))
        else:
            f3 = f.reshape(L // per, per, W)
            r3 = r.reshape(L // per, per, W)
            fwd.append(jnp.concatenate([f3[:, :b], f3[:, b:] + f3[:, b - 1:b]], axis=1).reshape(L, W))
            rev.append(jnp.concatenate([r3[:, :b] + r3[:, b:b + 1], r3[:, b:]], axis=1).reshape(L, W))
        b = per
    return fwd, rev


def _mixer_kernel(zqk_ref, zv_ref, zo_ref, zgq_ref, zgk_ref, zgv_ref, zgg_ref, zs_ref,
                  conv0_ref, c0_ref, n0_ref, m0_ref, s0_ref,
                  cw_ref, cb_ref, gm_ref, w2_ref, ba2_ref, gg_ref, lv_ref,
                  ya_ref, yb_ref, tail_ref, c_ref, n_ref, m_ref, st_ref, ext_ref, *, L):
    @pl.when(pl.program_id(1) == 0)
    def _():
        tail_ref[...] = conv0_ref[...]
        c_ref[...] = c0_ref[...]
        n_ref[...] = n0_ref[...]
        m_ref[...] = m0_ref[...]
        st_ref[...] = s0_ref[...]

    lv = lv_ref[...]
    causal = lv >= -1

    u = zqk_ref[...].astype(F32)
    ext_ref[0:SUBLANES, :] = tail_ref[0]
    ext_ref[SUBLANES:SUBLANES + L, :] = u
    acc = cb_ref[...] + cw_ref[CONV_W - 1:CONV_W, :] * u
    for w in range(CONV_W - 1):
        acc = acc + cw_ref[w:w + 1, :] * ext_ref[pl.ds(SUBLANES - (CONV_W - 1) + w, L), :]
    tail_ref[0] = ext_ref[pl.ds(L, SUBLANES), :]
    qk = acc * _sigmoid(acc)

    zs = zs_ref[...]
    row = lax.broadcasted_iota(jnp.int32, (L, LANES), 0)
    col = lax.broadcasted_iota(jnp.int32, (L, LANES), 1)
    bcs = _log_sigmoid(zs)
    sh = 1
    while sh < L:
        bcs = bcs + jnp.where(row >= sh, pltpu.roll(bcs, sh, 0), 0.0)
        sh *= 2
    gates = jnp.where(col < M_HEADS, zs, bcs)
    if L < LANES:
        gates = jnp.concatenate([gates, jnp.zeros((LANES - L, LANES), F32)], axis=0)
    gt = gates.T[:, :L]

    for h in range(M_HEADS):
        hs = slice(h * M_HD, (h + 1) * M_HD)
        ig_col = zs[:, h:h + 1]
        b_col = bcs[:, M_HEADS + h:M_HEADS + h + 1]
        ig_row = gt[h:h + 1, :]
        b_row = gt[M_HEADS + h:M_HEADS + h + 1, :]
        m_prev = m_ref[0, h][0:1, 0:1]
        logd = jnp.where(causal, b_col - b_row + ig_row, -jnp.inf)
        inter = b_col + m_prev
        m_t = jnp.maximum(inter, jnp.max(logd, axis=-1, keepdims=True))
        q_f = qk[:, hs]
        k_f = qk[:, M_DIM + h * M_HD:M_DIM + (h + 1) * M_HD] * (M_HD ** -0.5)
        q_b = q_f.astype(BF16)
        v_b = zv_ref[:, hs]
        s = _dot_nt(q_b, k_f.astype(BF16)) * jnp.exp(logd - m_t)
        w_int = jnp.exp(inter - m_t)
        c_old = c_ref[0, h]
        n_old = n_ref[0, h]
        num = _dot(s.astype(BF16), v_b) + w_int * _dot(q_b, c_old.astype(BF16))
        den = jnp.sum(s, axis=-1, keepdims=True) + w_int * jnp.sum(q_f * n_old, axis=-1, keepdims=True)
        hh = num * (1.0 / jnp.maximum(jnp.abs(den), jnp.exp(-m_t)))
        m_new = m_t[L - 1:L, :]
        b_last = b_col[L - 1:L, :]
        w_k = jnp.exp(b_last - b_col + ig_col - m_new)
        dec = jnp.exp(b_last + m_prev - m_new)
        k_w = k_f * w_k
        c_ref[0, h] = dec * c_old + _dot_tn(k_w.astype(BF16), v_b)
        n_ref[0, h] = dec * n_old + jnp.sum(k_w, axis=0, keepdims=True)
        m_ref[0, h] = jnp.broadcast_to(m_new, (SUBLANES, LANES))
        hn = _rms(hh, gm_ref[:, hs])
        ya_ref[:, hs] = (hn * _sigmoid(zo_ref[:, hs].astype(F32))).astype(BF16)

    la = _log_sigmoid(_dot(zs.astype(BF16), w2_ref[...]) + ba2_ref[...]) * (1.0 / G_TAU)
    fwd, rev = _segmented_scans(la, L)
    nlev = len(fwd) - 1
    qg = zgq_ref[...].astype(F32) * (G_DK ** -0.5)
    kg = zgk_ref[...].astype(F32)
    q_d = qg.astype(BF16)
    k_d = zgk_ref[...]
    qs = [(qg * jnp.exp(fwd[j])).astype(BF16) for j in range(nlev)]
    ks = [k_d] + [(kg * jnp.exp(rev[j] - la)).astype(BF16) for j in range(1, nlev)]
    q_in = (qg * jnp.exp(fwd[nlev])).astype(BF16)
    k_out = (kg * jnp.exp(rev[nlev] - la)).astype(BF16)
    dec_all = jnp.exp(fwd[nlev][L - 1:L, :])
    for h in range(G_HEADS):
        ks_ = slice(h * G_DK, (h + 1) * G_DK)
        vs_ = slice(h * G_DV, (h + 1) * G_DV)
        att = jnp.where(lv == -1, _dot_nt(q_d[:, ks_], k_d[:, ks_]), 0.0)
        for j in range(nlev):
            att = jnp.where(lv == j, _dot_nt(qs[j][:, ks_], ks[j][:, ks_]), att)
        v_b = zgv_ref[:, vs_]
        st_old = st_ref[0, h]
        o = _dot(att.astype(BF16), v_b) + _dot_nt(q_in[:, ks_], st_old.astype(BF16))
        st_ref[0, h] = st_old * dec_all[:, ks_] + _dot_tn(v_b, k_out[:, ks_])
        on = _rms(o, gg_ref[:, vs_])
        g = zgg_ref[:, vs_].astype(F32)
        yb_ref[:, vs_] = (on * (g * _sigmoid(g))).astype(BF16)


def _mixers(z, zs, conv0, c0, n0, m0, st0, cw, cb, gm, w2, ba2, gg, batch, T, L):
    nc = T // L
    lv = jnp.asarray(_level_matrix(L))
    row_blk = lambda width, cidx: pl.BlockSpec((L, width), lambda b, c: (b * nc + c, cidx))
    state = lambda *shape: pl.BlockSpec((1,) + shape, lambda b, c: (b,) + (0,) * len(shape))
    const = lambda *shape: pl.BlockSpec(shape, lambda b, c: (0,) * len(shape))
    n_tok = batch * T
    return pl.pallas_call(
        functools.partial(_mixer_kernel, L=L),
        grid=(batch, nc),
        in_specs=[
            row_blk(2 * M_DIM, 0),
            row_blk(M_DIM, 2),
            row_blk(M_DIM, 3),
            row_blk(G_KDIM, 8),
            row_blk(G_KDIM, 9),
            row_blk(G_VDIM, 5),
            row_blk(G_VDIM, 6),
            pl.BlockSpec((L, LANES), lambda b, c: (b * nc + c, 0)),
            state(SUBLANES, 2 * M_DIM),
            state(M_HEADS, M_HD, M_HD),
            state(M_HEADS, 1, M_HD),
            state(M_HEADS, SUBLANES, LANES),
            state(G_HEADS, G_DV, G_DK),
            const(CONV_W, 2 * M_DIM),
            const(1, 2 * M_DIM),
            const(1, M_DIM),
            const(LANES, G_KDIM),
            const(1, G_KDIM),
            const(1, G_VDIM),
            const(L, L),
        ],
        out_specs=[
            pl.BlockSpec((L, M_DIM), lambda b, c: (b * nc + c, 0)),
            pl.BlockSpec((L, G_VDIM), lambda b, c: (b * nc + c, 0)),
            state(SUBLANES, 2 * M_DIM),
            state(M_HEADS, M_HD, M_HD),
            state(M_HEADS, 1, M_HD),
            state(M_HEADS, SUBLANES, LANES),
            state(G_HEADS, G_DV, G_DK),
        ],
        out_shape=[
            jax.ShapeDtypeStruct((n_tok, M_DIM), BF16),
            jax.ShapeDtypeStruct((n_tok, G_VDIM), BF16),
            jax.ShapeDtypeStruct((batch, SUBLANES, 2 * M_DIM), F32),
            jax.ShapeDtypeStruct((batch, M_HEADS, M_HD, M_HD), F32),
            jax.ShapeDtypeStruct((batch, M_HEADS, 1, M_HD), F32),
            jax.ShapeDtypeStruct((batch, M_HEADS, SUBLANES, LANES), F32),
            jax.ShapeDtypeStruct((batch, G_HEADS, G_DV, G_DK), F32),
        ],
        scratch_shapes=[pltpu.VMEM((L + SUBLANES, 2 * M_DIM), F32)],
        compiler_params=_params("parallel", "arbitrary"),
        name="mixers",
    )(z, z, z, z, z, z, z, zs, conv0, c0, n0, m0, st0, cw, cb, gm, w2, ba2, gg, lv)


def _outproj_kernel(ya_p, yb_p, ua_p, ub_p, x_p, ya_s, yb_s, ua_s, ub_s, x_s,
                    wpa_ref, wpb_ref, wout_ref, gffn_ref, wr_ref, br_ref, tril_ref,
                    x1_ref, xn_ref, eo_ref, wo_ref, cnt_ref, *, tm, tiles_p, tiles_s):
    i = pl.program_id(0)

    @pl.when(i == 0)
    def _():
        cnt_ref[...] = jnp.zeros_like(cnt_ref)

    def body(ya_ref, yb_ref, ua_ref, ub_ref, x_ref):
        a = _dot(ya_ref[...], wpa_ref[...])
        b = _dot(yb_ref[...], wpb_ref[...])
        merged = _sigmoid(ua_ref[...].astype(F32)) * a + _sigmoid(ub_ref[...].astype(F32)) * b
        x1 = x_ref[...] + _dot(merged.astype(BF16), wout_ref[...])
        x1_ref[...] = x1
        xn = _rms(x1, gffn_ref[...])
        for j in range(ROW_CHUNKS):
            xn_ref[pl.ds(j, tm, stride=ROW_CHUNKS), :] = xn[:, j * LANES:(j + 1) * LANES]

        logits = _dot(xn.astype(BF16), wr_ref[...]) + br_ref[...]
        lane = lax.broadcasted_iota(jnp.int32, (tm, LANES), 1).astype(F32)
        vals, idxs = [], []
        l = logits
        for _ in range(TOP_K):
            mx = jnp.max(l, axis=-1, keepdims=True)
            ix = jnp.min(jnp.where(l == mx, lane, float(LANES)), axis=-1, keepdims=True)
            vals.append(mx)
            idxs.append(ix)
            l = jnp.where(lane == ix, -jnp.inf, l)
        ev = [jnp.exp(v - vals[0]) for v in vals]
        den = ev[0] + ev[1] + ev[2] + ev[3]

        cnt = jnp.zeros((tm, LANES), F32)
        for ix in idxs:
            cnt = cnt + jnp.where(lane == ix, 1.0, 0.0)
        pre = _dot(tril_ref[...], cnt.astype(BF16)) + cnt_ref[0:1, :]
        eo = jnp.zeros((tm, LANES), F32)
        wo = jnp.zeros((tm, LANES), F32)
        for k in range(TOP_K):
            rank = jnp.sum(jnp.where(lane == idxs[k], pre, 0.0), axis=-1, keepdims=True)
            eo = jnp.where(lane == float(k), idxs[k], eo)
            eo = jnp.where(lane == float(TOP_K + k), rank, eo)
            wo = jnp.where(lane == float(k), ev[k] / den, wo)
        eo_ref[...] = eo.T[0:SUBLANES, :].astype(jnp.int32)
        wo_ref[...] = wo
        cnt_ref[...] = cnt_ref[...] + jnp.sum(cnt, axis=0, keepdims=True)

    @pl.when(i < tiles_p)
    def _():
        body(ya_p, yb_p, ua_p, ub_p, x_p)

    @pl.when(jnp.logical_and(i >= tiles_p, i < tiles_p + tiles_s))
    def _():
        body(ya_s, yb_s, ua_s, ub_s, x_s)

    @pl.when(i == tiles_p + tiles_s)
    def _():
        xn_ref[...] = jnp.zeros_like(xn_ref)


def _outproj(prompt, sample, wpa, wpb, wout, gffn, wr, br, tm):
    n_p, n_s = prompt[3].shape[0], sample[3].shape[0]
    tiles_p, tiles_s = n_p // tm, n_s // tm
    tiles = tiles_p + tiles_s
    n = n_p + n_s
    tril = jnp.asarray(np.tril(np.ones((tm, tm), np.float32), -1), BF16)
    pidx = lambda i: jnp.minimum(i, tiles_p - 1)
    sidx = lambda i: jnp.clip(i - tiles_p, 0, tiles_s - 1)
    oidx = lambda i: jnp.minimum(i, tiles - 1)
    src = lambda idx: [pl.BlockSpec((tm, M_DIM), lambda i: (idx(i), 0)),
                       pl.BlockSpec((tm, G_VDIM), lambda i: (idx(i), 0)),
                       pl.BlockSpec((tm, D_MODEL), lambda i: (idx(i), 7)),
                       pl.BlockSpec((tm, D_MODEL), lambda i: (idx(i), 8)),
                       pl.BlockSpec((tm, D_MODEL), lambda i: (idx(i), 0))]
    const = lambda *shape: pl.BlockSpec(shape, lambda i: (0,) * len(shape))
    ya_p, yb_p, z_p, x_p = prompt
    ya_s, yb_s, z_s, x_s = sample
    return pl.pallas_call(
        functools.partial(_outproj_kernel, tm=tm, tiles_p=tiles_p, tiles_s=tiles_s),
        grid=(tiles + 1,),
        in_specs=src(pidx) + src(sidx) + [
            const(M_DIM, D_MODEL), const(G_VDIM, D_MODEL), const(D_MODEL, D_MODEL), const(1, D_MODEL),
            const(D_MODEL, LANES), const(1, LANES), const(tm, tm),
        ],
        out_specs=[
            pl.BlockSpec((tm, D_MODEL), lambda i: (oidx(i), 0)),
            pl.BlockSpec((tm * ROW_CHUNKS, LANES), lambda i: (i, 0)),
            pl.BlockSpec((SUBLANES, tm), lambda i: (0, oidx(i))),
            pl.BlockSpec((tm, LANES), lambda i: (oidx(i), 0)),
            const(SUBLANES, LANES),
        ],
        out_shape=[
            jax.ShapeDtypeStruct((n, D_MODEL), F32),
            jax.ShapeDtypeStruct(((n + tm) * ROW_CHUNKS, LANES), F32),
            jax.ShapeDtypeStruct((SUBLANES, n), jnp.int32),
            jax.ShapeDtypeStruct((n, LANES), F32),
            jax.ShapeDtypeStruct((SUBLANES, LANES), F32),
        ],
        compiler_params=_params("arbitrary"),
        name="outproj_router",
    )(ya_p, yb_p, z_p, z_p, x_p, ya_s, yb_s, z_s, z_s, x_s, wpa, wpb, wout, gffn, wr, br, tril)


def _row_copy(src, src_row, dst, dst_row, sem):
    return pltpu.make_async_copy(
        src.at[pl.ds(pl.multiple_of(src_row * ROW_CHUNKS, ROW_CHUNKS), ROW_CHUNKS)],
        dst.at[pl.ds(pl.multiple_of(dst_row * ROW_CHUNKS, ROW_CHUNKS), ROW_CHUNKS)], sem)


def _dispatch_kernel(pos_ref, fill_ref, xn_ref, xs_ref, zeros_ref, zsem, sem, *, tm, n, n_blocks):
    i = pl.program_id(0)
    base = i * tm

    def issue(r, carry):
        for k in range(TOP_K):
            _row_copy(xn_ref, r, xs_ref, pos_ref[k * n + base + r], sem).start(priority=k % 2)
        return carry

    lax.fori_loop(0, tm, issue, 0)
    for _ in range(TOP_K):
        pltpu.make_async_copy(xn_ref, xs_ref.at[pl.ds(0, tm * ROW_CHUNKS)], sem).wait()

    @pl.when(i == 0)
    def _():
        zeros_ref[...] = jnp.zeros_like(zeros_ref)
        bits = [1 << b for b in range(MOE_ROWS.bit_length() - 1)]

        def zero_copy(off, nrows):
            return pltpu.make_async_copy(
                zeros_ref.at[pl.ds(0, nrows * ROW_CHUNKS)],
                xs_ref.at[pl.ds(pl.multiple_of(off * ROW_CHUNKS, ROW_CHUNKS), nrows * ROW_CHUNKS)], zsem)

        def expert(e, carry):
            npad = fill_ref[N_EXPERTS + e]
            for wait in (False, True):
                off = fill_ref[e]
                for bit in bits:
                    @pl.when((npad & bit) != 0)
                    def _():
                        zero_copy(off, bit).wait() if wait else zero_copy(off, bit).start()
                    off = off + (npad & bit)
            return carry

        lax.fori_loop(0, N_EXPERTS, expert, 0)

        def block(b, carry):
            zero_copy(b * MOE_ROWS, MOE_ROWS).start()
            zero_copy(b * MOE_ROWS, MOE_ROWS).wait()
            return carry

        lax.fori_loop(fill_ref[2 * N_EXPERTS], n_blocks, block, 0)


def _dispatch(pos, fill, xn, n, n_blocks, tm):
    return pl.pallas_call(
        functools.partial(_dispatch_kernel, tm=tm, n=n, n_blocks=n_blocks),
        grid_spec=pltpu.PrefetchScalarGridSpec(
            num_scalar_prefetch=2,
            grid=(n // tm,),
            in_specs=[pl.BlockSpec((tm * ROW_CHUNKS, LANES), lambda i, *_: (i, 0))],
            out_specs=pl.BlockSpec(memory_space=pl.ANY),
            scratch_shapes=[pltpu.VMEM((MOE_ROWS * ROW_CHUNKS, LANES), F32), pltpu.SemaphoreType.DMA(()),
                            pltpu.SemaphoreType.DMA(())],
        ),
        out_shape=jax.ShapeDtypeStruct((n_blocks * MOE_ROWS * ROW_CHUNKS, LANES), F32),
        compiler_params=_params("arbitrary"),
        name="dispatch",
    )(pos, fill, xn)


def _expert_kernel(plan_ref, nused_ref, xs_ref, wgu_hbm, bgu_ref, wd_hbm, bd_ref, y_ref,
                   wgu_f32, wd_f32, wgu_bf, wd_bf, wsem, *, rows):
    i = pl.program_id(0)
    active = i < nused_ref[0]
    expert, next_expert, slot = plan_ref[0, i], plan_ref[1, i], plan_ref[2, i]
    first = jnp.logical_or(i == 0, expert != plan_ref[0, jnp.maximum(i - 1, 0)])

    def weight_copies(e, s):
        return (pltpu.make_async_copy(wgu_hbm.at[e], wgu_f32.at[s], wsem.at[0, s]),
                pltpu.make_async_copy(wd_hbm.at[e], wd_f32.at[s], wsem.at[1, s]))

    @pl.when(i == 0)
    def _():
        for c in weight_copies(expert, slot):
            c.start()

    @pl.when(jnp.logical_and(active, first))
    def _():
        for c in weight_copies(expert, slot):
            c.wait()
        wgu_bf[...] = wgu_f32[slot].astype(BF16)
        wd_bf[...] = wd_f32[slot].astype(BF16)

        @pl.when(next_expert < N_EXPERTS)
        def _():
            for c in weight_copies(next_expert, 1 - slot):
                c.start()

    @pl.when(active)
    def _():
        x = jnp.concatenate([xs_ref[pl.ds(j, rows, stride=ROW_CHUNKS), :] for j in range(ROW_CHUNKS)], axis=-1)
        zz = _dot(x.astype(BF16), wgu_bf[...]) + bgu_ref[...]
        g = jnp.minimum(zz[:, :D_FF], SWIGLU_LIMIT)
        u = jnp.clip(zz[:, D_FF:], -SWIGLU_LIMIT, SWIGLU_LIMIT)
        act = g * _sigmoid(SWIGLU_ALPHA * g) * (u + 1.0)
        y = _dot(act.astype(BF16), wd_bf[...]) + bd_ref[...]
        for j in range(ROW_CHUNKS):
            y_ref[pl.ds(j, rows, stride=ROW_CHUNKS), :] = y[:, j * LANES:(j + 1) * LANES]

    @pl.when(jnp.logical_not(active))
    def _():
        y_ref[...] = jnp.zeros_like(y_ref)


def _experts(plan, n_used, xs, wgu, bgu, wd, bd, rows):
    nb = plan.shape[1]
    return pl.pallas_call(
        functools.partial(_expert_kernel, rows=rows),
        grid_spec=pltpu.PrefetchScalarGridSpec(
            num_scalar_prefetch=2,
            grid=(nb,),
            in_specs=[
                pl.BlockSpec((rows * ROW_CHUNKS, LANES), lambda i, plan, nu: (i, 0)),
                pl.BlockSpec(memory_space=pl.ANY),
                pl.BlockSpec((None, 1, 2 * D_FF), lambda i, plan, nu: (plan[0, i], 0, 0)),
                pl.BlockSpec(memory_space=pl.ANY),
                pl.BlockSpec((None, 1, D_MODEL), lambda i, plan, nu: (plan[0, i], 0, 0)),
            ],
            out_specs=pl.BlockSpec((rows * ROW_CHUNKS, LANES), lambda i, plan, nu: (i, 0)),
            scratch_shapes=[pltpu.VMEM((2, D_MODEL, 2 * D_FF), F32), pltpu.VMEM((2, D_FF, D_MODEL), F32),
                            pltpu.VMEM((D_MODEL, 2 * D_FF), BF16), pltpu.VMEM((D_FF, D_MODEL), BF16),
                            pltpu.SemaphoreType.DMA((2, 2))],
        ),
        out_shape=jax.ShapeDtypeStruct(xs.shape, F32),
        compiler_params=_params("arbitrary"),
        name="experts",
    )(plan, n_used, xs, wgu, bgu, wd, bd)


def _combine_kernel(pos_ref, ye_ref, x1_ref, wo_ref, p_ref, wproj_ref, wgate_ref, gple_ref, gfin_ref,
                    y_ref, rows_scr, sem, *, tm, nt, n_all, tok0):
    i = pl.program_id(0)

    def gather(tile, slot):
        pbase = tok0 + tile * tm
        dbase = slot * (TOP_K * tm * ROW_CHUNKS)
        sem_slot = sem.at[slot]

        def issue(g, carry):
            for u in range(GATHER_UNROLL):
                r = g * GATHER_UNROLL + u
                tok = pbase + r
                drow = pl.multiple_of(dbase + r * ROW_CHUNKS, ROW_CHUNKS)
                for k in range(TOP_K):
                    srow = pl.multiple_of(pos_ref[k * n_all + tok] * ROW_CHUNKS, ROW_CHUNKS)
                    pltpu.make_async_copy(ye_ref.at[pl.ds(srow, ROW_CHUNKS)],
                                          rows_scr.at[pl.ds(drow + k * tm * ROW_CHUNKS, ROW_CHUNKS)],
                                          sem_slot).start(priority=k % 2)
            return carry
        lax.fori_loop(0, tm // GATHER_UNROLL, issue, 0)

    @pl.when(i == 0)
    def _():
        gather(0, 0)

    @pl.when(i + 1 < nt)
    def _():
        gather(i + 1, (i + 1) % 2)

    slot = i % 2
    seg = tm * ROW_CHUNKS
    for k in range(TOP_K):
        off = pl.multiple_of((slot * TOP_K + k) * seg, seg)
        pltpu.make_async_copy(ye_ref.at[pl.ds(0, seg)], rows_scr.at[pl.ds(off, seg)], sem.at[slot]).wait()

    wo = wo_ref[...]
    moe = jnp.zeros((tm, D_MODEL), F32)
    for k in range(TOP_K):
        off = pl.multiple_of((slot * TOP_K + k) * seg, seg)
        rows = jnp.concatenate(
            [rows_scr[pl.ds(off + j, tm, stride=ROW_CHUNKS), :] for j in range(ROW_CHUNKS)], axis=-1)
        moe = moe + rows * wo[:, k:k + 1]
    x2 = x1_ref[...] + moe
    gate = _sigmoid(_dot(_rms(x2, gple_ref[...]).astype(BF16), wgate_ref[...]))
    x3 = x2 + _dot(p_ref[...].astype(BF16), wproj_ref[...]) * gate
    y_ref[...] = _rms(x3, gfin_ref[...])


def _combine(pos_flat, ye, x1, wo, p, wproj, wgate, gple, gfin, tm, tok0):
    n = p.shape[0]
    n_all = x1.shape[0]
    blk0 = tok0 // tm
    tok = lambda width: pl.BlockSpec((tm, width), lambda i, pos: (blk0 + i, 0))
    const = lambda *shape: pl.BlockSpec(shape, lambda i, pos: (0,) * len(shape))
    return pl.pallas_call(
        functools.partial(_combine_kernel, tm=tm, nt=n // tm, n_all=n_all, tok0=tok0),
        grid_spec=pltpu.PrefetchScalarGridSpec(
            num_scalar_prefetch=1,
            grid=(n // tm,),
            in_specs=[
                pl.BlockSpec(memory_space=pl.ANY),
                tok(D_MODEL), tok(LANES), pl.BlockSpec((tm, PLE_DIM), lambda i, pos: (i, 0)),
                const(PLE_DIM, D_MODEL), const(D_MODEL, D_MODEL), const(1, D_MODEL), const(1, D_MODEL),
            ],
            out_specs=pl.BlockSpec((tm, D_MODEL), lambda i, pos: (i, 0)),
            scratch_shapes=[pltpu.VMEM((2 * TOP_K * tm * ROW_CHUNKS, LANES), F32), pltpu.SemaphoreType.DMA((2,))],
        ),
        out_shape=jax.ShapeDtypeStruct((n, D_MODEL), F32),
        compiler_params=_params("arbitrary"),
        name="combine_ple",
    )(pos_flat, ye, x1, wo, p, wproj, wgate, gple, gfin)


def _tile(n, pref):
    t = pref
    while n % t:
        t //= 2
    return t


def _seg(a, i):
    return a[..., _OFF[i]:_OFF[i + 1]]


def kernel(x_prompt, x_sample, state_conv, state_mlstm_C, state_mlstm_n, state_mlstm_m, state_gla_S, p_prompt, p_sample, g_mix, w_in, b_in, conv_w, conv_b, g_mnorm, w_a2, b_a2, g_gnorm, w_pa, w_pb, w_out, g_ffn, w_router, b_router, w_gate_up, b_gate_up, w_down, b_down, g_ple, w_ple_proj, w_ple_gate, g_final):
    assert g_mix.shape[0] == 1, "single-layer kernel"
    bp, tp, _ = x_prompt.shape
    bs, ts, _ = x_sample.shape

    big = (0, 1, 2, 5, 6, 7, 8, 10, 11, 12)
    w_big = jnp.concatenate([_seg(w_in[0], i) for i in big], axis=-1).astype(BF16)
    b_big = jnp.concatenate([_seg(b_in[0], i) for i in big], axis=-1)[None, :]
    small = (3, 4, 9)
    n_small = sum(SPLITS[i] for i in small)
    w_small = jnp.pad(jnp.concatenate([_seg(w_in[0], i) for i in small], axis=-1),
                      ((0, 0), (0, LANES - n_small))).astype(BF16)
    b_small = jnp.pad(jnp.concatenate([_seg(b_in[0], i) for i in small], axis=-1), (0, LANES - n_small))[None, :]
    w2 = jnp.pad(w_a2[0], ((2 * M_HEADS, LANES - 2 * M_HEADS - G_RANK), (0, 0))).astype(BF16)
    wr = jnp.pad(w_router[0], ((0, 0), (0, LANES - N_EXPERTS))).astype(BF16)
    br = jnp.pad(b_router[0], (0, LANES - N_EXPERTS), constant_values=NEG_BIG)[None, :]
    wpa, wpb, wout = w_pa[0].astype(BF16), w_pb[0].astype(BF16), w_out[0].astype(BF16)
    wproj, wgate = w_ple_proj[0].astype(BF16), w_ple_gate[0].astype(BF16)
    row = lambda a: a.reshape(1, -1)

    def mix_path(x3d, conv0, c0, n0, m0, s0):
        batch, T, _ = x3d.shape
        x = x3d.reshape(batch * T, D_MODEL)
        z, zs = _inproj(x, row(g_mix[0]), w_big, b_big, w_small, b_small, _tile(batch * T, 1024))
        L = min(MIX_CHUNK, T)
        conv0p = jnp.pad(conv0, ((0, 0), (SUBLANES - (CONV_W - 1), 0), (0, 0)))
        m0b = jnp.broadcast_to(m0[:, :, None, None], (batch, M_HEADS, SUBLANES, LANES))
        ya, yb, tail, c1, n1, m1, st1 = _mixers(
            z, zs, conv0p, c0, n0[:, :, None, :], m0b, jnp.swapaxes(s0, -1, -2),
            conv_w[0], row(conv_b[0]), row(g_mnorm[0]), w2, row(b_a2[0]), row(g_gnorm[0]), batch, T, L)
        new_state = (tail[:, SUBLANES - (CONV_W - 1):, :], c1, n1[:, :, 0, :], m1[:, :, 0, 0],
                     jnp.swapaxes(st1, -1, -2))
        return x, z, ya, yb, new_state

    zero_state = (jnp.zeros((bp, CONV_W - 1, 2 * M_DIM), F32), jnp.zeros((bp, M_HEADS, M_HD, M_HD), F32),
                  jnp.zeros((bp, M_HEADS, M_HD), F32), jnp.zeros((bp, M_HEADS), F32),
                  jnp.zeros((bp, G_HEADS, G_DK, G_DV), F32))
    xp, zp, yap, ybp, st_p = mix_path(x_prompt, *zero_state)
    xs_, zs_, yas, ybs, st_s = mix_path(x_sample, state_conv[0], state_mlstm_C[0], state_mlstm_n[0],
                                        state_mlstm_m[0], state_gla_S[0])

    n_p, n_s = xp.shape[0], xs_.shape[0]
    n_all = n_p + n_s
    tm = _tile(n_s, 256)
    assert n_p % tm == 0
    x1, xn, eo, wo, cnt = _outproj((yap, ybp, zp, xp), (yas, ybs, zs_, xs_), wpa, wpb, wout, row(g_ffn[0]), wr, br, tm)

    counts = cnt[0, :N_EXPERTS].astype(jnp.int32)
    padded = (counts + MOE_ROWS - 1) // MOE_ROWS * MOE_ROWS
    pend = jnp.cumsum(padded)
    pstart = pend - padded
    nk = n_all * TOP_K
    n_blocks = (nk + N_EXPERTS * (MOE_ROWS - 1) + MOE_ROWS - 1) // MOE_ROWS
    starts = jnp.arange(n_blocks, dtype=jnp.int32) * MOE_ROWS
    block_e = jnp.minimum(jnp.sum((pend[None, :] <= starts[:, None]).astype(jnp.int32), axis=1), N_EXPERTS - 1)
    ids = jnp.arange(N_EXPERTS, dtype=jnp.int32)
    later = jnp.where((ids[None, :] > ids[:, None]) & (counts[None, :] > 0), ids[None, :], N_EXPERTS)
    next_e = jnp.min(later, axis=1)
    group = jnp.cumsum(jnp.concatenate([jnp.zeros((1,), jnp.int32),
                                        (block_e[1:] != block_e[:-1]).astype(jnp.int32)]))
    plan = jnp.stack([block_e, next_e[block_e], group % 2]).astype(jnp.int32)
    n_used = (pend[-1:] // MOE_ROWS).astype(jnp.int32)
    e, rank = eo[:TOP_K], eo[TOP_K:2 * TOP_K]
    onehot = e[..., None] == jnp.arange(N_EXPERTS, dtype=jnp.int32)
    pos = (jnp.sum(jnp.where(onehot, pstart, 0), axis=-1) + rank).reshape(-1)
    fill = jnp.concatenate([pstart + counts, padded - counts, n_used]).astype(jnp.int32)

    xs_sorted = _dispatch(pos, fill, xn, n_all, n_blocks, tm)
    ye = _experts(plan, n_used, xs_sorted, w_gate_up[0], b_gate_up[0][:, None, :], w_down[0],
                  b_down[0][:, None, :], MOE_ROWS)
    fin = lambda p, tok0: _combine(pos, ye, x1, wo, p.reshape(-1, PLE_DIM), wproj, wgate,
                                   row(g_ple[0]), row(g_final), tm, tok0)
    y_p = fin(p_prompt[0], 0).reshape(bp, tp, D_MODEL)
    y_s = fin(p_sample[0], n_p).reshape(bs, ts, D_MODEL)

    lead = lambda a: a[None]
    return (y_p, y_s) + tuple(lead(a) for a in st_p) + tuple(lead(a) for a in st_s)
```

```python
import functools

import numpy as np
import jax
import jax.numpy as jnp
from jax import lax
from jax.experimental import pallas as pl
from jax.experimental.pallas import tpu as pltpu

F32 = jnp.float32
BF16 = jnp.bfloat16

D_MODEL = 1024
EPS = 1e-6
M_HEADS = 4
M_DIM = D_MODEL
M_HD = M_DIM // M_HEADS
CONV_W = 4
G_HEADS = 4
G_KDIM = D_MODEL // 2
G_VDIM = D_MODEL
G_DK = G_KDIM // G_HEADS
G_DV = G_VDIM // G_HEADS
G_RANK = 16
G_TAU = 16.0
N_EXPERTS = 32
TOP_K = 4
D_FF = D_MODEL
SWIGLU_LIMIT = 7.0
SWIGLU_ALPHA = 1.702
PLE_DIM = 256
SPLITS = (M_DIM, M_DIM, M_DIM, M_HEADS, M_HEADS, M_DIM, G_KDIM, G_KDIM, G_VDIM, G_RANK, G_VDIM, D_MODEL, D_MODEL)
_OFF = np.concatenate([[0], np.cumsum(SPLITS)]).tolist()

LANES = 128
SUBLANES = 8
ROW_CHUNKS = D_MODEL // LANES
VMEM_LIMIT = 56 * 1024 * 1024
MIX_CHUNK = 256
MOE_ROWS = 256
GATHER_UNROLL = 4
NEG_BIG = -1e30


def _sigmoid(x):
    return 0.5 * jnp.tanh(0.5 * x) + 0.5


def _log_sigmoid(x):
    return jnp.minimum(x, 0.0) - jnp.log(1.0 + jnp.exp(-jnp.abs(x)))


def _rms(x, g):
    return x * lax.rsqrt(jnp.mean(x * x, axis=-1, keepdims=True) + EPS) * g


def _dot(a, b):
    return jnp.dot(a, b, preferred_element_type=F32)


def _dot_nt(a, b):
    return lax.dot_general(a, b, (((1,), (1,)), ((), ())), preferred_element_type=F32)


def _dot_tn(a, b):
    return lax.dot_general(a, b, (((0,), (0,)), ((), ())), preferred_element_type=F32)


def _params(*sem):
    return pltpu.CompilerParams(dimension_semantics=sem, vmem_limit_bytes=VMEM_LIMIT)


def _inproj_kernel(x_ref, g_ref, w_ref, b_ref, ws_ref, bs_ref, z_ref, zs_ref, h_scr):
    @pl.when(pl.program_id(1) == 0)
    def _():
        hb = _rms(x_ref[...], g_ref[...]).astype(BF16)
        h_scr[...] = hb
        zs_ref[...] = _dot(hb, ws_ref[...]) + bs_ref[...]

    z_ref[...] = (_dot(h_scr[...], w_ref[...]) + b_ref[...]).astype(BF16)


def _inproj(x, g, w_big, b_big, w_small, b_small, tm, tn=1024):
    n = x.shape[0]
    cols = w_big.shape[1]
    return pl.pallas_call(
        _inproj_kernel,
        grid=(n // tm, cols // tn),
        in_specs=[
            pl.BlockSpec((tm, D_MODEL), lambda i, j: (i, 0)),
            pl.BlockSpec((1, D_MODEL), lambda i, j: (0, 0)),
            pl.BlockSpec((D_MODEL, tn), lambda i, j: (0, j)),
            pl.BlockSpec((1, tn), lambda i, j: (0, j)),
            pl.BlockSpec((D_MODEL, LANES), lambda i, j: (0, 0)),
            pl.BlockSpec((1, LANES), lambda i, j: (0, 0)),
        ],
        out_specs=[
            pl.BlockSpec((tm, tn), lambda i, j: (i, j)),
            pl.BlockSpec((tm, LANES), lambda i, j: (i, 0)),
        ],
        out_shape=[jax.ShapeDtypeStruct((n, cols), BF16), jax.ShapeDtypeStruct((n, LANES), F32)],
        scratch_shapes=[pltpu.VMEM((tm, D_MODEL), BF16)],
        compiler_params=_params("parallel", "arbitrary"),
        name="inproj",
    )(x, g, w_big, b_big, w_small, b_small)


def _level_matrix(L):
    t = np.arange(L)[:, None]
    s = np.arange(L)[None, :]
    x = np.maximum(t ^ s, 1)
    lv = np.floor(np.log2(x)).astype(np.int32)
    lv = np.where(t == s, -1, lv)
    return np.where(s > t, -2, lv).astype(np.int32)


def _bcast_row(x, period, r):
    L, W = x.shape
    x3 = x.reshape(L // period, period, W)
    return jnp.broadcast_to(x3[:, r:r + 1, :], x3.shape).reshape(L, W)


def _decay_products(ela, L):
    W = ela.shape[1]
    row = lax.broadcasted_iota(jnp.int32, (L, W), 0)
    fwd = [ela]
    rex = [jnp.ones_like(ela)]
    b = 1
    while b < L:
        per = 2 * b
        res = row & (per - 1)
        f, r = fwd[-1], rex[-1]
        if per < SUBLANES:
            mulf = jnp.ones_like(ela)
            mulr = jnp.ones_like(ela)
            for j in range(b, per):
                mulf = jnp.where(res == j, pltpu.roll(f, j - (b - 1), 0), mulf)
            for j in range(0, b):
                mulr = jnp.where(res == j, pltpu.roll(f, L - (per - 1 - j), 0), mulr)
        else:
            mulf = jnp.where(res >= b, _bcast_row(f, per, b - 1), 1.0)
            mulr = jnp.where(res < b, _bcast_row(f, per, per - 1), 1.0)
        fwd.append(f * mulf)
        rex.append(r * mulr)
        b = per
    return fwd, rex


def _mixer_kernel(zqk_ref, zv_ref, zo_ref, zgq_ref, zgk_ref, zgv_ref, zgg_ref, zs_ref,
                  conv0_ref, c0_ref, n0_ref, m0_ref, s0_ref,
                  cw_ref, cb_ref, gm_ref, w2_ref, ba2_ref, gg_ref, lv_ref, shift_ref,
                  ya_ref, yb_ref, tail_ref, c_ref, n_ref, m_ref, st_ref, *, L):
    @pl.when(pl.program_id(1) == 0)
    def _():
        tail_ref[...] = conv0_ref[...]
        c_ref[...] = c0_ref[...]
        n_ref[...] = n0_ref[...]
        m_ref[...] = m0_ref[...]
        st_ref[...] = s0_ref[...]

    lv = lv_ref[...]
    causal = lv >= -1

    u_b = zqk_ref[...]
    u = u_b.astype(F32)
    tail = tail_ref[0]
    row8 = lax.broadcasted_iota(jnp.int32, (SUBLANES, 2 * M_DIM), 0)
    acc = cb_ref[...] + cw_ref[CONV_W - 1:CONV_W, :] * u
    for w in range(CONV_W - 1):
        delay = CONV_W - 1 - w
        shifted = _dot(shift_ref[w], u_b)
        head = jnp.where(row8 < delay, pltpu.roll(tail, delay, 0), 0.0)
        shifted = jnp.concatenate([shifted[0:SUBLANES] + head, shifted[SUBLANES:]], axis=0)
        acc = acc + cw_ref[w:w + 1, :] * shifted
    tail_ref[0] = u[L - SUBLANES:L, :]
    qk = acc * _sigmoid(acc)

    zs = zs_ref[...]
    row = lax.broadcasted_iota(jnp.int32, (L, LANES), 0)
    col = lax.broadcasted_iota(jnp.int32, (L, LANES), 1)
    bcs = _log_sigmoid(zs)
    sh = 1
    while sh < L:
        bcs = bcs + jnp.where(row >= sh, pltpu.roll(bcs, sh, 0), 0.0)
        sh *= 2
    gates = jnp.where(col < M_HEADS, zs, bcs)
    if L < LANES:
        gates = jnp.concatenate([gates, jnp.zeros((LANES - L, LANES), F32)], axis=0)
    gt = gates.T[:, :L]

    for h in range(M_HEADS):
        hs = slice(h * M_HD, (h + 1) * M_HD)
        ig_col = zs[:, h:h + 1]
        b_col = bcs[:, M_HEADS + h:M_HEADS + h + 1]
        ig_row = gt[h:h + 1, :]
        b_row = gt[M_HEADS + h:M_HEADS + h + 1, :]
        m_prev = m_ref[0, h][0:1, 0:1]
        logd = jnp.where(causal, b_col - b_row + ig_row, -jnp.inf)
        inter = b_col + m_prev
        m_t = jnp.maximum(inter, jnp.max(logd, axis=-1, keepdims=True))
        q_f = qk[:, hs]
        k_f = qk[:, M_DIM + h * M_HD:M_DIM + (h + 1) * M_HD] * (M_HD ** -0.5)
        q_b = q_f.astype(BF16)
        v_b = zv_ref[:, hs]
        s = _dot_nt(q_b, k_f.astype(BF16)) * jnp.exp(logd - m_t)
        w_int = jnp.exp(inter - m_t)
        c_old = c_ref[0, h]
        n_old = n_ref[0, h]
        num = _dot(s.astype(BF16), v_b) + w_int * _dot(q_b, c_old.astype(BF16))
        den = jnp.sum(s, axis=-1, keepdims=True) + w_int * jnp.sum(q_f * n_old, axis=-1, keepdims=True)
        hh = num * (1.0 / jnp.maximum(jnp.abs(den), jnp.exp(-m_t)))
        m_new = m_t[L - 1:L, :]
        b_last = b_col[L - 1:L, :]
        w_k = jnp.exp(b_last - b_col + ig_col - m_new)
        dec = jnp.exp(b_last + m_prev - m_new)
        k_w = k_f * w_k
        c_ref[0, h] = dec * c_old + _dot_tn(k_w.astype(BF16), v_b)
        n_ref[0, h] = dec * n_old + jnp.sum(k_w, axis=0, keepdims=True)
        m_ref[0, h] = jnp.broadcast_to(m_new, (SUBLANES, LANES))
        hn = _rms(hh, gm_ref[:, hs])
        ya_ref[:, hs] = (hn * _sigmoid(zo_ref[:, hs].astype(F32))).astype(BF16)

    la = _log_sigmoid(_dot(zs.astype(BF16), w2_ref[...]) + ba2_ref[...]) * (1.0 / G_TAU)
    fwd, rex = _decay_products(jnp.exp(la), L)
    nlev = len(fwd) - 1
    qg = zgq_ref[...].astype(F32) * (G_DK ** -0.5)
    kg = zgk_ref[...].astype(F32)
    q_d = qg.astype(BF16)
    k_d = zgk_ref[...]
    qs = [(qg * fwd[j]).astype(BF16) for j in range(nlev)]
    ks = [k_d] + [(kg * rex[j]).astype(BF16) for j in range(1, nlev)]
    q_in = (qg * fwd[nlev]).astype(BF16)
    k_out = (kg * rex[nlev]).astype(BF16)
    dec_all = fwd[nlev][L - 1:L, :]
    for h in range(G_HEADS):
        ks_ = slice(h * G_DK, (h + 1) * G_DK)
        vs_ = slice(h * G_DV, (h + 1) * G_DV)
        att = jnp.where(lv == -1, _dot_nt(q_d[:, ks_], k_d[:, ks_]), 0.0)
        for j in range(nlev):
            att = jnp.where(lv == j, _dot_nt(qs[j][:, ks_], ks[j][:, ks_]), att)
        v_b = zgv_ref[:, vs_]
        st_old = st_ref[0, h]
        o = _dot(att.astype(BF16), v_b) + _dot_nt(q_in[:, ks_], st_old.astype(BF16))
        st_ref[0, h] = st_old * dec_all[:, ks_] + _dot_tn(v_b, k_out[:, ks_])
        on = _rms(o, gg_ref[:, vs_])
        g = zgg_ref[:, vs_].astype(F32)
        yb_ref[:, vs_] = (on * (g * _sigmoid(g))).astype(BF16)


def _mixers(z, zs, conv0, c0, n0, m0, st0, cw, cb, gm, w2, ba2, gg, batch, T, L):
    nc = T // L
    lv = jnp.asarray(_level_matrix(L))
    shift = jnp.asarray(np.stack([np.eye(L, k=-(CONV_W - 1 - w)) for w in range(CONV_W - 1)]), BF16)
    row_blk = lambda width, cidx: pl.BlockSpec((L, width), lambda b, c: (b * nc + c, cidx))
    state = lambda *shape: pl.BlockSpec((1,) + shape, lambda b, c: (b,) + (0,) * len(shape))
    const = lambda *shape: pl.BlockSpec(shape, lambda b, c: (0,) * len(shape))
    n_tok = batch * T
    return pl.pallas_call(
        functools.partial(_mixer_kernel, L=L),
        grid=(batch, nc),
        in_specs=[
            row_blk(2 * M_DIM, 0),
            row_blk(M_DIM, 2),
            row_blk(M_DIM, 3),
            row_blk(G_KDIM, 8),
            row_blk(G_KDIM, 9),
            row_blk(G_VDIM, 5),
            row_blk(G_VDIM, 6),
            pl.BlockSpec((L, LANES), lambda b, c: (b * nc + c, 0)),
            state(SUBLANES, 2 * M_DIM),
            state(M_HEADS, M_HD, M_HD),
            state(M_HEADS, 1, M_HD),
            state(M_HEADS, SUBLANES, LANES),
            state(G_HEADS, G_DV, G_DK),
            const(CONV_W, 2 * M_DIM),
            const(1, 2 * M_DIM),
            const(1, M_DIM),
            const(LANES, G_KDIM),
            const(1, G_KDIM),
            const(1, G_VDIM),
            const(L, L),
            const(CONV_W - 1, L, L),
        ],
        out_specs=[
            pl.BlockSpec((L, M_DIM), lambda b, c: (b * nc + c, 0)),
            pl.BlockSpec((L, G_VDIM), lambda b, c: (b * nc + c, 0)),
            state(SUBLANES, 2 * M_DIM),
            state(M_HEADS, M_HD, M_HD),
            state(M_HEADS, 1, M_HD),
            state(M_HEADS, SUBLANES, LANES),
            state(G_HEADS, G_DV, G_DK),
        ],
        out_shape=[
            jax.ShapeDtypeStruct((n_tok, M_DIM), BF16),
            jax.ShapeDtypeStruct((n_tok, G_VDIM), BF16),
            jax.ShapeDtypeStruct((batch, SUBLANES, 2 * M_DIM), F32),
            jax.ShapeDtypeStruct((batch, M_HEADS, M_HD, M_HD), F32),
            jax.ShapeDtypeStruct((batch, M_HEADS, 1, M_HD), F32),
            jax.ShapeDtypeStruct((batch, M_HEADS, SUBLANES, LANES), F32),
            jax.ShapeDtypeStruct((batch, G_HEADS, G_DV, G_DK), F32),
        ],
        compiler_params=_params("parallel", "arbitrary"),
        name="mixers",
    )(z, z, z, z, z, z, z, zs, conv0, c0, n0, m0, st0, cw, cb, gm, w2, ba2, gg, lv, shift)


def _outproj_kernel(ya_p, yb_p, ua_p, ub_p, x_p, ya_s, yb_s, ua_s, ub_s, x_s,
                    wpa_ref, wpb_ref, wout_ref, gffn_ref, wr_ref, br_ref, tril_ref,
                    x1_ref, xn_ref, eo_ref, wo_ref, cnt_ref, *, tm, tiles_p, tiles_s):
    i = pl.program_id(0)

    @pl.when(i == 0)
    def _():
        cnt_ref[...] = jnp.zeros_like(cnt_ref)

    def body(ya_ref, yb_ref, ua_ref, ub_ref, x_ref):
        a = _dot(ya_ref[...], wpa_ref[...])
        b = _dot(yb_ref[...], wpb_ref[...])
        merged = _sigmoid(ua_ref[...].astype(F32)) * a + _sigmoid(ub_ref[...].astype(F32)) * b
        x1 = x_ref[...] + _dot(merged.astype(BF16), wout_ref[...])
        x1_ref[...] = x1
        xn = _rms(x1, gffn_ref[...])
        for j in range(ROW_CHUNKS):
            xn_ref[pl.ds(j, tm, stride=ROW_CHUNKS), :] = xn[:, j * LANES:(j + 1) * LANES]

        logits = _dot(xn.astype(BF16), wr_ref[...]) + br_ref[...]
        lane = lax.broadcasted_iota(jnp.int32, (tm, LANES), 1).astype(F32)
        vals, idxs = [], []
        l = logits
        for _ in range(TOP_K):
            mx = jnp.max(l, axis=-1, keepdims=True)
            ix = jnp.min(jnp.where(l == mx, lane, float(LANES)), axis=-1, keepdims=True)
            vals.append(mx)
            idxs.append(ix)
            l = jnp.where(lane == ix, -jnp.inf, l)
        ev = [jnp.exp(v - vals[0]) for v in vals]
        den = ev[0] + ev[1] + ev[2] + ev[3]

        cnt = jnp.zeros((tm, LANES), F32)
        for ix in idxs:
            cnt = cnt + jnp.where(lane == ix, 1.0, 0.0)
        pre = _dot(tril_ref[...], cnt.astype(BF16)) + cnt_ref[0:1, :]
        eo = jnp.zeros((tm, LANES), F32)
        wo = jnp.zeros((tm, LANES), F32)
        for k in range(TOP_K):
            rank = jnp.sum(jnp.where(lane == idxs[k], pre, 0.0), axis=-1, keepdims=True)
            eo = jnp.where(lane == float(k), idxs[k], eo)
            eo = jnp.where(lane == float(TOP_K + k), rank, eo)
            wo = jnp.where(lane == float(k), ev[k] / den, wo)
        eo_ref[...] = eo.T[0:SUBLANES, :].astype(jnp.int32)
        wo_ref[...] = wo
        cnt_ref[...] = cnt_ref[...] + jnp.sum(cnt, axis=0, keepdims=True)

    @pl.when(i < tiles_p)
    def _():
        body(ya_p, yb_p, ua_p, ub_p, x_p)

    @pl.when(jnp.logical_and(i >= tiles_p, i < tiles_p + tiles_s))
    def _():
        body(ya_s, yb_s, ua_s, ub_s, x_s)

    @pl.when(i == tiles_p + tiles_s)
    def _():
        xn_ref[...] = jnp.zeros_like(xn_ref)


def _outproj(prompt, sample, wpa, wpb, wout, gffn, wr, br, tm):
    n_p, n_s = prompt[3].shape[0], sample[3].shape[0]
    tiles_p, tiles_s = n_p // tm, n_s // tm
    tiles = tiles_p + tiles_s
    n = n_p + n_s
    tril = jnp.asarray(np.tril(np.ones((tm, tm), np.float32), -1), BF16)
    pidx = lambda i: jnp.minimum(i, tiles_p - 1)
    sidx = lambda i: jnp.clip(i - tiles_p, 0, tiles_s - 1)
    oidx = lambda i: jnp.minimum(i, tiles - 1)
    src = lambda idx: [pl.BlockSpec((tm, M_DIM), lambda i: (idx(i), 0)),
                       pl.BlockSpec((tm, G_VDIM), lambda i: (idx(i), 0)),
                       pl.BlockSpec((tm, D_MODEL), lambda i: (idx(i), 7)),
                       pl.BlockSpec((tm, D_MODEL), lambda i: (idx(i), 8)),
                       pl.BlockSpec((tm, D_MODEL), lambda i: (idx(i), 0))]
    const = lambda *shape: pl.BlockSpec(shape, lambda i: (0,) * len(shape))
    ya_p, yb_p, z_p, x_p = prompt
    ya_s, yb_s, z_s, x_s = sample
    return pl.pallas_call(
        functools.partial(_outproj_kernel, tm=tm, tiles_p=tiles_p, tiles_s=tiles_s),
        grid=(tiles + 1,),
        in_specs=src(pidx) + src(sidx) + [
            const(M_DIM, D_MODEL), const(G_VDIM, D_MODEL), const(D_MODEL, D_MODEL), const(1, D_MODEL),
            const(D_MODEL, LANES), const(1, LANES), const(tm, tm),
        ],
        out_specs=[
            pl.BlockSpec((tm, D_MODEL), lambda i: (oidx(i), 0)),
            pl.BlockSpec((tm * ROW_CHUNKS, LANES), lambda i: (i, 0)),
            pl.BlockSpec((SUBLANES, tm), lambda i: (0, oidx(i))),
            pl.BlockSpec((tm, LANES), lambda i: (oidx(i), 0)),
            const(SUBLANES, LANES),
        ],
        out_shape=[
            jax.ShapeDtypeStruct((n, D_MODEL), F32),
            jax.ShapeDtypeStruct(((n + tm) * ROW_CHUNKS, LANES), F32),
            jax.ShapeDtypeStruct((SUBLANES, n), jnp.int32),
            jax.ShapeDtypeStruct((n, LANES), F32),
            jax.ShapeDtypeStruct((SUBLANES, LANES), F32),
        ],
        compiler_params=_params("arbitrary"),
        name="outproj_router",
    )(ya_p, yb_p, z_p, z_p, x_p, ya_s, yb_s, z_s, z_s, x_s, wpa, wpb, wout, gffn, wr, br, tril)


def _row_copy(src, src_row, dst, dst_row, sem):
    return pltpu.make_async_copy(
        src.at[pl.ds(pl.multiple_of(src_row * ROW_CHUNKS, ROW_CHUNKS), ROW_CHUNKS)],
        dst.at[pl.ds(pl.multiple_of(dst_row * ROW_CHUNKS, ROW_CHUNKS), ROW_CHUNKS)], sem)


def _dispatch_kernel(pos_ref, fill_ref, xn_ref, xs_ref, zeros_ref, zsem, sem, *, tm, n, n_blocks):
    i = pl.program_id(0)
    base = i * tm

    def issue(r, carry):
        for k in range(TOP_K):
            _row_copy(xn_ref, r, xs_ref, pos_ref[k * n + base + r], sem).start(priority=k % 2)
        return carry

    lax.fori_loop(0, tm, issue, 0)
    for _ in range(TOP_K):
        pltpu.make_async_copy(xn_ref, xs_ref.at[pl.ds(0, tm * ROW_CHUNKS)], sem).wait()

    @pl.when(i == 0)
    def _():
        zeros_ref[...] = jnp.zeros_like(zeros_ref)
        bits = [1 << b for b in range(MOE_ROWS.bit_length() - 1)]

        def zero_copy(off, nrows):
            return pltpu.make_async_copy(
                zeros_ref.at[pl.ds(0, nrows * ROW_CHUNKS)],
                xs_ref.at[pl.ds(pl.multiple_of(off * ROW_CHUNKS, ROW_CHUNKS), nrows * ROW_CHUNKS)], zsem)

        def expert(e, carry):
            npad = fill_ref[N_EXPERTS + e]
            for wait in (False, True):
                off = fill_ref[e]
                for bit in bits:
                    @pl.when((npad & bit) != 0)
                    def _():
                        zero_copy(off, bit).wait() if wait else zero_copy(off, bit).start()
                    off = off + (npad & bit)
            return carry

        lax.fori_loop(0, N_EXPERTS, expert, 0)

        def block(b, carry):
            zero_copy(b * MOE_ROWS, MOE_ROWS).start()
            zero_copy(b * MOE_ROWS, MOE_ROWS).wait()
            return carry

        lax.fori_loop(fill_ref[2 * N_EXPERTS], n_blocks, block, 0)


def _dispatch(pos, fill, xn, n, n_blocks, tm):
    return pl.pallas_call(
        functools.partial(_dispatch_kernel, tm=tm, n=n, n_blocks=n_blocks),
        grid_spec=pltpu.PrefetchScalarGridSpec(
            num_scalar_prefetch=2,
            grid=(n // tm,),
            in_specs=[pl.BlockSpec((tm * ROW_CHUNKS, LANES), lambda i, *_: (i, 0))],
            out_specs=pl.BlockSpec(memory_space=pl.ANY),
            scratch_shapes=[pltpu.VMEM((MOE_ROWS * ROW_CHUNKS, LANES), F32), pltpu.SemaphoreType.DMA(()),
                            pltpu.SemaphoreType.DMA(())],
        ),
        out_shape=jax.ShapeDtypeStruct((n_blocks * MOE_ROWS * ROW_CHUNKS, LANES), F32),
        compiler_params=_params("arbitrary"),
        name="dispatch",
    )(pos, fill, xn)


def _expert_kernel(plan_ref, nused_ref, xs_ref, wgu_hbm, bgu_ref, wd_hbm, bd_ref, y_ref,
                   wgu_f32, wd_f32, wgu_bf, wd_bf, wsem, *, rows):
    i = pl.program_id(0)
    active = i < nused_ref[0]
    expert, next_expert, slot = plan_ref[0, i], plan_ref[1, i], plan_ref[2, i]
    first = jnp.logical_or(i == 0, expert != plan_ref[0, jnp.maximum(i - 1, 0)])

    def weight_copies(e, s):
        return (pltpu.make_async_copy(wgu_hbm.at[e], wgu_f32.at[s], wsem.at[0, s]),
                pltpu.make_async_copy(wd_hbm.at[e], wd_f32.at[s], wsem.at[1, s]))

    @pl.when(i == 0)
    def _():
        for c in weight_copies(expert, slot):
            c.start()

    @pl.when(jnp.logical_and(active, first))
    def _():
        for c in weight_copies(expert, slot):
            c.wait()
        wgu_bf[...] = wgu_f32[slot].astype(BF16)
        wd_bf[...] = wd_f32[slot].astype(BF16)

        @pl.when(next_expert < N_EXPERTS)
        def _():
            for c in weight_copies(next_expert, 1 - slot):
                c.start()

    @pl.when(active)
    def _():
        x = jnp.concatenate([xs_ref[pl.ds(j, rows, stride=ROW_CHUNKS), :] for j in range(ROW_CHUNKS)], axis=-1)
        zz = _dot(x.astype(BF16), wgu_bf[...]) + bgu_ref[...]
        g = jnp.minimum(zz[:, :D_FF], SWIGLU_LIMIT)
        u = jnp.clip(zz[:, D_FF:], -SWIGLU_LIMIT, SWIGLU_LIMIT)
        act = g * _sigmoid(SWIGLU_ALPHA * g) * (u + 1.0)
        y = _dot(act.astype(BF16), wd_bf[...]) + bd_ref[...]
        for j in range(ROW_CHUNKS):
            y_ref[pl.ds(j, rows, stride=ROW_CHUNKS), :] = y[:, j * LANES:(j + 1) * LANES]

    @pl.when(jnp.logical_not(active))
    def _():
        y_ref[...] = jnp.zeros_like(y_ref)


def _experts(plan, n_used, xs, wgu, bgu, wd, bd, rows):
    nb = plan.shape[1]
    return pl.pallas_call(
        functools.partial(_expert_kernel, rows=rows),
        grid_spec=pltpu.PrefetchScalarGridSpec(
            num_scalar_prefetch=2,
            grid=(nb,),
            in_specs=[
                pl.BlockSpec((rows * ROW_CHUNKS, LANES), lambda i, plan, nu: (i, 0)),
                pl.BlockSpec(memory_space=pl.ANY),
                pl.BlockSpec((None, 1, 2 * D_FF), lambda i, plan, nu: (plan[0, i], 0, 0)),
                pl.BlockSpec(memory_space=pl.ANY),
                pl.BlockSpec((None, 1, D_MODEL), lambda i, plan, nu: (plan[0, i], 0, 0)),
            ],
            out_specs=pl.BlockSpec((rows * ROW_CHUNKS, LANES), lambda i, plan, nu: (i, 0)),
            scratch_shapes=[pltpu.VMEM((2, D_MODEL, 2 * D_FF), F32), pltpu.VMEM((2, D_FF, D_MODEL), F32),
                            pltpu.VMEM((D_MODEL, 2 * D_FF), BF16), pltpu.VMEM((D_FF, D_MODEL), BF16),
                            pltpu.SemaphoreType.DMA((2, 2))],
        ),
        out_shape=jax.ShapeDtypeStruct(xs.shape, F32),
        compiler_params=_params("arbitrary"),
        name="experts",
    )(plan, n_used, xs, wgu, bgu, wd, bd)


def _combine_kernel(pos_ref, ye_ref, x1_ref, wo_ref, p_ref, wproj_ref, wgate_ref, gple_ref, gfin_ref,
                    y_ref, rows_scr, sem, *, tm, nt, n_all, tok0):
    i = pl.program_id(0)

    def gather(tile, slot):
        pbase = tok0 + tile * tm
        dbase = slot * (TOP_K * tm * ROW_CHUNKS)
        sem_slot = sem.at[slot]

        def issue(g, carry):
            for u in range(GATHER_UNROLL):
                r = g * GATHER_UNROLL + u
                tok = pbase + r
                drow = pl.multiple_of(dbase + r * ROW_CHUNKS, ROW_CHUNKS)
                for k in range(TOP_K):
                    srow = pl.multiple_of(pos_ref[k * n_all + tok] * ROW_CHUNKS, ROW_CHUNKS)
                    pltpu.make_async_copy(ye_ref.at[pl.ds(srow, ROW_CHUNKS)],
                                          rows_scr.at[pl.ds(drow + k * tm * ROW_CHUNKS, ROW_CHUNKS)],
                                          sem_slot).start(priority=k % 2)
            return carry
        lax.fori_loop(0, tm // GATHER_UNROLL, issue, 0)

    @pl.when(i == 0)
    def _():
        gather(0, 0)

    @pl.when(i + 1 < nt)
    def _():
        gather(i + 1, (i + 1) % 2)

    slot = i % 2
    seg = tm * ROW_CHUNKS
    for k in range(TOP_K):
        off = pl.multiple_of((slot * TOP_K + k) * seg, seg)
        pltpu.make_async_copy(ye_ref.at[pl.ds(0, seg)], rows_scr.at[pl.ds(off, seg)], sem.at[slot]).wait()

    wo = wo_ref[...]
    moe = jnp.zeros((tm, D_MODEL), F32)
    for k in range(TOP_K):
        off = pl.multiple_of((slot * TOP_K + k) * seg, seg)
        rows = jnp.concatenate(
            [rows_scr[pl.ds(off + j, tm, stride=ROW_CHUNKS), :] for j in range(ROW_CHUNKS)], axis=-1)
        moe = moe + rows * wo[:, k:k + 1]
    x2 = x1_ref[...] + moe
    gate = _sigmoid(_dot(_rms(x2, gple_ref[...]).astype(BF16), wgate_ref[...]))
    x3 = x2 + _dot(p_ref[...].astype(BF16), wproj_ref[...]) * gate
    y_ref[...] = _rms(x3, gfin_ref[...])


def _combine(pos_flat, ye, x1, wo, p, wproj, wgate, gple, gfin, tm, tok0):
    n = p.shape[0]
    n_all = x1.shape[0]
    blk0 = tok0 // tm
    tok = lambda width: pl.BlockSpec((tm, width), lambda i, pos: (blk0 + i, 0))
    const = lambda *shape: pl.BlockSpec(shape, lambda i, pos: (0,) * len(shape))
    return pl.pallas_call(
        functools.partial(_combine_kernel, tm=tm, nt=n // tm, n_all=n_all, tok0=tok0),
        grid_spec=pltpu.PrefetchScalarGridSpec(
            num_scalar_prefetch=1,
            grid=(n // tm,),
            in_specs=[
                pl.BlockSpec(memory_space=pl.ANY),
                tok(D_MODEL), tok(LANES), pl.BlockSpec((tm, PLE_DIM), lambda i, pos: (i, 0)),
                const(PLE_DIM, D_MODEL), const(D_MODEL, D_MODEL), const(1, D_MODEL), const(1, D_MODEL),
            ],
            out_specs=pl.BlockSpec((tm, D_MODEL), lambda i, pos: (i, 0)),
            scratch_shapes=[pltpu.VMEM((2 * TOP_K * tm * ROW_CHUNKS, LANES), F32), pltpu.SemaphoreType.DMA((2,))],
        ),
        out_shape=jax.ShapeDtypeStruct((n, D_MODEL), F32),
        compiler_params=_params("arbitrary"),
        name="combine_ple",
    )(pos_flat, ye, x1, wo, p, wproj, wgate, gple, gfin)


def _tile(n, pref):
    t = pref
    while n % t:
        t //= 2
    return t


def _seg(a, i):
    return a[..., _OFF[i]:_OFF[i + 1]]


def kernel(x_prompt, x_sample, state_conv, state_mlstm_C, state_mlstm_n, state_mlstm_m, state_gla_S, p_prompt, p_sample, g_mix, w_in, b_in, conv_w, conv_b, g_mnorm, w_a2, b_a2, g_gnorm, w_pa, w_pb, w_out, g_ffn, w_router, b_router, w_gate_up, b_gate_up, w_down, b_down, g_ple, w_ple_proj, w_ple_gate, g_final):
    assert g_mix.shape[0] == 1, "single-layer kernel"
    bp, tp, _ = x_prompt.shape
    bs, ts, _ = x_sample.shape

    big = (0, 1, 2, 5, 6, 7, 8, 10, 11, 12)
    w_big = jnp.concatenate([_seg(w_in[0], i) for i in big], axis=-1).astype(BF16)
    b_big = jnp.concatenate([_seg(b_in[0], i) for i in big], axis=-1)[None, :]
    small = (3, 4, 9)
    n_small = sum(SPLITS[i] for i in small)
    w_small = jnp.pad(jnp.concatenate([_seg(w_in[0], i) for i in small], axis=-1),
                      ((0, 0), (0, LANES - n_small))).astype(BF16)
    b_small = jnp.pad(jnp.concatenate([_seg(b_in[0], i) for i in small], axis=-1), (0, LANES - n_small))[None, :]
    w2 = jnp.pad(w_a2[0], ((2 * M_HEADS, LANES - 2 * M_HEADS - G_RANK), (0, 0))).astype(BF16)
    wr = jnp.pad(w_router[0], ((0, 0), (0, LANES - N_EXPERTS))).astype(BF16)
    br = jnp.pad(b_router[0], (0, LANES - N_EXPERTS), constant_values=NEG_BIG)[None, :]
    wpa, wpb, wout = w_pa[0].astype(BF16), w_pb[0].astype(BF16), w_out[0].astype(BF16)
    wproj, wgate = w_ple_proj[0].astype(BF16), w_ple_gate[0].astype(BF16)
    row = lambda a: a.reshape(1, -1)

    def mix_path(x3d, conv0, c0, n0, m0, s0):
        batch, T, _ = x3d.shape
        x = x3d.reshape(batch * T, D_MODEL)
        z, zs = _inproj(x, row(g_mix[0]), w_big, b_big, w_small, b_small, _tile(batch * T, 1024))
        L = min(MIX_CHUNK, T)
        conv0p = jnp.pad(conv0, ((0, 0), (SUBLANES - (CONV_W - 1), 0), (0, 0)))
        m0b = jnp.broadcast_to(m0[:, :, None, None], (batch, M_HEADS, SUBLANES, LANES))
        ya, yb, tail, c1, n1, m1, st1 = _mixers(
            z, zs, conv0p, c0, n0[:, :, None, :], m0b, jnp.swapaxes(s0, -1, -2),
            conv_w[0], row(conv_b[0]), row(g_mnorm[0]), w2, row(b_a2[0]), row(g_gnorm[0]), batch, T, L)
        new_state = (tail[:, SUBLANES - (CONV_W - 1):, :], c1, n1[:, :, 0, :], m1[:, :, 0, 0],
                     jnp.swapaxes(st1, -1, -2))
        return x, z, ya, yb, new_state

    zero_state = (jnp.zeros((bp, CONV_W - 1, 2 * M_DIM), F32), jnp.zeros((bp, M_HEADS, M_HD, M_HD), F32),
                  jnp.zeros((bp, M_HEADS, M_HD), F32), jnp.zeros((bp, M_HEADS), F32),
                  jnp.zeros((bp, G_HEADS, G_DK, G_DV), F32))
    xp, zp, yap, ybp, st_p = mix_path(x_prompt, *zero_state)
    xs_, zs_, yas, ybs, st_s = mix_path(x_sample, state_conv[0], state_mlstm_C[0], state_mlstm_n[0],
                                        state_mlstm_m[0], state_gla_S[0])

    n_p, n_s = xp.shape[0], xs_.shape[0]
    n_all = n_p + n_s
    tm = _tile(n_s, 256)
    assert n_p % tm == 0
    x1, xn, eo, wo, cnt = _outproj((yap, ybp, zp, xp), (yas, ybs, zs_, xs_), wpa, wpb, wout, row(g_ffn[0]), wr, br, tm)

    counts = cnt[0, :N_EXPERTS].astype(jnp.int32)
    padded = (counts + MOE_ROWS - 1) // MOE_ROWS * MOE_ROWS
    pend = jnp.cumsum(padded)
    pstart = pend - padded
    nk = n_all * TOP_K
    n_blocks = (nk + N_EXPERTS * (MOE_ROWS - 1) + MOE_ROWS - 1) // MOE_ROWS
    starts = jnp.arange(n_blocks, dtype=jnp.int32) * MOE_ROWS
    block_e = jnp.minimum(jnp.sum((pend[None, :] <= starts[:, None]).astype(jnp.int32), axis=1), N_EXPERTS - 1)
    ids = jnp.arange(N_EXPERTS, dtype=jnp.int32)
    later = jnp.where((ids[None, :] > ids[:, None]) & (counts[None, :] > 0), ids[None, :], N_EXPERTS)
    next_e = jnp.min(later, axis=1)
    group = jnp.cumsum(jnp.concatenate([jnp.zeros((1,), jnp.int32),
                                        (block_e[1:] != block_e[:-1]).astype(jnp.int32)]))
    plan = jnp.stack([block_e, next_e[block_e], group % 2]).astype(jnp.int32)
    n_used = (pend[-1:] // MOE_ROWS).astype(jnp.int32)
    e, rank = eo[:TOP_K], eo[TOP_K:2 * TOP_K]
    onehot = e[..., None] == jnp.arange(N_EXPERTS, dtype=jnp.int32)
    pos = (jnp.sum(jnp.where(onehot, pstart, 0), axis=-1) + rank).reshape(-1)
    fill = jnp.concatenate([pstart + counts, padded - counts, n_used]).astype(jnp.int32)

    xs_sorted = _dispatch(pos, fill, xn, n_all, n_blocks, tm)
    ye = _experts(plan, n_used, xs_sorted, w_gate_up[0], b_gate_up[0][:, None, :], w_down[0],
                  b_down[0][:, None, :], MOE_ROWS)
    fin = lambda p, tok0: _combine(pos, ye, x1, wo, p.reshape(-1, PLE_DIM), wproj, wgate,
                                   row(g_ple[0]), row(g_final), tm, tok0)
    y_p = fin(p_prompt[0], 0).reshape(bp, tp, D_MODEL)
    y_s = fin(p_sample[0], n_p).reshape(bs, ts, D_MODEL)

    lead = lambda a: a[None]
    return (y_p, y_s) + tuple(lead(a) for a in st_p) + tuple(lead(a) for a in st_s)
```

```python
import functools

import numpy as np
import jax
import jax.numpy as jnp
from jax import lax
from jax.experimental import pallas as pl
from jax.experimental.pallas import tpu as pltpu

F32 = jnp.float32
BF16 = jnp.bfloat16

D_MODEL = 1024
EPS = 1e-6
M_HEADS = 4
M_DIM = D_MODEL
M_HD = M_DIM // M_HEADS
CONV_W = 4
G_HEADS = 4
G_KDIM = D_MODEL // 2
G_VDIM = D_MODEL
G_DK = G_KDIM // G_HEADS
G_DV = G_VDIM // G_HEADS
G_RANK = 16
G_TAU = 16.0
N_EXPERTS = 32
TOP_K = 4
D_FF = D_MODEL
SWIGLU_LIMIT = 7.0
SWIGLU_ALPHA = 1.702
PLE_DIM = 256
SPLITS = (M_DIM, M_DIM, M_DIM, M_HEADS, M_HEADS, M_DIM, G_KDIM, G_KDIM, G_VDIM, G_RANK, G_VDIM, D_MODEL, D_MODEL)
_OFF = np.concatenate([[0], np.cumsum(SPLITS)]).tolist()

LANES = 128
SUBLANES = 8
ROW_CHUNKS = D_MODEL // LANES
VMEM_LIMIT = 56 * 1024 * 1024
MIX_CHUNK = 256
MOE_ROWS = 256
NEG_BIG = -1e30


def _sigmoid(x):
    return 0.5 * jnp.tanh(0.5 * x) + 0.5


def _log_sigmoid(x):
    return jnp.minimum(x, 0.0) - jnp.log(1.0 + jnp.exp(-jnp.abs(x)))


def _rms(x, g):
    return x * lax.rsqrt(jnp.mean(x * x, axis=-1, keepdims=True) + EPS) * g


def _dot(a, b):
    return jnp.dot(a, b, preferred_element_type=F32)


def _dot_nt(a, b):
    return lax.dot_general(a, b, (((1,), (1,)), ((), ())), preferred_element_type=F32)


def _dot_tn(a, b):
    return lax.dot_general(a, b, (((0,), (0,)), ((), ())), preferred_element_type=F32)


def _params(*sem):
    return pltpu.CompilerParams(dimension_semantics=sem, vmem_limit_bytes=VMEM_LIMIT)


def _inproj_kernel(x_ref, g_ref, w_ref, b_ref, ws_ref, bs_ref, z_ref, zs_ref, h_scr):
    @pl.when(pl.program_id(1) == 0)
    def _():
        hb = _rms(x_ref[...], g_ref[...]).astype(BF16)
        h_scr[...] = hb
        zs_ref[...] = _dot(hb, ws_ref[...]) + bs_ref[...]

    z_ref[...] = (_dot(h_scr[...], w_ref[...]) + b_ref[...]).astype(BF16)


def _inproj(x, g, w_big, b_big, w_small, b_small, tm, tn=1024):
    n = x.shape[0]
    cols = w_big.shape[1]
    return pl.pallas_call(
        _inproj_kernel,
        grid=(n // tm, cols // tn),
        in_specs=[
            pl.BlockSpec((tm, D_MODEL), lambda i, j: (i, 0)),
            pl.BlockSpec((1, D_MODEL), lambda i, j: (0, 0)),
            pl.BlockSpec((D_MODEL, tn), lambda i, j: (0, j)),
            pl.BlockSpec((1, tn), lambda i, j: (0, j)),
            pl.BlockSpec((D_MODEL, LANES), lambda i, j: (0, 0)),
            pl.BlockSpec((1, LANES), lambda i, j: (0, 0)),
        ],
        out_specs=[
            pl.BlockSpec((tm, tn), lambda i, j: (i, j)),
            pl.BlockSpec((tm, LANES), lambda i, j: (i, 0)),
        ],
        out_shape=[jax.ShapeDtypeStruct((n, cols), BF16), jax.ShapeDtypeStruct((n, LANES), F32)],
        scratch_shapes=[pltpu.VMEM((tm, D_MODEL), BF16)],
        compiler_params=_params("parallel", "arbitrary"),
        name="inproj",
    )(x, g, w_big, b_big, w_small, b_small)


def _level_matrix(L):
    t = np.arange(L)[:, None]
    s = np.arange(L)[None, :]
    x = np.maximum(t ^ s, 1)
    lv = np.floor(np.log2(x)).astype(np.int32)
    lv = np.where(t == s, -1, lv)
    return np.where(s > t, -2, lv).astype(np.int32)


def _bcast_row(x, period, r):
    L, W = x.shape
    x3 = x.reshape(L // period, period, W)
    return jnp.broadcast_to(x3[:, r:r + 1, :], x3.shape).reshape(L, W)


def _decay_products(ela, L):
    W = ela.shape[1]
    row = lax.broadcasted_iota(jnp.int32, (L, W), 0)
    fwd = [ela]
    rex = [jnp.ones_like(ela)]
    b = 1
    while b < L:
        per = 2 * b
        res = row & (per - 1)
        f, r = fwd[-1], rex[-1]
        if per < SUBLANES:
            mulf = jnp.ones_like(ela)
            mulr = jnp.ones_like(ela)
            for j in range(b, per):
                mulf = jnp.where(res == j, pltpu.roll(f, j - (b - 1), 0), mulf)
            for j in range(0, b):
                mulr = jnp.where(res == j, pltpu.roll(f, L - (per - 1 - j), 0), mulr)
        else:
            mulf = jnp.where(res >= b, _bcast_row(f, per, b - 1), 1.0)
            mulr = jnp.where(res < b, _bcast_row(f, per, per - 1), 1.0)
        fwd.append(f * mulf)
        rex.append(r * mulr)
        b = per
    return fwd, rex


def _mixer_kernel(zqk_ref, zv_ref, zo_ref, zgq_ref, zgk_ref, zgv_ref, zgg_ref, zs_ref,
                  conv0_ref, c0_ref, n0_ref, m0_ref, s0_ref,
                  cw_ref, cb_ref, gm_ref, w2_ref, ba2_ref, gg_ref, lv_ref, shift_ref,
                  ya_ref, yb_ref, tail_ref, c_ref, n_ref, m_ref, st_ref, *, L):
    @pl.when(pl.program_id(1) == 0)
    def _():
        tail_ref[...] = conv0_ref[...]
        c_ref[...] = c0_ref[...]
        n_ref[...] = n0_ref[...]
        m_ref[...] = m0_ref[...]
        st_ref[...] = s0_ref[...]

    lv = lv_ref[...]
    causal = lv >= -1

    u_b = zqk_ref[...]
    u = u_b.astype(F32)
    tail = tail_ref[0]
    row8 = lax.broadcasted_iota(jnp.int32, (SUBLANES, 2 * M_DIM), 0)
    acc = cb_ref[...] + cw_ref[CONV_W - 1:CONV_W, :] * u
    for w in range(CONV_W - 1):
        delay = CONV_W - 1 - w
        shifted = _dot(shift_ref[w], u_b)
        head = jnp.where(row8 < delay, pltpu.roll(tail, delay, 0), 0.0)
        shifted = jnp.concatenate([shifted[0:SUBLANES] + head, shifted[SUBLANES:]], axis=0)
        acc = acc + cw_ref[w:w + 1, :] * shifted
    tail_ref[0] = u[L - SUBLANES:L, :]
    qk = acc * _sigmoid(acc)

    zs = zs_ref[...]
    row = lax.broadcasted_iota(jnp.int32, (L, LANES), 0)
    col = lax.broadcasted_iota(jnp.int32, (L, LANES), 1)
    bcs = _log_sigmoid(zs)
    sh = 1
    while sh < L:
        bcs = bcs + jnp.where(row >= sh, pltpu.roll(bcs, sh, 0), 0.0)
        sh *= 2
    gates = jnp.where(col < M_HEADS, zs, bcs)
    if L < LANES:
        gates = jnp.concatenate([gates, jnp.zeros((LANES - L, LANES), F32)], axis=0)
    gt = gates.T[:, :L]

    for h in range(M_HEADS):
        hs = slice(h * M_HD, (h + 1) * M_HD)
        ig_col = zs[:, h:h + 1]
        b_col = bcs[:, M_HEADS + h:M_HEADS + h + 1]
        ig_row = gt[h:h + 1, :]
        b_row = gt[M_HEADS + h:M_HEADS + h + 1, :]
        m_prev = m_ref[0, h][0:1, 0:1]
        logd = jnp.where(causal, b_col - b_row + ig_row, -jnp.inf)
        inter = b_col + m_prev
        m_t = jnp.maximum(inter, jnp.max(logd, axis=-1, keepdims=True))
        q_f = qk[:, hs]
        k_f = qk[:, M_DIM + h * M_HD:M_DIM + (h + 1) * M_HD] * (M_HD ** -0.5)
        q_b = q_f.astype(BF16)
        v_b = zv_ref[:, hs]
        s = _dot_nt(q_b, k_f.astype(BF16)) * jnp.exp(logd - m_t)
        w_int = jnp.exp(inter - m_t)
        c_old = c_ref[0, h]
        n_old = n_ref[0, h]
        num = _dot(s.astype(BF16), v_b) + w_int * _dot(q_b, c_old.astype(BF16))
        den = jnp.sum(s, axis=-1, keepdims=True) + w_int * jnp.sum(q_f * n_old, axis=-1, keepdims=True)
        hh = num * (1.0 / jnp.maximum(jnp.abs(den), jnp.exp(-m_t)))
        m_new = m_t[L - 1:L, :]
        b_last = b_col[L - 1:L, :]
        w_k = jnp.exp(b_last - b_col + ig_col - m_new)
        dec = jnp.exp(b_last + m_prev - m_new)
        k_w = k_f * w_k
        c_ref[0, h] = dec * c_old + _dot_tn(k_w.astype(BF16), v_b)
        n_ref[0, h] = dec * n_old + jnp.sum(k_w, axis=0, keepdims=True)
        m_ref[0, h] = jnp.broadcast_to(m_new, (SUBLANES, LANES))
        hn = _rms(hh, gm_ref[:, hs])
        ya_ref[:, hs] = (hn * _sigmoid(zo_ref[:, hs].astype(F32))).astype(BF16)

    la = _log_sigmoid(_dot(zs.astype(BF16), w2_ref[...]) + ba2_ref[...]) * (1.0 / G_TAU)
    fwd, rex = _decay_products(jnp.exp(la), L)
    nlev = len(fwd) - 1
    qg = zgq_ref[...].astype(F32) * (G_DK ** -0.5)
    kg = zgk_ref[...].astype(F32)
    q_d = qg.astype(BF16)
    k_d = zgk_ref[...]
    qs = [(qg * fwd[j]).astype(BF16) for j in range(nlev)]
    ks = [k_d] + [(kg * rex[j]).astype(BF16) for j in range(1, nlev)]
    q_in = (qg * fwd[nlev]).astype(BF16)
    k_out = (kg * rex[nlev]).astype(BF16)
    dec_all = fwd[nlev][L - 1:L, :]
    for h in range(G_HEADS):
        ks_ = slice(h * G_DK, (h + 1) * G_DK)
        vs_ = slice(h * G_DV, (h + 1) * G_DV)
        att = jnp.where(lv == -1, _dot_nt(q_d[:, ks_], k_d[:, ks_]), 0.0)
        for j in range(nlev):
            att = jnp.where(lv == j, _dot_nt(qs[j][:, ks_], ks[j][:, ks_]), att)
        v_b = zgv_ref[:, vs_]
        st_old = st_ref[0, h]
        o = _dot(att.astype(BF16), v_b) + _dot_nt(q_in[:, ks_], st_old.astype(BF16))
        st_ref[0, h] = st_old * dec_all[:, ks_] + _dot_tn(v_b, k_out[:, ks_])
        on = _rms(o, gg_ref[:, vs_])
        g = zgg_ref[:, vs_].astype(F32)
        yb_ref[:, vs_] = (on * (g * _sigmoid(g))).astype(BF16)


def _mixers(z, zs, conv0, c0, n0, m0, st0, cw, cb, gm, w2, ba2, gg, batch, T, L):
    nc = T // L
    lv = jnp.asarray(_level_matrix(L))
    shift = jnp.asarray(np.stack([np.eye(L, k=-(CONV_W - 1 - w)) for w in range(CONV_W - 1)]), BF16)
    row_blk = lambda width, cidx: pl.BlockSpec((L, width), lambda b, c: (b * nc + c, cidx))
    state = lambda *shape: pl.BlockSpec((1,) + shape, lambda b, c: (b,) + (0,) * len(shape))
    const = lambda *shape: pl.BlockSpec(shape, lambda b, c: (0,) * len(shape))
    n_tok = batch * T
    return pl.pallas_call(
        functools.partial(_mixer_kernel, L=L),
        grid=(batch, nc),
        in_specs=[
            row_blk(2 * M_DIM, 0),
            row_blk(M_DIM, 2),
            row_blk(M_DIM, 3),
            row_blk(G_KDIM, 8),
            row_blk(G_KDIM, 9),
            row_blk(G_VDIM, 5),
            row_blk(G_VDIM, 6),
            pl.BlockSpec((L, LANES), lambda b, c: (b * nc + c, 0)),
            state(SUBLANES, 2 * M_DIM),
            state(M_HEADS, M_HD, M_HD),
            state(M_HEADS, 1, M_HD),
            state(M_HEADS, SUBLANES, LANES),
            state(G_HEADS, G_DV, G_DK),
            const(CONV_W, 2 * M_DIM),
            const(1, 2 * M_DIM),
            const(1, M_DIM),
            const(LANES, G_KDIM),
            const(1, G_KDIM),
            const(1, G_VDIM),
            const(L, L),
            const(CONV_W - 1, L, L),
        ],
        out_specs=[
            pl.BlockSpec((L, M_DIM), lambda b, c: (b * nc + c, 0)),
            pl.BlockSpec((L, G_VDIM), lambda b, c: (b * nc + c, 0)),
            state(SUBLANES, 2 * M_DIM),
            state(M_HEADS, M_HD, M_HD),
            state(M_HEADS, 1, M_HD),
            state(M_HEADS, SUBLANES, LANES),
            state(G_HEADS, G_DV, G_DK),
        ],
        out_shape=[
            jax.ShapeDtypeStruct((n_tok, M_DIM), BF16),
            jax.ShapeDtypeStruct((n_tok, G_VDIM), BF16),
            jax.ShapeDtypeStruct((batch, SUBLANES, 2 * M_DIM), F32),
            jax.ShapeDtypeStruct((batch, M_HEADS, M_HD, M_HD), F32),
            jax.ShapeDtypeStruct((batch, M_HEADS, 1, M_HD), F32),
            jax.ShapeDtypeStruct((batch, M_HEADS, SUBLANES, LANES), F32),
            jax.ShapeDtypeStruct((batch, G_HEADS, G_DV, G_DK), F32),
        ],
        compiler_params=_params("parallel", "arbitrary"),
        name="mixers",
    )(z, z, z, z, z, z, z, zs, conv0, c0, n0, m0, st0, cw, cb, gm, w2, ba2, gg, lv, shift)


def _outproj_kernel(ya_p, yb_p, ua_p, ub_p, x_p, ya_s, yb_s, ua_s, ub_s, x_s,
                    wpa_ref, wpb_ref, wout_ref, gffn_ref, wr_ref, br_ref, tril_ref,
                    x1_ref, xn_ref, eo_ref, wo_ref, cnt_ref, *, tm, tiles_p, tiles_s):
    i = pl.program_id(0)

    @pl.when(i == 0)
    def _():
        cnt_ref[...] = jnp.zeros_like(cnt_ref)

    def body(ya_ref, yb_ref, ua_ref, ub_ref, x_ref):
        a = _dot(ya_ref[...], wpa_ref[...])
        b = _dot(yb_ref[...], wpb_ref[...])
        merged = _sigmoid(ua_ref[...].astype(F32)) * a + _sigmoid(ub_ref[...].astype(F32)) * b
        x1 = x_ref[...] + _dot(merged.astype(BF16), wout_ref[...])
        x1_ref[...] = x1
        xn = _rms(x1, gffn_ref[...])
        for j in range(ROW_CHUNKS):
            xn_ref[pl.ds(j, tm, stride=ROW_CHUNKS), :] = xn[:, j * LANES:(j + 1) * LANES]

        logits = _dot(xn.astype(BF16), wr_ref[...]) + br_ref[...]
        lane = lax.broadcasted_iota(jnp.int32, (tm, LANES), 1).astype(F32)
        vals, idxs = [], []
        l = logits
        for _ in range(TOP_K):
            mx = jnp.max(l, axis=-1, keepdims=True)
            ix = jnp.min(jnp.where(l == mx, lane, float(LANES)), axis=-1, keepdims=True)
            vals.append(mx)
            idxs.append(ix)
            l = jnp.where(lane == ix, -jnp.inf, l)
        ev = [jnp.exp(v - vals[0]) for v in vals]
        den = ev[0] + ev[1] + ev[2] + ev[3]

        cnt = jnp.zeros((tm, LANES), F32)
        for ix in idxs:
            cnt = cnt + jnp.where(lane == ix, 1.0, 0.0)
        pre = _dot(tril_ref[...], cnt.astype(BF16)) + cnt_ref[0:1, :]
        eo = jnp.zeros((tm, LANES), F32)
        wo = jnp.zeros((tm, LANES), F32)
        for k in range(TOP_K):
            rank = jnp.sum(jnp.where(lane == idxs[k], pre, 0.0), axis=-1, keepdims=True)
            eo = jnp.where(lane == float(k), idxs[k], eo)
            eo = jnp.where(lane == float(TOP_K + k), rank, eo)
            wo = jnp.where(lane == float(k), ev[k] / den, wo)
        eo_ref[...] = eo.T[0:SUBLANES, :].astype(jnp.int32)
        wo_ref[...] = wo
        cnt_ref[...] = cnt_ref[...] + jnp.sum(cnt, axis=0, keepdims=True)

    @pl.when(i < tiles_p)
    def _():
        body(ya_p, yb_p, ua_p, ub_p, x_p)

    @pl.when(jnp.logical_and(i >= tiles_p, i < tiles_p + tiles_s))
    def _():
        body(ya_s, yb_s, ua_s, ub_s, x_s)

    @pl.when(i == tiles_p + tiles_s)
    def _():
        xn_ref[...] = jnp.zeros_like(xn_ref)


def _outproj(prompt, sample, wpa, wpb, wout, gffn, wr, br, tm):
    n_p, n_s = prompt[3].shape[0], sample[3].shape[0]
    tiles_p, tiles_s = n_p // tm, n_s // tm
    tiles = tiles_p + tiles_s
    n = n_p + n_s
    tril = jnp.asarray(np.tril(np.ones((tm, tm), np.float32), -1), BF16)
    pidx = lambda i: jnp.minimum(i, tiles_p - 1)
    sidx = lambda i: jnp.clip(i - tiles_p, 0, tiles_s - 1)
    oidx = lambda i: jnp.minimum(i, tiles - 1)
    src = lambda idx: [pl.BlockSpec((tm, M_DIM), lambda i: (idx(i), 0)),
                       pl.BlockSpec((tm, G_VDIM), lambda i: (idx(i), 0)),
                       pl.BlockSpec((tm, D_MODEL), lambda i: (idx(i), 7)),
                       pl.BlockSpec((tm, D_MODEL), lambda i: (idx(i), 8)),
                       pl.BlockSpec((tm, D_MODEL), lambda i: (idx(i), 0))]
    const = lambda *shape: pl.BlockSpec(shape, lambda i: (0,) * len(shape))
    ya_p, yb_p, z_p, x_p = prompt
    ya_s, yb_s, z_s, x_s = sample
    return pl.pallas_call(
        functools.partial(_outproj_kernel, tm=tm, tiles_p=tiles_p, tiles_s=tiles_s),
        grid=(tiles + 1,),
        in_specs=src(pidx) + src(sidx) + [
            const(M_DIM, D_MODEL), const(G_VDIM, D_MODEL), const(D_MODEL, D_MODEL), const(1, D_MODEL),
            const(D_MODEL, LANES), const(1, LANES), const(tm, tm),
        ],
        out_specs=[
            pl.BlockSpec((tm, D_MODEL), lambda i: (oidx(i), 0)),
            pl.BlockSpec((tm * ROW_CHUNKS, LANES), lambda i: (i, 0)),
            pl.BlockSpec((SUBLANES, tm), lambda i: (0, oidx(i))),
            pl.BlockSpec((tm, LANES), lambda i: (oidx(i), 0)),
            const(SUBLANES, LANES),
        ],
        out_shape=[
            jax.ShapeDtypeStruct((n, D_MODEL), F32),
            jax.ShapeDtypeStruct(((n + tm) * ROW_CHUNKS, LANES), F32),
            jax.ShapeDtypeStruct((SUBLANES, n), jnp.int32),
            jax.ShapeDtypeStruct((n, LANES), F32),
            jax.ShapeDtypeStruct((SUBLANES, LANES), F32),
        ],
        compiler_params=_params("arbitrary"),
        name="outproj_router",
    )(ya_p, yb_p, z_p, z_p, x_p, ya_s, yb_s, z_s, z_s, x_s, wpa, wpb, wout, gffn, wr, br, tril)


def _row_copy(src, src_row, dst, dst_row, sem):
    return pltpu.make_async_copy(
        src.at[pl.ds(pl.multiple_of(src_row * ROW_CHUNKS, ROW_CHUNKS), ROW_CHUNKS)],
        dst.at[pl.ds(pl.multiple_of(dst_row * ROW_CHUNKS, ROW_CHUNKS), ROW_CHUNKS)], sem)


def _dispatch_kernel(pos_ref, fill_ref, xn_ref, xs_ref, zeros_ref, zsem, sem, *, tm, n, n_blocks):
    i = pl.program_id(0)
    base = i * tm

    def issue(r, carry):
        for k in range(TOP_K):
            _row_copy(xn_ref, r, xs_ref, pos_ref[k * n + base + r], sem).start(priority=k % 2)
        return carry

    lax.fori_loop(0, tm, issue, 0)
    for _ in range(TOP_K):
        pltpu.make_async_copy(xn_ref, xs_ref.at[pl.ds(0, tm * ROW_CHUNKS)], sem).wait()

    @pl.when(i == 0)
    def _():
        zeros_ref[...] = jnp.zeros_like(zeros_ref)
        bits = [1 << b for b in range(MOE_ROWS.bit_length() - 1)]

        def zero_copy(off, nrows):
            return pltpu.make_async_copy(
                zeros_ref.at[pl.ds(0, nrows * ROW_CHUNKS)],
                xs_ref.at[pl.ds(pl.multiple_of(off * ROW_CHUNKS, ROW_CHUNKS), nrows * ROW_CHUNKS)], zsem)

        def expert(e, carry):
            npad = fill_ref[N_EXPERTS + e]
            for wait in (False, True):
                off = fill_ref[e]
                for bit in bits:
                    @pl.when((npad & bit) != 0)
                    def _():
                        zero_copy(off, bit).wait() if wait else zero_copy(off, bit).start()
                    off = off + (npad & bit)
            return carry

        lax.fori_loop(0, N_EXPERTS, expert, 0)

        def block(b, carry):
            zero_copy(b * MOE_ROWS, MOE_ROWS).start()
            zero_copy(b * MOE_ROWS, MOE_ROWS).wait()
            return carry

        lax.fori_loop(fill_ref[2 * N_EXPERTS], n_blocks, block, 0)


def _dispatch(pos, fill, xn, n, n_blocks, tm):
    return pl.pallas_call(
        functools.partial(_dispatch_kernel, tm=tm, n=n, n_blocks=n_blocks),
        grid_spec=pltpu.PrefetchScalarGridSpec(
            num_scalar_prefetch=2,
            grid=(n // tm,),
            in_specs=[pl.BlockSpec((tm * ROW_CHUNKS, LANES), lambda i, *_: (i, 0))],
            out_specs=pl.BlockSpec(memory_space=pl.ANY),
            scratch_shapes=[pltpu.VMEM((MOE_ROWS * ROW_CHUNKS, LANES), F32), pltpu.SemaphoreType.DMA(()),
                            pltpu.SemaphoreType.DMA(())],
        ),
        out_shape=jax.ShapeDtypeStruct((n_blocks * MOE_ROWS * ROW_CHUNKS, LANES), F32),
        compiler_params=_params("arbitrary"),
        name="dispatch",
    )(pos, fill, xn)


def _expert_kernel(plan_ref, nused_ref, xs_ref, wgu_hbm, bgu_ref, wd_hbm, bd_ref, y_ref,
                   wgu_f32, wd_f32, wgu_bf, wd_bf, wsem, *, rows):
    i = pl.program_id(0)
    active = i < nused_ref[0]
    expert, next_expert, slot = plan_ref[0, i], plan_ref[1, i], plan_ref[2, i]
    first = jnp.logical_or(i == 0, expert != plan_ref[0, jnp.maximum(i - 1, 0)])

    def weight_copies(e, s):
        return (pltpu.make_async_copy(wgu_hbm.at[e], wgu_f32.at[s], wsem.at[0, s]),
                pltpu.make_async_copy(wd_hbm.at[e], wd_f32.at[s], wsem.at[1, s]))

    @pl.when(i == 0)
    def _():
        for c in weight_copies(expert, slot):
            c.start()

    @pl.when(jnp.logical_and(active, first))
    def _():
        for c in weight_copies(expert, slot):
            c.wait()
        wgu_bf[...] = wgu_f32[slot].astype(BF16)
        wd_bf[...] = wd_f32[slot].astype(BF16)

        @pl.when(next_expert < N_EXPERTS)
        def _():
            for c in weight_copies(next_expert, 1 - slot):
                c.start()

    @pl.when(active)
    def _():
        x = jnp.concatenate([xs_ref[pl.ds(j, rows, stride=ROW_CHUNKS), :] for j in range(ROW_CHUNKS)], axis=-1)
        zz = _dot(x.astype(BF16), wgu_bf[...]) + bgu_ref[...]
        g = jnp.minimum(zz[:, :D_FF], SWIGLU_LIMIT)
        u = jnp.clip(zz[:, D_FF:], -SWIGLU_LIMIT, SWIGLU_LIMIT)
        act = g * _sigmoid(SWIGLU_ALPHA * g) * (u + 1.0)
        y = _dot(act.astype(BF16), wd_bf[...]) + bd_ref[...]
        for j in range(ROW_CHUNKS):
            y_ref[pl.ds(j, rows, stride=ROW_CHUNKS), :] = y[:, j * LANES:(j + 1) * LANES]

    @pl.when(jnp.logical_not(active))
    def _():
        y_ref[...] = jnp.zeros_like(y_ref)


def _experts(plan, n_used, xs, wgu, bgu, wd, bd, rows):
    nb = plan.shape[1]
    return pl.pallas_call(
        functools.partial(_expert_kernel, rows=rows),
        grid_spec=pltpu.PrefetchScalarGridSpec(
            num_scalar_prefetch=2,
            grid=(nb,),
            in_specs=[
                pl.BlockSpec((rows * ROW_CHUNKS, LANES), lambda i, plan, nu: (i, 0)),
                pl.BlockSpec(memory_space=pl.ANY),
                pl.BlockSpec((None, 1, 2 * D_FF), lambda i, plan, nu: (plan[0, i], 0, 0)),
                pl.BlockSpec(memory_space=pl.ANY),
                pl.BlockSpec((None, 1, D_MODEL), lambda i, plan, nu: (plan[0, i], 0, 0)),
            ],
            out_specs=pl.BlockSpec((rows * ROW_CHUNKS, LANES), lambda i, plan, nu: (i, 0)),
            scratch_shapes=[pltpu.VMEM((2, D_MODEL, 2 * D_FF), F32), pltpu.VMEM((2, D_FF, D_MODEL), F32),
                            pltpu.VMEM((D_MODEL, 2 * D_FF), BF16), pltpu.VMEM((D_FF, D_MODEL), BF16),
                            pltpu.SemaphoreType.DMA((2, 2))],
        ),
        out_shape=jax.ShapeDtypeStruct(xs.shape, F32),
        compiler_params=_params("arbitrary"),
        name="experts",
    )(plan, n_used, xs, wgu, bgu, wd, bd)


def _combine_kernel(pos_ref, ye_ref, x1_ref, wo_ref, p_ref, wproj_ref, wgate_ref, gple_ref, gfin_ref,
                    y_ref, rows_a, rows_b, sem, *, tm, nt, n_all, tok0):
    i = pl.program_id(0)
    seg = tm * ROW_CHUNKS

    def start_row(tok, r, dst, sem_slot):
        for k in range(TOP_K):
            srow = pl.multiple_of(pos_ref[k * n_all + tok] * ROW_CHUNKS, ROW_CHUNKS)
            pltpu.make_async_copy(ye_ref.at[pl.ds(srow, ROW_CHUNKS)],
                                  dst.at[pl.ds((k * tm + r) * ROW_CHUNKS, ROW_CHUNKS)], sem_slot).start(priority=k % 2)

    def wait_rows(dst, sem_slot):
        for k in range(TOP_K):
            pltpu.make_async_copy(ye_ref.at[pl.ds(0, seg)], dst.at[pl.ds(k * seg, seg)], sem_slot).wait()

    def compute(half, src):
        rs = slice(half * tm, (half + 1) * tm)
        wo = wo_ref[rs, :]
        moe = jnp.zeros((tm, D_MODEL), F32)
        for k in range(TOP_K):
            rows = jnp.concatenate(
                [src[pl.ds(k * seg + j, tm, stride=ROW_CHUNKS), :] for j in range(ROW_CHUNKS)], axis=-1)
            moe = moe + rows * wo[:, k:k + 1]
        x2 = x1_ref[rs, :] + moe
        gate = _sigmoid(_dot(_rms(x2, gple_ref[...]).astype(BF16), wgate_ref[...]))
        x3 = x2 + _dot(p_ref[rs, :].astype(BF16), wproj_ref[...]) * gate
        y_ref[rs, :] = _rms(x3, gfin_ref[...])

    @pl.when(i == 0)
    def _():
        def issue(r, carry):
            start_row(tok0 + r, r, rows_a, sem.at[0])
            return carry
        lax.fori_loop(0, tm, issue, 0)

    wait_rows(rows_a, sem.at[0])
    base_b = tok0 + (2 * i + 1) * tm
    for r in range(tm):
        start_row(base_b + r, r, rows_b, sem.at[1])
    compute(0, rows_a)

    wait_rows(rows_b, sem.at[1])
    base_a = tok0 + jnp.minimum(2 * i + 2, nt - 1) * tm
    for r in range(tm):
        start_row(base_a + r, r, rows_a, sem.at[0])
    compute(1, rows_b)

    @pl.when(i == nt // 2 - 1)
    def _():
        wait_rows(rows_a, sem.at[0])


def _combine(pos_flat, ye, x1, wo, p, wproj, wgate, gple, gfin, tm, tok0):
    n = p.shape[0]
    n_all = x1.shape[0]
    nt = n // tm
    assert nt % 2 == 0 and tok0 % (2 * tm) == 0
    blk0 = tok0 // (2 * tm)
    tok = lambda width: pl.BlockSpec((2 * tm, width), lambda i, pos: (blk0 + i, 0))
    const = lambda *shape: pl.BlockSpec(shape, lambda i, pos: (0,) * len(shape))
    rows_buf = pltpu.VMEM((TOP_K * tm * ROW_CHUNKS, LANES), F32)
    return pl.pallas_call(
        functools.partial(_combine_kernel, tm=tm, nt=nt, n_all=n_all, tok0=tok0),
        grid_spec=pltpu.PrefetchScalarGridSpec(
            num_scalar_prefetch=1,
            grid=(nt // 2,),
            in_specs=[
                pl.BlockSpec(memory_space=pl.ANY),
                tok(D_MODEL), tok(LANES), pl.BlockSpec((2 * tm, PLE_DIM), lambda i, pos: (i, 0)),
                const(PLE_DIM, D_MODEL), const(D_MODEL, D_MODEL), const(1, D_MODEL), const(1, D_MODEL),
            ],
            out_specs=pl.BlockSpec((2 * tm, D_MODEL), lambda i, pos: (i, 0)),
            scratch_shapes=[rows_buf, rows_buf, pltpu.SemaphoreType.DMA((2,))],
        ),
        out_shape=jax.ShapeDtypeStruct((n, D_MODEL), F32),
        compiler_params=_params("arbitrary"),
        name="combine_ple",
    )(pos_flat, ye, x1, wo, p, wproj, wgate, gple, gfin)


def _tile(n, pref):
    t = pref
    while n % t:
        t //= 2
    return t


def _seg(a, i):
    return a[..., _OFF[i]:_OFF[i + 1]]


def kernel(x_prompt, x_sample, state_conv, state_mlstm_C, state_mlstm_n, state_mlstm_m, state_gla_S, p_prompt, p_sample, g_mix, w_in, b_in, conv_w, conv_b, g_mnorm, w_a2, b_a2, g_gnorm, w_pa, w_pb, w_out, g_ffn, w_router, b_router, w_gate_up, b_gate_up, w_down, b_down, g_ple, w_ple_proj, w_ple_gate, g_final):
    assert g_mix.shape[0] == 1, "single-layer kernel"
    bp, tp, _ = x_prompt.shape
    bs, ts, _ = x_sample.shape

    big = (0, 1, 2, 5, 6, 7, 8, 10, 11, 12)
    w_big = jnp.concatenate([_seg(w_in[0], i) for i in big], axis=-1).astype(BF16)
    b_big = jnp.concatenate([_seg(b_in[0], i) for i in big], axis=-1)[None, :]
    small = (3, 4, 9)
    n_small = sum(SPLITS[i] for i in small)
    w_small = jnp.pad(jnp.concatenate([_seg(w_in[0], i) for i in small], axis=-1),
                      ((0, 0), (0, LANES - n_small))).astype(BF16)
    b_small = jnp.pad(jnp.concatenate([_seg(b_in[0], i) for i in small], axis=-1), (0, LANES - n_small))[None, :]
    w2 = jnp.pad(w_a2[0], ((2 * M_HEADS, LANES - 2 * M_HEADS - G_RANK), (0, 0))).astype(BF16)
    wr = jnp.pad(w_router[0], ((0, 0), (0, LANES - N_EXPERTS))).astype(BF16)
    br = jnp.pad(b_router[0], (0, LANES - N_EXPERTS), constant_values=NEG_BIG)[None, :]
    wpa, wpb, wout = w_pa[0].astype(BF16), w_pb[0].astype(BF16), w_out[0].astype(BF16)
    wproj, wgate = w_ple_proj[0].astype(BF16), w_ple_gate[0].astype(BF16)
    row = lambda a: a.reshape(1, -1)

    def mix_path(x3d, conv0, c0, n0, m0, s0):
        batch, T, _ = x3d.shape
        x = x3d.reshape(batch * T, D_MODEL)
        z, zs = _inproj(x, row(g_mix[0]), w_big, b_big, w_small, b_small, _tile(batch * T, 1024))
        L = min(MIX_CHUNK, T)
        conv0p = jnp.pad(conv0, ((0, 0), (SUBLANES - (CONV_W - 1), 0), (0, 0)))
        m0b = jnp.broadcast_to(m0[:, :, None, None], (batch, M_HEADS, SUBLANES, LANES))
        ya, yb, tail, c1, n1, m1, st1 = _mixers(
            z, zs, conv0p, c0, n0[:, :, None, :], m0b, jnp.swapaxes(s0, -1, -2),
            conv_w[0], row(conv_b[0]), row(g_mnorm[0]), w2, row(b_a2[0]), row(g_gnorm[0]), batch, T, L)
        new_state = (tail[:, SUBLANES - (CONV_W - 1):, :], c1, n1[:, :, 0, :], m1[:, :, 0, 0],
                     jnp.swapaxes(st1, -1, -2))
        return x, z, ya, yb, new_state

    zero_state = (jnp.zeros((bp, CONV_W - 1, 2 * M_DIM), F32), jnp.zeros((bp, M_HEADS, M_HD, M_HD), F32),
                  jnp.zeros((bp, M_HEADS, M_HD), F32), jnp.zeros((bp, M_HEADS), F32),
                  jnp.zeros((bp, G_HEADS, G_DK, G_DV), F32))
    xp, zp, yap, ybp, st_p = mix_path(x_prompt, *zero_state)
    xs_, zs_, yas, ybs, st_s = mix_path(x_sample, state_conv[0], state_mlstm_C[0], state_mlstm_n[0],
                                        state_mlstm_m[0], state_gla_S[0])

    n_p, n_s = xp.shape[0], xs_.shape[0]
    n_all = n_p + n_s
    tm = _tile(n_s, 256)
    assert n_p % tm == 0
    x1, xn, eo, wo, cnt = _outproj((yap, ybp, zp, xp), (yas, ybs, zs_, xs_), wpa, wpb, wout, row(g_ffn[0]), wr, br, tm)

    counts = cnt[0, :N_EXPERTS].astype(jnp.int32)
    padded = (counts + MOE_ROWS - 1) // MOE_ROWS * MOE_ROWS
    pend = jnp.cumsum(padded)
    pstart = pend - padded
    nk = n_all * TOP_K
    n_blocks = (nk + N_EXPERTS * (MOE_ROWS - 1) + MOE_ROWS - 1) // MOE_ROWS
    starts = jnp.arange(n_blocks, dtype=jnp.int32) * MOE_ROWS
    block_e = jnp.minimum(jnp.sum((pend[None, :] <= starts[:, None]).astype(jnp.int32), axis=1), N_EXPERTS - 1)
    ids = jnp.arange(N_EXPERTS, dtype=jnp.int32)
    later = jnp.where((ids[None, :] > ids[:, None]) & (counts[None, :] > 0), ids[None, :], N_EXPERTS)
    next_e = jnp.min(later, axis=1)
    group = jnp.cumsum(jnp.concatenate([jnp.zeros((1,), jnp.int32),
                                        (block_e[1:] != block_e[:-1]).astype(jnp.int32)]))
    plan = jnp.stack([block_e, next_e[block_e], group % 2]).astype(jnp.int32)
    n_used = (pend[-1:] // MOE_ROWS).astype(jnp.int32)
    e, rank = eo[:TOP_K], eo[TOP_K:2 * TOP_K]
    onehot = e[..., None] == jnp.arange(N_EXPERTS, dtype=jnp.int32)
    pos = (jnp.sum(jnp.where(onehot, pstart, 0), axis=-1) + rank).reshape(-1)
    fill = jnp.concatenate([pstart + counts, padded - counts, n_used]).astype(jnp.int32)

    xs_sorted = _dispatch(pos, fill, xn, n_all, n_blocks, tm)
    ye = _experts(plan, n_used, xs_sorted, w_gate_up[0], b_gate_up[0][:, None, :], w_down[0],
                  b_down[0][:, None, :], MOE_ROWS)
    fin = lambda p, tok0, tmc: _combine(pos, ye, x1, wo, p.reshape(-1, PLE_DIM), wproj, wgate,
                                        row(g_ple[0]), row(g_final), tmc, tok0)
    y_p = fin(p_prompt[0], 0, tm).reshape(bp, tp, D_MODEL)
    y_s = fin(p_sample[0], n_p, _tile(n_s // 2, tm)).reshape(bs, ts, D_MODEL)

    lead = lambda a: a[None]
    return (y_p, y_s) + tuple(lead(a) for a in st_p) + tuple(lead(a) for a in st_s)
```

```python
import functools

import numpy as np
import jax
import jax.numpy as jnp
from jax import lax
from jax.experimental import pallas as pl
from jax.experimental.pallas import tpu as pltpu

F32 = jnp.float32
BF16 = jnp.bfloat16

D_MODEL = 1024
EPS = 1e-6
M_HEADS = 4
M_DIM = D_MODEL
M_HD = M_DIM // M_HEADS
CONV_W = 4
G_HEADS = 4
G_KDIM = D_MODEL // 2
G_VDIM = D_MODEL
G_DK = G_KDIM // G_HEADS
G_DV = G_VDIM // G_HEADS
G_RANK = 16
G_TAU = 16.0
N_EXPERTS = 32
TOP_K = 4
D_FF = D_MODEL
SWIGLU_LIMIT = 7.0
SWIGLU_ALPHA = 1.702
PLE_DIM = 256
SPLITS = (M_DIM, M_DIM, M_DIM, M_HEADS, M_HEADS, M_DIM, G_KDIM, G_KDIM, G_VDIM, G_RANK, G_VDIM, D_MODEL, D_MODEL)
_OFF = np.concatenate([[0], np.cumsum(SPLITS)]).tolist()

LANES = 128
SUBLANES = 8
ROW_CHUNKS = D_MODEL // LANES
VMEM_LIMIT = 56 * 1024 * 1024
MIX_CHUNK = 256
MOE_ROWS = 512
NEG_BIG = -1e30


def _sigmoid(x):
    return 0.5 * jnp.tanh(0.5 * x) + 0.5


def _log_sigmoid(x):
    return jnp.minimum(x, 0.0) - jnp.log(1.0 + jnp.exp(-jnp.abs(x)))


def _rms(x, g):
    return x * lax.rsqrt(jnp.mean(x * x, axis=-1, keepdims=True) + EPS) * g


def _dot(a, b):
    return jnp.dot(a, b, preferred_element_type=F32)


def _dot_nt(a, b):
    return lax.dot_general(a, b, (((1,), (1,)), ((), ())), preferred_element_type=F32)


def _dot_tn(a, b):
    return lax.dot_general(a, b, (((0,), (0,)), ((), ())), preferred_element_type=F32)


def _params(*sem):
    return pltpu.CompilerParams(dimension_semantics=sem, vmem_limit_bytes=VMEM_LIMIT)


def _inproj_kernel(x_ref, g_ref, w_ref, b_ref, ws_ref, bs_ref, z_ref, zs_ref, h_scr):
    @pl.when(pl.program_id(1) == 0)
    def _():
        hb = _rms(x_ref[...], g_ref[...]).astype(BF16)
        h_scr[...] = hb
        zs_ref[...] = _dot(hb, ws_ref[...]) + bs_ref[...]

    z_ref[...] = (_dot(h_scr[...], w_ref[...]) + b_ref[...]).astype(BF16)


def _inproj(x, g, w_big, b_big, w_small, b_small, tm, tn=1024):
    n = x.shape[0]
    cols = w_big.shape[1]
    return pl.pallas_call(
        _inproj_kernel,
        grid=(n // tm, cols // tn),
        in_specs=[
            pl.BlockSpec((tm, D_MODEL), lambda i, j: (i, 0)),
            pl.BlockSpec((1, D_MODEL), lambda i, j: (0, 0)),
            pl.BlockSpec((D_MODEL, tn), lambda i, j: (0, j)),
            pl.BlockSpec((1, tn), lambda i, j: (0, j)),
            pl.BlockSpec((D_MODEL, LANES), lambda i, j: (0, 0)),
            pl.BlockSpec((1, LANES), lambda i, j: (0, 0)),
        ],
        out_specs=[
            pl.BlockSpec((tm, tn), lambda i, j: (i, j)),
            pl.BlockSpec((tm, LANES), lambda i, j: (i, 0)),
        ],
        out_shape=[jax.ShapeDtypeStruct((n, cols), BF16), jax.ShapeDtypeStruct((n, LANES), F32)],
        scratch_shapes=[pltpu.VMEM((tm, D_MODEL), BF16)],
        compiler_params=_params("parallel", "arbitrary"),
        name="inproj",
    )(x, g, w_big, b_big, w_small, b_small)


def _level_matrix(L):
    t = np.arange(L)[:, None]
    s = np.arange(L)[None, :]
    x = np.maximum(t ^ s, 1)
    lv = np.floor(np.log2(x)).astype(np.int32)
    lv = np.where(t == s, -1, lv)
    return np.where(s > t, -2, lv).astype(np.int32)


def _bcast_row(x, period, r):
    L, W = x.shape
    x3 = x.reshape(L // period, period, W)
    return jnp.broadcast_to(x3[:, r:r + 1, :], x3.shape).reshape(L, W)


def _decay_products(ela, L):
    W = ela.shape[1]
    row = lax.broadcasted_iota(jnp.int32, (L, W), 0)
    fwd = [ela]
    rex = [jnp.ones_like(ela)]
    b = 1
    while b < L:
        per = 2 * b
        res = row & (per - 1)
        f, r = fwd[-1], rex[-1]
        if per < SUBLANES:
            mulf = jnp.ones_like(ela)
            mulr = jnp.ones_like(ela)
            for j in range(b, per):
                mulf = jnp.where(res == j, pltpu.roll(f, j - (b - 1), 0), mulf)
            for j in range(0, b):
                mulr = jnp.where(res == j, pltpu.roll(f, L - (per - 1 - j), 0), mulr)
        else:
            mulf = jnp.where(res >= b, _bcast_row(f, per, b - 1), 1.0)
            mulr = jnp.where(res < b, _bcast_row(f, per, per - 1), 1.0)
        fwd.append(f * mulf)
        rex.append(r * mulr)
        b = per
    return fwd, rex


def _mixer_kernel(zqk_ref, zv_ref, zo_ref, zgq_ref, zgk_ref, zgv_ref, zgg_ref, zs_ref,
                  conv0_ref, c0_ref, n0_ref, m0_ref, s0_ref,
                  cw_ref, cb_ref, gm_ref, w2_ref, ba2_ref, gg_ref, lv_ref, shift_ref,
                  ya_ref, yb_ref, tail_ref, c_ref, n_ref, m_ref, st_ref, *, L):
    @pl.when(pl.program_id(1) == 0)
    def _():
        tail_ref[...] = conv0_ref[...]
        c_ref[...] = c0_ref[...]
        n_ref[...] = n0_ref[...]
        m_ref[...] = m0_ref[...]
        st_ref[...] = s0_ref[...]

    lv = lv_ref[...]
    causal = lv >= -1

    u_b = zqk_ref[...]
    u = u_b.astype(F32)
    tail = tail_ref[0]
    row8 = lax.broadcasted_iota(jnp.int32, (SUBLANES, 2 * M_DIM), 0)
    acc = cb_ref[...] + cw_ref[CONV_W - 1:CONV_W, :] * u
    for w in range(CONV_W - 1):
        delay = CONV_W - 1 - w
        shifted = _dot(shift_ref[w], u_b)
        head = jnp.where(row8 < delay, pltpu.roll(tail, delay, 0), 0.0)
        shifted = jnp.concatenate([shifted[0:SUBLANES] + head, shifted[SUBLANES:]], axis=0)
        acc = acc + cw_ref[w:w + 1, :] * shifted
    tail_ref[0] = u[L - SUBLANES:L, :]
    qk = acc * _sigmoid(acc)

    zs = zs_ref[...]
    row = lax.broadcasted_iota(jnp.int32, (L, LANES), 0)
    col = lax.broadcasted_iota(jnp.int32, (L, LANES), 1)
    bcs = _log_sigmoid(zs)
    sh = 1
    while sh < L:
        bcs = bcs + jnp.where(row >= sh, pltpu.roll(bcs, sh, 0), 0.0)
        sh *= 2
    gates = jnp.where(col < M_HEADS, zs, bcs)
    if L < LANES:
        gates = jnp.concatenate([gates, jnp.zeros((LANES - L, LANES), F32)], axis=0)
    gt = gates.T[:, :L]

    for h in range(M_HEADS):
        hs = slice(h * M_HD, (h + 1) * M_HD)
        ig_col = zs[:, h:h + 1]
        b_col = bcs[:, M_HEADS + h:M_HEADS + h + 1]
        ig_row = gt[h:h + 1, :]
        b_row = gt[M_HEADS + h:M_HEADS + h + 1, :]
        m_prev = m_ref[0, h][0:1, 0:1]
        logd = jnp.where(causal, b_col - b_row + ig_row, -jnp.inf)
        inter = b_col + m_prev
        m_t = jnp.maximum(inter, jnp.max(logd, axis=-1, keepdims=True))
        q_f = qk[:, hs]
        k_f = qk[:, M_DIM + h * M_HD:M_DIM + (h + 1) * M_HD] * (M_HD ** -0.5)
        q_b = q_f.astype(BF16)
        v_b = zv_ref[:, hs]
        s = _dot_nt(q_b, k_f.astype(BF16)) * jnp.exp(logd - m_t)
        w_int = jnp.exp(inter - m_t)
        c_old = c_ref[0, h]
        n_old = n_ref[0, h]
        num = _dot(s.astype(BF16), v_b) + w_int * _dot(q_b, c_old.astype(BF16))
        den = jnp.sum(s, axis=-1, keepdims=True) + w_int * jnp.sum(q_f * n_old, axis=-1, keepdims=True)
        hh = num * (1.0 / jnp.maximum(jnp.abs(den), jnp.exp(-m_t)))
        m_new = m_t[L - 1:L, :]
        b_last = b_col[L - 1:L, :]
        w_k = jnp.exp(b_last - b_col + ig_col - m_new)
        dec = jnp.exp(b_last + m_prev - m_new)
        k_w = k_f * w_k
        c_ref[0, h] = dec * c_old + _dot_tn(k_w.astype(BF16), v_b)
        n_ref[0, h] = dec * n_old + jnp.sum(k_w, axis=0, keepdims=True)
        m_ref[0, h] = jnp.broadcast_to(m_new, (SUBLANES, LANES))
        hn = _rms(hh, gm_ref[:, hs])
        ya_ref[:, hs] = (hn * _sigmoid(zo_ref[:, hs].astype(F32))).astype(BF16)

    la = _log_sigmoid(_dot(zs.astype(BF16), w2_ref[...]) + ba2_ref[...]) * (1.0 / G_TAU)
    fwd, rex = _decay_products(jnp.exp(la), L)
    nlev = len(fwd) - 1
    qg = zgq_ref[...].astype(F32) * (G_DK ** -0.5)
    kg = zgk_ref[...].astype(F32)
    q_d = qg.astype(BF16)
    k_d = zgk_ref[...]
    qs = [(qg * fwd[j]).astype(BF16) for j in range(nlev)]
    ks = [k_d] + [(kg * rex[j]).astype(BF16) for j in range(1, nlev)]
    q_in = (qg * fwd[nlev]).astype(BF16)
    k_out = (kg * rex[nlev]).astype(BF16)
    dec_all = fwd[nlev][L - 1:L, :]
    for h in range(G_HEADS):
        ks_ = slice(h * G_DK, (h + 1) * G_DK)
        vs_ = slice(h * G_DV, (h + 1) * G_DV)
        att = jnp.where(lv == -1, _dot_nt(q_d[:, ks_], k_d[:, ks_]), 0.0)
        for j in range(nlev):
            att = jnp.where(lv == j, _dot_nt(qs[j][:, ks_], ks[j][:, ks_]), att)
        v_b = zgv_ref[:, vs_]
        st_old = st_ref[0, h]
        o = _dot(att.astype(BF16), v_b) + _dot_nt(q_in[:, ks_], st_old.astype(BF16))
        st_ref[0, h] = st_old * dec_all[:, ks_] + _dot_tn(v_b, k_out[:, ks_])
        on = _rms(o, gg_ref[:, vs_])
        g = zgg_ref[:, vs_].astype(F32)
        yb_ref[:, vs_] = (on * (g * _sigmoid(g))).astype(BF16)


def _mixers(z, zs, conv0, c0, n0, m0, st0, cw, cb, gm, w2, ba2, gg, batch, T, L):
    nc = T // L
    lv = jnp.asarray(_level_matrix(L))
    shift = jnp.asarray(np.stack([np.eye(L, k=-(CONV_W - 1 - w)) for w in range(CONV_W - 1)]), BF16)
    row_blk = lambda width, cidx: pl.BlockSpec((L, width), lambda b, c: (b * nc + c, cidx))
    state = lambda *shape: pl.BlockSpec((1,) + shape, lambda b, c: (b,) + (0,) * len(shape))
    const = lambda *shape: pl.BlockSpec(shape, lambda b, c: (0,) * len(shape))
    n_tok = batch * T
    return pl.pallas_call(
        functools.partial(_mixer_kernel, L=L),
        grid=(batch, nc),
        in_specs=[
            row_blk(2 * M_DIM, 0),
            row_blk(M_DIM, 2),
            row_blk(M_DIM, 3),
            row_blk(G_KDIM, 8),
            row_blk(G_KDIM, 9),
            row_blk(G_VDIM, 5),
            row_blk(G_VDIM, 6),
            pl.BlockSpec((L, LANES), lambda b, c: (b * nc + c, 0)),
            state(SUBLANES, 2 * M_DIM),
            state(M_HEADS, M_HD, M_HD),
            state(M_HEADS, 1, M_HD),
            state(M_HEADS, SUBLANES, LANES),
            state(G_HEADS, G_DV, G_DK),
            const(CONV_W, 2 * M_DIM),
            const(1, 2 * M_DIM),
            const(1, M_DIM),
            const(LANES, G_KDIM),
            const(1, G_KDIM),
            const(1, G_VDIM),
            const(L, L),
            const(CONV_W - 1, L, L),
        ],
        out_specs=[
            pl.BlockSpec((L, M_DIM), lambda b, c: (b * nc + c, 0)),
            pl.BlockSpec((L, G_VDIM), lambda b, c: (b * nc + c, 0)),
            state(SUBLANES, 2 * M_DIM),
            state(M_HEADS, M_HD, M_HD),
            state(M_HEADS, 1, M_HD),
            state(M_HEADS, SUBLANES, LANES),
            state(G_HEADS, G_DV, G_DK),
        ],
        out_shape=[
            jax.ShapeDtypeStruct((n_tok, M_DIM), BF16),
            jax.ShapeDtypeStruct((n_tok, G_VDIM), BF16),
            jax.ShapeDtypeStruct((batch, SUBLANES, 2 * M_DIM), F32),
            jax.ShapeDtypeStruct((batch, M_HEADS, M_HD, M_HD), F32),
            jax.ShapeDtypeStruct((batch, M_HEADS, 1, M_HD), F32),
            jax.ShapeDtypeStruct((batch, M_HEADS, SUBLANES, LANES), F32),
            jax.ShapeDtypeStruct((batch, G_HEADS, G_DV, G_DK), F32),
        ],
        compiler_params=_params("parallel", "arbitrary"),
        name="mixers",
    )(z, z, z, z, z, z, z, zs, conv0, c0, n0, m0, st0, cw, cb, gm, w2, ba2, gg, lv, shift)


def _outproj_kernel(ya_p, yb_p, ua_p, ub_p, x_p, ya_s, yb_s, ua_s, ub_s, x_s,
                    wpa_ref, wpb_ref, wout_ref, gffn_ref, wr_ref, br_ref, tril_ref,
                    x1_ref, xn_ref, eo_ref, wo_ref, cnt_ref, *, tm, tiles_p, tiles_s):
    i = pl.program_id(0)

    @pl.when(i == 0)
    def _():
        cnt_ref[...] = jnp.zeros_like(cnt_ref)

    def body(ya_ref, yb_ref, ua_ref, ub_ref, x_ref):
        a = _dot(ya_ref[...], wpa_ref[...])
        b = _dot(yb_ref[...], wpb_ref[...])
        merged = _sigmoid(ua_ref[...].astype(F32)) * a + _sigmoid(ub_ref[...].astype(F32)) * b
        x1 = x_ref[...] + _dot(merged.astype(BF16), wout_ref[...])
        x1_ref[...] = x1
        xn = _rms(x1, gffn_ref[...])
        for j in range(ROW_CHUNKS):
            xn_ref[pl.ds(j, tm, stride=ROW_CHUNKS), :] = xn[:, j * LANES:(j + 1) * LANES]

        logits = _dot(xn.astype(BF16), wr_ref[...]) + br_ref[...]
        lane = lax.broadcasted_iota(jnp.int32, (tm, LANES), 1).astype(F32)
        vals, idxs = [], []
        l = logits
        for _ in range(TOP_K):
            mx = jnp.max(l, axis=-1, keepdims=True)
            ix = jnp.min(jnp.where(l == mx, lane, float(LANES)), axis=-1, keepdims=True)
            vals.append(mx)
            idxs.append(ix)
            l = jnp.where(lane == ix, -jnp.inf, l)
        ev = [jnp.exp(v - vals[0]) for v in vals]
        den = ev[0] + ev[1] + ev[2] + ev[3]

        cnt = jnp.zeros((tm, LANES), F32)
        for ix in idxs:
            cnt = cnt + jnp.where(lane == ix, 1.0, 0.0)
        pre = _dot(tril_ref[...], cnt.astype(BF16)) + cnt_ref[0:1, :]
        eo = jnp.zeros((tm, LANES), F32)
        wo = jnp.zeros((tm, LANES), F32)
        for k in range(TOP_K):
            rank = jnp.sum(jnp.where(lane == idxs[k], pre, 0.0), axis=-1, keepdims=True)
            eo = jnp.where(lane == float(k), idxs[k], eo)
            eo = jnp.where(lane == float(TOP_K + k), rank, eo)
            wo = jnp.where(lane == float(k), ev[k] / den, wo)
        eo_ref[...] = eo.T[0:SUBLANES, :].astype(jnp.int32)
        wo_ref[...] = wo
        cnt_ref[...] = cnt_ref[...] + jnp.sum(cnt, axis=0, keepdims=True)

    @pl.when(i < tiles_p)
    def _():
        body(ya_p, yb_p, ua_p, ub_p, x_p)

    @pl.when(jnp.logical_and(i >= tiles_p, i < tiles_p + tiles_s))
    def _():
        body(ya_s, yb_s, ua_s, ub_s, x_s)

    @pl.when(i == tiles_p + tiles_s)
    def _():
        xn_ref[...] = jnp.zeros_like(xn_ref)


def _outproj(prompt, sample, wpa, wpb, wout, gffn, wr, br, tm):
    n_p, n_s = prompt[3].shape[0], sample[3].shape[0]
    tiles_p, tiles_s = n_p // tm, n_s // tm
    tiles = tiles_p + tiles_s
    n = n_p + n_s
    tril = jnp.asarray(np.tril(np.ones((tm, tm), np.float32), -1), BF16)
    pidx = lambda i: jnp.minimum(i, tiles_p - 1)
    sidx = lambda i: jnp.clip(i - tiles_p, 0, tiles_s - 1)
    oidx = lambda i: jnp.minimum(i, tiles - 1)
    src = lambda idx: [pl.BlockSpec((tm, M_DIM), lambda i: (idx(i), 0)),
                       pl.BlockSpec((tm, G_VDIM), lambda i: (idx(i), 0)),
                       pl.BlockSpec((tm, D_MODEL), lambda i: (idx(i), 7)),
                       pl.BlockSpec((tm, D_MODEL), lambda i: (idx(i), 8)),
                       pl.BlockSpec((tm, D_MODEL), lambda i: (idx(i), 0))]
    const = lambda *shape: pl.BlockSpec(shape, lambda i: (0,) * len(shape))
    ya_p, yb_p, z_p, x_p = prompt
    ya_s, yb_s, z_s, x_s = sample
    return pl.pallas_call(
        functools.partial(_outproj_kernel, tm=tm, tiles_p=tiles_p, tiles_s=tiles_s),
        grid=(tiles + 1,),
        in_specs=src(pidx) + src(sidx) + [
            const(M_DIM, D_MODEL), const(G_VDIM, D_MODEL), const(D_MODEL, D_MODEL), const(1, D_MODEL),
            const(D_MODEL, LANES), const(1, LANES), const(tm, tm),
        ],
        out_specs=[
            pl.BlockSpec((tm, D_MODEL), lambda i: (oidx(i), 0)),
            pl.BlockSpec((tm * ROW_CHUNKS, LANES), lambda i: (i, 0)),
            pl.BlockSpec((SUBLANES, tm), lambda i: (0, oidx(i))),
            pl.BlockSpec((tm, LANES), lambda i: (oidx(i), 0)),
            const(SUBLANES, LANES),
        ],
        out_shape=[
            jax.ShapeDtypeStruct((n, D_MODEL), F32),
            jax.ShapeDtypeStruct(((n + tm) * ROW_CHUNKS, LANES), F32),
            jax.ShapeDtypeStruct((SUBLANES, n), jnp.int32),
            jax.ShapeDtypeStruct((n, LANES), F32),
            jax.ShapeDtypeStruct((SUBLANES, LANES), F32),
        ],
        compiler_params=_params("arbitrary"),
        name="outproj_router",
    )(ya_p, yb_p, z_p, z_p, x_p, ya_s, yb_s, z_s, z_s, x_s, wpa, wpb, wout, gffn, wr, br, tril)


def _row_copy(src, src_row, dst, dst_row, sem):
    return pltpu.make_async_copy(
        src.at[pl.ds(pl.multiple_of(src_row * ROW_CHUNKS, ROW_CHUNKS), ROW_CHUNKS)],
        dst.at[pl.ds(pl.multiple_of(dst_row * ROW_CHUNKS, ROW_CHUNKS), ROW_CHUNKS)], sem)


def _dispatch_kernel(pos_ref, fill_ref, xn_ref, xs_ref, zeros_ref, zsem, sem, *, tm, n, n_blocks):
    i = pl.program_id(0)
    base = i * tm

    def issue(r, carry):
        for k in range(TOP_K):
            _row_copy(xn_ref, r, xs_ref, pos_ref[k * n + base + r], sem).start(priority=k % 2)
        return carry

    lax.fori_loop(0, tm, issue, 0)
    for _ in range(TOP_K):
        pltpu.make_async_copy(xn_ref, xs_ref.at[pl.ds(0, tm * ROW_CHUNKS)], sem).wait()

    @pl.when(i == 0)
    def _():
        zeros_ref[...] = jnp.zeros_like(zeros_ref)
        bits = [1 << b for b in range(MOE_ROWS.bit_length() - 1)]

        def zero_copy(off, nrows):
            return pltpu.make_async_copy(
                zeros_ref.at[pl.ds(0, nrows * ROW_CHUNKS)],
                xs_ref.at[pl.ds(pl.multiple_of(off * ROW_CHUNKS, ROW_CHUNKS), nrows * ROW_CHUNKS)], zsem)

        def expert(e, carry):
            npad = fill_ref[N_EXPERTS + e]
            for wait in (False, True):
                off = fill_ref[e]
                for bit in bits:
                    @pl.when((npad & bit) != 0)
                    def _():
                        zero_copy(off, bit).wait() if wait else zero_copy(off, bit).start()
                    off = off + (npad & bit)
            return carry

        lax.fori_loop(0, N_EXPERTS, expert, 0)

        def block(b, carry):
            zero_copy(b * MOE_ROWS, MOE_ROWS).start()
            zero_copy(b * MOE_ROWS, MOE_ROWS).wait()
            return carry

        lax.fori_loop(fill_ref[2 * N_EXPERTS], n_blocks, block, 0)


def _dispatch(pos, fill, xn, n, n_blocks, tm):
    return pl.pallas_call(
        functools.partial(_dispatch_kernel, tm=tm, n=n, n_blocks=n_blocks),
        grid_spec=pltpu.PrefetchScalarGridSpec(
            num_scalar_prefetch=2,
            grid=(n // tm,),
            in_specs=[pl.BlockSpec((tm * ROW_CHUNKS, LANES), lambda i, *_: (i, 0))],
            out_specs=pl.BlockSpec(memory_space=pl.ANY),
            scratch_shapes=[pltpu.VMEM((MOE_ROWS * ROW_CHUNKS, LANES), F32), pltpu.SemaphoreType.DMA(()),
                            pltpu.SemaphoreType.DMA(())],
        ),
        out_shape=jax.ShapeDtypeStruct((n_blocks * MOE_ROWS * ROW_CHUNKS, LANES), F32),
        compiler_params=_params("arbitrary"),
        name="dispatch",
    )(pos, fill, xn)


def _expert_kernel(plan_ref, nused_ref, xs_ref, wgu_hbm, bgu_ref, wd_hbm, bd_ref, y_ref,
                   wgu_f32, wd_f32, wgu_bf, wd_bf, wsem, *, rows):
    i = pl.program_id(0)
    active = i < nused_ref[0]
    expert, next_expert, slot = plan_ref[0, i], plan_ref[1, i], plan_ref[2, i]
    first = jnp.logical_or(i == 0, expert != plan_ref[0, jnp.maximum(i - 1, 0)])

    def weight_copies(e, s):
        return (pltpu.make_async_copy(wgu_hbm.at[e], wgu_f32.at[s], wsem.at[0, s]),
                pltpu.make_async_copy(wd_hbm.at[e], wd_f32.at[s], wsem.at[1, s]))

    @pl.when(i == 0)
    def _():
        for c in weight_copies(expert, slot):
            c.start()

    @pl.when(jnp.logical_and(active, first))
    def _():
        for c in weight_copies(expert, slot):
            c.wait()
        wgu_bf[...] = wgu_f32[slot].astype(BF16)
        wd_bf[...] = wd_f32[slot].astype(BF16)

        @pl.when(next_expert < N_EXPERTS)
        def _():
            for c in weight_copies(next_expert, 1 - slot):
                c.start()

    @pl.when(active)
    def _():
        x = jnp.concatenate([xs_ref[pl.ds(j, rows, stride=ROW_CHUNKS), :] for j in range(ROW_CHUNKS)], axis=-1)
        zz = _dot(x.astype(BF16), wgu_bf[...]) + bgu_ref[...]
        g = jnp.minimum(zz[:, :D_FF], SWIGLU_LIMIT)
        u = jnp.clip(zz[:, D_FF:], -SWIGLU_LIMIT, SWIGLU_LIMIT)
        act = g * _sigmoid(SWIGLU_ALPHA * g) * (u + 1.0)
        y = _dot(act.astype(BF16), wd_bf[...]) + bd_ref[...]
        for j in range(ROW_CHUNKS):
            y_ref[pl.ds(j, rows, stride=ROW_CHUNKS), :] = y[:, j * LANES:(j + 1) * LANES]

    @pl.when(jnp.logical_not(active))
    def _():
        y_ref[...] = jnp.zeros_like(y_ref)


def _experts(plan, n_used, xs, wgu, bgu, wd, bd, rows):
    nb = plan.shape[1]
    return pl.pallas_call(
        functools.partial(_expert_kernel, rows=rows),
        grid_spec=pltpu.PrefetchScalarGridSpec(
            num_scalar_prefetch=2,
            grid=(nb,),
            in_specs=[
                pl.BlockSpec((rows * ROW_CHUNKS, LANES), lambda i, plan, nu: (i, 0)),
                pl.BlockSpec(memory_space=pl.ANY),
                pl.BlockSpec((None, 1, 2 * D_FF), lambda i, plan, nu: (plan[0, i], 0, 0)),
                pl.BlockSpec(memory_space=pl.ANY),
                pl.BlockSpec((None, 1, D_MODEL), lambda i, plan, nu: (plan[0, i], 0, 0)),
            ],
            out_specs=pl.BlockSpec((rows * ROW_CHUNKS, LANES), lambda i, plan, nu: (i, 0)),
            scratch_shapes=[pltpu.VMEM((2, D_MODEL, 2 * D_FF), F32), pltpu.VMEM((2, D_FF, D_MODEL), F32),
                            pltpu.VMEM((D_MODEL, 2 * D_FF), BF16), pltpu.VMEM((D_FF, D_MODEL), BF16),
                            pltpu.SemaphoreType.DMA((2, 2))],
        ),
        out_shape=jax.ShapeDtypeStruct(xs.shape, F32),
        compiler_params=_params("arbitrary"),
        name="experts",
    )(plan, n_used, xs, wgu, bgu, wd, bd)


def _combine_kernel(pos_ref, ye_ref, x1_ref, wo_ref, p_ref, wproj_ref, wgate_ref, gple_ref, gfin_ref,
                    y_ref, rows_a, rows_b, sem, *, tm, nt, n_all, tok0):
    i = pl.program_id(0)
    seg = tm * ROW_CHUNKS

    def start_row(tok, r, dst, sem_slot):
        for k in range(TOP_K):
            srow = pl.multiple_of(pos_ref[k * n_all + tok] * ROW_CHUNKS, ROW_CHUNKS)
            pltpu.make_async_copy(ye_ref.at[pl.ds(srow, ROW_CHUNKS)],
                                  dst.at[pl.ds((k * tm + r) * ROW_CHUNKS, ROW_CHUNKS)], sem_slot).start(priority=k % 2)

    def wait_rows(dst, sem_slot):
        for k in range(TOP_K):
            pltpu.make_async_copy(ye_ref.at[pl.ds(0, seg)], dst.at[pl.ds(k * seg, seg)], sem_slot).wait()

    def compute(half, src):
        rs = slice(half * tm, (half + 1) * tm)
        wo = wo_ref[rs, :]
        moe = jnp.zeros((tm, D_MODEL), F32)
        for k in range(TOP_K):
            rows = jnp.concatenate(
                [src[pl.ds(k * seg + j, tm, stride=ROW_CHUNKS), :] for j in range(ROW_CHUNKS)], axis=-1)
            moe = moe + rows * wo[:, k:k + 1]
        x2 = x1_ref[rs, :] + moe
        gate = _sigmoid(_dot(_rms(x2, gple_ref[...]).astype(BF16), wgate_ref[...]))
        x3 = x2 + _dot(p_ref[rs, :].astype(BF16), wproj_ref[...]) * gate
        y_ref[rs, :] = _rms(x3, gfin_ref[...])

    @pl.when(i == 0)
    def _():
        def issue(r, carry):
            start_row(tok0 + r, r, rows_a, sem.at[0])
            return carry
        lax.fori_loop(0, tm, issue, 0)

    wait_rows(rows_a, sem.at[0])
    base_b = tok0 + (2 * i + 1) * tm
    for r in range(tm):
        start_row(base_b + r, r, rows_b, sem.at[1])
    compute(0, rows_a)

    wait_rows(rows_b, sem.at[1])
    base_a = tok0 + jnp.minimum(2 * i + 2, nt - 1) * tm
    for r in range(tm):
        start_row(base_a + r, r, rows_a, sem.at[0])
    compute(1, rows_b)

    @pl.when(i == nt // 2 - 1)
    def _():
        wait_rows(rows_a, sem.at[0])


def _combine(pos_flat, ye, x1, wo, p, wproj, wgate, gple, gfin, tm, tok0):
    n = p.shape[0]
    n_all = x1.shape[0]
    nt = n // tm
    assert nt % 2 == 0 and tok0 % (2 * tm) == 0
    blk0 = tok0 // (2 * tm)
    tok = lambda width: pl.BlockSpec((2 * tm, width), lambda i, pos: (blk0 + i, 0))
    const = lambda *shape: pl.BlockSpec(shape, lambda i, pos: (0,) * len(shape))
    rows_buf = pltpu.VMEM((TOP_K * tm * ROW_CHUNKS, LANES), F32)
    return pl.pallas_call(
        functools.partial(_combine_kernel, tm=tm, nt=nt, n_all=n_all, tok0=tok0),
        grid_spec=pltpu.PrefetchScalarGridSpec(
            num_scalar_prefetch=1,
            grid=(nt // 2,),
            in_specs=[
                pl.BlockSpec(memory_space=pl.ANY),
                tok(D_MODEL), tok(LANES), pl.BlockSpec((2 * tm, PLE_DIM), lambda i, pos: (i, 0)),
                const(PLE_DIM, D_MODEL), const(D_MODEL, D_MODEL), const(1, D_MODEL), const(1, D_MODEL),
            ],
            out_specs=pl.BlockSpec((2 * tm, D_MODEL), lambda i, pos: (i, 0)),
            scratch_shapes=[rows_buf, rows_buf, pltpu.SemaphoreType.DMA((2,))],
        ),
        out_shape=jax.ShapeDtypeStruct((n, D_MODEL), F32),
        compiler_params=_params("arbitrary"),
        name="combine_ple",
    )(pos_flat, ye, x1, wo, p, wproj, wgate, gple, gfin)


def _tile(n, pref):
    t = pref
    while n % t:
        t //= 2
    return t


def _seg(a, i):
    return a[..., _OFF[i]:_OFF[i + 1]]


def kernel(x_prompt, x_sample, state_conv, state_mlstm_C, state_mlstm_n, state_mlstm_m, state_gla_S, p_prompt, p_sample, g_mix, w_in, b_in, conv_w, conv_b, g_mnorm, w_a2, b_a2, g_gnorm, w_pa, w_pb, w_out, g_ffn, w_router, b_router, w_gate_up, b_gate_up, w_down, b_down, g_ple, w_ple_proj, w_ple_gate, g_final):
    assert g_mix.shape[0] == 1, "single-layer kernel"
    bp, tp, _ = x_prompt.shape
    bs, ts, _ = x_sample.shape

    big = (0, 1, 2, 5, 6, 7, 8, 10, 11, 12)
    w_big = jnp.concatenate([_seg(w_in[0], i) for i in big], axis=-1).astype(BF16)
    b_big = jnp.concatenate([_seg(b_in[0], i) for i in big], axis=-1)[None, :]
    small = (3, 4, 9)
    n_small = sum(SPLITS[i] for i in small)
    w_small = jnp.pad(jnp.concatenate([_seg(w_in[0], i) for i in small], axis=-1),
                      ((0, 0), (0, LANES - n_small))).astype(BF16)
    b_small = jnp.pad(jnp.concatenate([_seg(b_in[0], i) for i in small], axis=-1), (0, LANES - n_small))[None, :]
    w2 = jnp.pad(w_a2[0], ((2 * M_HEADS, LANES - 2 * M_HEADS - G_RANK), (0, 0))).astype(BF16)
    wr = jnp.pad(w_router[0], ((0, 0), (0, LANES - N_EXPERTS))).astype(BF16)
    br = jnp.pad(b_router[0], (0, LANES - N_EXPERTS), constant_values=NEG_BIG)[None, :]
    wpa, wpb, wout = w_pa[0].astype(BF16), w_pb[0].astype(BF16), w_out[0].astype(BF16)
    wproj, wgate = w_ple_proj[0].astype(BF16), w_ple_gate[0].astype(BF16)
    row = lambda a: a.reshape(1, -1)

    def mix_path(x3d, conv0, c0, n0, m0, s0):
        batch, T, _ = x3d.shape
        x = x3d.reshape(batch * T, D_MODEL)
        z, zs = _inproj(x, row(g_mix[0]), w_big, b_big, w_small, b_small, _tile(batch * T, 2048))
        L = min(MIX_CHUNK, T)
        conv0p = jnp.pad(conv0, ((0, 0), (SUBLANES - (CONV_W - 1), 0), (0, 0)))
        m0b = jnp.broadcast_to(m0[:, :, None, None], (batch, M_HEADS, SUBLANES, LANES))
        ya, yb, tail, c1, n1, m1, st1 = _mixers(
            z, zs, conv0p, c0, n0[:, :, None, :], m0b, jnp.swapaxes(s0, -1, -2),
            conv_w[0], row(conv_b[0]), row(g_mnorm[0]), w2, row(b_a2[0]), row(g_gnorm[0]), batch, T, L)
        new_state = (tail[:, SUBLANES - (CONV_W - 1):, :], c1, n1[:, :, 0, :], m1[:, :, 0, 0],
                     jnp.swapaxes(st1, -1, -2))
        return x, z, ya, yb, new_state

    zero_state = (jnp.zeros((bp, CONV_W - 1, 2 * M_DIM), F32), jnp.zeros((bp, M_HEADS, M_HD, M_HD), F32),
                  jnp.zeros((bp, M_HEADS, M_HD), F32), jnp.zeros((bp, M_HEADS), F32),
                  jnp.zeros((bp, G_HEADS, G_DK, G_DV), F32))
    xp, zp, yap, ybp, st_p = mix_path(x_prompt, *zero_state)
    xs_, zs_, yas, ybs, st_s = mix_path(x_sample, state_conv[0], state_mlstm_C[0], state_mlstm_n[0],
                                        state_mlstm_m[0], state_gla_S[0])

    n_p, n_s = xp.shape[0], xs_.shape[0]
    n_all = n_p + n_s
    tm = _tile(n_s, 256)
    assert n_p % tm == 0
    x1, xn, eo, wo, cnt = _outproj((yap, ybp, zp, xp), (yas, ybs, zs_, xs_), wpa, wpb, wout, row(g_ffn[0]), wr, br, tm)

    counts = cnt[0, :N_EXPERTS].astype(jnp.int32)
    padded = (counts + MOE_ROWS - 1) // MOE_ROWS * MOE_ROWS
    pend = jnp.cumsum(padded)
    pstart = pend - padded
    nk = n_all * TOP_K
    n_blocks = (nk + N_EXPERTS * (MOE_ROWS - 1) + MOE_ROWS - 1) // MOE_ROWS
    starts = jnp.arange(n_blocks, dtype=jnp.int32) * MOE_ROWS
    block_e = jnp.minimum(jnp.sum((pend[None, :] <= starts[:, None]).astype(jnp.int32), axis=1), N_EXPERTS - 1)
    ids = jnp.arange(N_EXPERTS, dtype=jnp.int32)
    later = jnp.where((ids[None, :] > ids[:, None]) & (counts[None, :] > 0), ids[None, :], N_EXPERTS)
    next_e = jnp.min(later, axis=1)
    group = jnp.cumsum(jnp.concatenate([jnp.zeros((1,), jnp.int32),
                                        (block_e[1:] != block_e[:-1]).astype(jnp.int32)]))
    plan = jnp.stack([block_e, next_e[block_e], group % 2]).astype(jnp.int32)
    n_used = (pend[-1:] // MOE_ROWS).astype(jnp.int32)
    e, rank = eo[:TOP_K], eo[TOP_K:2 * TOP_K]
    onehot = e[..., None] == jnp.arange(N_EXPERTS, dtype=jnp.int32)
    pos = (jnp.sum(jnp.where(onehot, pstart, 0), axis=-1) + rank).reshape(-1)
    fill = jnp.concatenate([pstart + counts, padded - counts, n_used]).astype(jnp.int32)

    xs_sorted = _dispatch(pos, fill, xn, n_all, n_blocks, tm)
    ye = _experts(plan, n_used, xs_sorted, w_gate_up[0], b_gate_up[0][:, None, :], w_down[0],
                  b_down[0][:, None, :], MOE_ROWS)
    fin = lambda p, tok0, tmc: _combine(pos, ye, x1, wo, p.reshape(-1, PLE_DIM), wproj, wgate,
                                        row(g_ple[0]), row(g_final), tmc, tok0)
    y_p = fin(p_prompt[0], 0, tm).reshape(bp, tp, D_MODEL)
    y_s = fin(p_sample[0], n_p, _tile(n_s // 2, tm)).reshape(bs, ts, D_MODEL)

    lead = lambda a: a[None]
    return (y_p, y_s) + tuple(lead(a) for a in st_p) + tuple(lead(a) for a in st_s)
```

```python
import functools

import numpy as np
import jax
import jax.numpy as jnp
from jax import lax
from jax.experimental import pallas as pl
from jax.experimental.pallas import tpu as pltpu

F32 = jnp.float32
BF16 = jnp.bfloat16

D_MODEL = 1024
EPS = 1e-6
M_HEADS = 4
M_DIM = D_MODEL
M_HD = M_DIM // M_HEADS
CONV_W = 4
G_HEADS = 4
G_KDIM = D_MODEL // 2
G_VDIM = D_MODEL
G_DK = G_KDIM // G_HEADS
G_DV = G_VDIM // G_HEADS
G_RANK = 16
G_TAU = 16.0
N_EXPERTS = 32
TOP_K = 4
D_FF = D_MODEL
SWIGLU_LIMIT = 7.0
SWIGLU_ALPHA = 1.702
PLE_DIM = 256
SPLITS = (M_DIM, M_DIM, M_DIM, M_HEADS, M_HEADS, M_DIM, G_KDIM, G_KDIM, G_VDIM, G_RANK, G_VDIM, D_MODEL, D_MODEL)
_OFF = np.concatenate([[0], np.cumsum(SPLITS)]).tolist()

LANES = 128
SUBLANES = 8
ROW_CHUNKS = D_MODEL // LANES
VMEM_LIMIT = 56 * 1024 * 1024
MIX_CHUNK = 256
MOE_ROWS = 512
NEG_BIG = -1e30


def _sigmoid(x):
    return 0.5 * jnp.tanh(0.5 * x) + 0.5


def _log_sigmoid(x):
    return jnp.minimum(x, 0.0) - jnp.log(1.0 + jnp.exp(-jnp.abs(x)))


def _rms(x, g):
    return x * lax.rsqrt(jnp.mean(x * x, axis=-1, keepdims=True) + EPS) * g


def _dot(a, b):
    return jnp.dot(a, b, preferred_element_type=F32)


def _dot_nt(a, b):
    return lax.dot_general(a, b, (((1,), (1,)), ((), ())), preferred_element_type=F32)


def _dot_tn(a, b):
    return lax.dot_general(a, b, (((0,), (0,)), ((), ())), preferred_element_type=F32)


def _params(*sem):
    return pltpu.CompilerParams(dimension_semantics=sem, vmem_limit_bytes=VMEM_LIMIT)


def _inproj_kernel(x_ref, g_ref, w_ref, b_ref, ws_ref, bs_ref, z_ref, zs_ref, h_scr):
    @pl.when(pl.program_id(1) == 0)
    def _():
        hb = _rms(x_ref[...], g_ref[...]).astype(BF16)
        h_scr[...] = hb
        zs_ref[...] = _dot(hb, ws_ref[...]) + bs_ref[...]

    z_ref[...] = (_dot(h_scr[...], w_ref[...]) + b_ref[...]).astype(BF16)


def _inproj(x, g, w_big, b_big, w_small, b_small, tm, tn=1024):
    n = x.shape[0]
    cols = w_big.shape[1]
    return pl.pallas_call(
        _inproj_kernel,
        grid=(n // tm, cols // tn),
        in_specs=[
            pl.BlockSpec((tm, D_MODEL), lambda i, j: (i, 0)),
            pl.BlockSpec((1, D_MODEL), lambda i, j: (0, 0)),
            pl.BlockSpec((D_MODEL, tn), lambda i, j: (0, j)),
            pl.BlockSpec((1, tn), lambda i, j: (0, j)),
            pl.BlockSpec((D_MODEL, LANES), lambda i, j: (0, 0)),
            pl.BlockSpec((1, LANES), lambda i, j: (0, 0)),
        ],
        out_specs=[
            pl.BlockSpec((tm, tn), lambda i, j: (i, j)),
            pl.BlockSpec((tm, LANES), lambda i, j: (i, 0)),
        ],
        out_shape=[jax.ShapeDtypeStruct((n, cols), BF16), jax.ShapeDtypeStruct((n, LANES), F32)],
        scratch_shapes=[pltpu.VMEM((tm, D_MODEL), BF16)],
        compiler_params=_params("parallel", "arbitrary"),
        name="inproj",
    )(x, g, w_big, b_big, w_small, b_small)


def _level_matrix(L):
    t = np.arange(L)[:, None]
    s = np.arange(L)[None, :]
    x = np.maximum(t ^ s, 1)
    lv = np.floor(np.log2(x)).astype(np.int32)
    lv = np.where(t == s, -1, lv)
    return np.where(s > t, -2, lv).astype(np.int32)


def _bcast_row(x, period, r):
    L, W = x.shape
    x3 = x.reshape(L // period, period, W)
    return jnp.broadcast_to(x3[:, r:r + 1, :], x3.shape).reshape(L, W)


def _decay_products(ela, L):
    W = ela.shape[1]
    row = lax.broadcasted_iota(jnp.int32, (L, W), 0)
    fwd = [ela]
    rex = [jnp.ones_like(ela)]
    b = 1
    while b < L:
        per = 2 * b
        res = row & (per - 1)
        f, r = fwd[-1], rex[-1]
        if per < SUBLANES:
            mulf = jnp.ones_like(ela)
            mulr = jnp.ones_like(ela)
            for j in range(b, per):
                mulf = jnp.where(res == j, pltpu.roll(f, j - (b - 1), 0), mulf)
            for j in range(0, b):
                mulr = jnp.where(res == j, pltpu.roll(f, L - (per - 1 - j), 0), mulr)
        else:
            mulf = jnp.where(res >= b, _bcast_row(f, per, b - 1), 1.0)
            mulr = jnp.where(res < b, _bcast_row(f, per, per - 1), 1.0)
        fwd.append(f * mulf)
        rex.append(r * mulr)
        b = per
    return fwd, rex


def _mixer_kernel(zqk_ref, zv_ref, zo_ref, zgq_ref, zgk_ref, zgv_ref, zgg_ref, zs_ref,
                  conv0_ref, c0_ref, n0_ref, m0_ref, s0_ref,
                  cw_ref, cb_ref, gm_ref, w2_ref, ba2_ref, gg_ref, lv_ref, shift_ref,
                  ya_ref, yb_ref, tail_ref, c_ref, n_ref, m_ref, st_ref, *, L):
    @pl.when(pl.program_id(1) == 0)
    def _():
        tail_ref[...] = conv0_ref[...]
        c_ref[...] = c0_ref[...]
        n_ref[...] = n0_ref[...]
        m_ref[...] = m0_ref[...]
        st_ref[...] = s0_ref[...]

    lv = lv_ref[...]
    causal = lv >= -1

    u_b = zqk_ref[...]
    u = u_b.astype(F32)
    tail = tail_ref[0]
    row8 = lax.broadcasted_iota(jnp.int32, (SUBLANES, 2 * M_DIM), 0)
    acc = cb_ref[...] + cw_ref[CONV_W - 1:CONV_W, :] * u
    for w in range(CONV_W - 1):
        delay = CONV_W - 1 - w
        shifted = _dot(shift_ref[w], u_b)
        head = jnp.where(row8 < delay, pltpu.roll(tail, delay, 0), 0.0)
        shifted = jnp.concatenate([shifted[0:SUBLANES] + head, shifted[SUBLANES:]], axis=0)
        acc = acc + cw_ref[w:w + 1, :] * shifted
    tail_ref[0] = u[L - SUBLANES:L, :]
    qk = acc * _sigmoid(acc)

    zs = zs_ref[...]
    row = lax.broadcasted_iota(jnp.int32, (L, LANES), 0)
    col = lax.broadcasted_iota(jnp.int32, (L, LANES), 1)
    bcs = _log_sigmoid(zs)
    sh = 1
    while sh < L:
        bcs = bcs + jnp.where(row >= sh, pltpu.roll(bcs, sh, 0), 0.0)
        sh *= 2
    gates = jnp.where(col < M_HEADS, zs, bcs)
    if L < LANES:
        gates = jnp.concatenate([gates, jnp.zeros((LANES - L, LANES), F32)], axis=0)
    gt = gates.T[:, :L]

    for h in range(M_HEADS):
        hs = slice(h * M_HD, (h + 1) * M_HD)
        ig_col = zs[:, h:h + 1]
        b_col = bcs[:, M_HEADS + h:M_HEADS + h + 1]
        ig_row = gt[h:h + 1, :]
        b_row = gt[M_HEADS + h:M_HEADS + h + 1, :]
        m_prev = m_ref[0, h][0:1, 0:1]
        logd = jnp.where(causal, b_col - b_row + ig_row, -jnp.inf)
        inter = b_col + m_prev
        m_t = jnp.maximum(inter, jnp.max(logd, axis=-1, keepdims=True))
        q_f = qk[:, hs]
        k_f = qk[:, M_DIM + h * M_HD:M_DIM + (h + 1) * M_HD] * (M_HD ** -0.5)
        q_b = q_f.astype(BF16)
        v_b = zv_ref[:, hs]
        s = _dot_nt(q_b, k_f.astype(BF16)) * jnp.exp(logd - m_t)
        w_int = jnp.exp(inter - m_t)
        c_old = c_ref[0, h]
        n_old = n_ref[0, h]
        num = _dot(s.astype(BF16), v_b) + w_int * _dot(q_b, c_old.astype(BF16))
        den = jnp.sum(s, axis=-1, keepdims=True) + w_int * jnp.sum(q_f * n_old, axis=-1, keepdims=True)
        hh = num * (1.0 / jnp.maximum(jnp.abs(den), jnp.exp(-m_t)))
        m_new = m_t[L - 1:L, :]
        b_last = b_col[L - 1:L, :]
        w_k = jnp.exp(b_last - b_col + ig_col - m_new)
        dec = jnp.exp(b_last + m_prev - m_new)
        k_w = k_f * w_k
        c_ref[0, h] = dec * c_old + _dot_tn(k_w.astype(BF16), v_b)
        n_ref[0, h] = dec * n_old + jnp.sum(k_w, axis=0, keepdims=True)
        m_ref[0, h] = jnp.broadcast_to(m_new, (SUBLANES, LANES))
        hn = _rms(hh, gm_ref[:, hs])
        ya_ref[:, hs] = (hn * _sigmoid(zo_ref[:, hs].astype(F32))).astype(BF16)

    la = _log_sigmoid(_dot(zs.astype(BF16), w2_ref[...]) + ba2_ref[...]) * (1.0 / G_TAU)
    fwd, rex = _decay_products(jnp.exp(la), L)
    nlev = len(fwd) - 1
    qg = zgq_ref[...].astype(F32) * (G_DK ** -0.5)
    kg = zgk_ref[...].astype(F32)
    q_d = qg.astype(BF16)
    k_d = zgk_ref[...]
    qs = [(qg * fwd[j]).astype(BF16) for j in range(nlev)]
    ks = [k_d] + [(kg * rex[j]).astype(BF16) for j in range(1, nlev)]
    q_in = (qg * fwd[nlev]).astype(BF16)
    k_out = (kg * rex[nlev]).astype(BF16)
    dec_all = fwd[nlev][L - 1:L, :]
    for h in range(G_HEADS):
        ks_ = slice(h * G_DK, (h + 1) * G_DK)
        vs_ = slice(h * G_DV, (h + 1) * G_DV)
        att = jnp.where(lv == -1, _dot_nt(q_d[:, ks_], k_d[:, ks_]), 0.0)
        for j in range(nlev):
            att = jnp.where(lv == j, _dot_nt(qs[j][:, ks_], ks[j][:, ks_]), att)
        v_b = zgv_ref[:, vs_]
        st_old = st_ref[0, h]
        o = _dot(att.astype(BF16), v_b) + _dot_nt(q_in[:, ks_], st_old.astype(BF16))
        st_ref[0, h] = st_old * dec_all[:, ks_] + _dot_tn(v_b, k_out[:, ks_])
        on = _rms(o, gg_ref[:, vs_])
        g = zgg_ref[:, vs_].astype(F32)
        yb_ref[:, vs_] = (on * (g * _sigmoid(g))).astype(BF16)


def _mixers(z, zs, conv0, c0, n0, m0, st0, cw, cb, gm, w2, ba2, gg, batch, T, L):
    nc = T // L
    lv = jnp.asarray(_level_matrix(L))
    shift = jnp.asarray(np.stack([np.eye(L, k=-(CONV_W - 1 - w)) for w in range(CONV_W - 1)]), BF16)
    row_blk = lambda width, cidx: pl.BlockSpec((L, width), lambda b, c: (b * nc + c, cidx))
    state = lambda *shape: pl.BlockSpec((1,) + shape, lambda b, c: (b,) + (0,) * len(shape))
    const = lambda *shape: pl.BlockSpec(shape, lambda b, c: (0,) * len(shape))
    n_tok = batch * T
    return pl.pallas_call(
        functools.partial(_mixer_kernel, L=L),
        grid=(batch, nc),
        in_specs=[
            row_blk(2 * M_DIM, 0),
            row_blk(M_DIM, 2),
            row_blk(M_DIM, 3),
            row_blk(G_KDIM, 8),
            row_blk(G_KDIM, 9),
            row_blk(G_VDIM, 5),
            row_blk(G_VDIM, 6),
            pl.BlockSpec((L, LANES), lambda b, c: (b * nc + c, 0)),
            state(SUBLANES, 2 * M_DIM),
            state(M_HEADS, M_HD, M_HD),
            state(M_HEADS, 1, M_HD),
            state(M_HEADS, SUBLANES, LANES),
            state(G_HEADS, G_DV, G_DK),
            const(CONV_W, 2 * M_DIM),
            const(1, 2 * M_DIM),
            const(1, M_DIM),
            const(LANES, G_KDIM),
            const(1, G_KDIM),
            const(1, G_VDIM),
            const(L, L),
            const(CONV_W - 1, L, L),
        ],
        out_specs=[
            pl.BlockSpec((L, M_DIM), lambda b, c: (b * nc + c, 0)),
            pl.BlockSpec((L, G_VDIM), lambda b, c: (b * nc + c, 0)),
            state(SUBLANES, 2 * M_DIM),
            state(M_HEADS, M_HD, M_HD),
            state(M_HEADS, 1, M_HD),
            state(M_HEADS, SUBLANES, LANES),
            state(G_HEADS, G_DV, G_DK),
        ],
        out_shape=[
            jax.ShapeDtypeStruct((n_tok, M_DIM), BF16),
            jax.ShapeDtypeStruct((n_tok, G_VDIM), BF16),
            jax.ShapeDtypeStruct((batch, SUBLANES, 2 * M_DIM), F32),
            jax.ShapeDtypeStruct((batch, M_HEADS, M_HD, M_HD), F32),
            jax.ShapeDtypeStruct((batch, M_HEADS, 1, M_HD), F32),
            jax.ShapeDtypeStruct((batch, M_HEADS, SUBLANES, LANES), F32),
            jax.ShapeDtypeStruct((batch, G_HEADS, G_DV, G_DK), F32),
        ],
        compiler_params=_params("parallel", "arbitrary"),
        name="mixers",
    )(z, z, z, z, z, z, z, zs, conv0, c0, n0, m0, st0, cw, cb, gm, w2, ba2, gg, lv, shift)


def _outproj_kernel(ya_p, yb_p, ua_p, ub_p, x_p, ya_s, yb_s, ua_s, ub_s, x_s,
                    wpa_ref, wpb_ref, wout_ref, gffn_ref, wr_ref, br_ref, tril_ref,
                    x1p_ref, x1s_ref, xn_ref, eop_ref, eos_ref, wop_ref, wos_ref, cnt_ref, *, tm, steps_p):
    i = pl.program_id(0)

    @pl.when(i == 0)
    def _():
        cnt_ref[...] = jnp.zeros_like(cnt_ref)

    def chains(ins, parts, x1_ref, eo_ref, wo_ref):
        ya_ref, yb_ref, ua_ref, ub_ref, x_ref = ins
        a = [_dot(ya_ref[rs, :], wpa_ref[...]) for rs in parts]
        b = [_dot(yb_ref[rs, :], wpb_ref[...]) for rs in parts]
        merged = [_sigmoid(ua_ref[rs, :].astype(F32)) * a_ + _sigmoid(ub_ref[rs, :].astype(F32)) * b_
                  for rs, a_, b_ in zip(parts, a, b)]
        x1 = [x_ref[rs, :] + _dot(m.astype(BF16), wout_ref[...]) for rs, m in zip(parts, merged)]
        xn = []
        for h, (rs, x1_) in enumerate(zip(parts, x1)):
            x1_ref[rs, :] = x1_
            xn_ = _rms(x1_, gffn_ref[...])
            for j in range(ROW_CHUNKS):
                xn_ref[pl.ds(h * tm * ROW_CHUNKS + j, tm, stride=ROW_CHUNKS), :] = xn_[:, j * LANES:(j + 1) * LANES]
            xn.append(xn_)

        ls = [_dot(xn_.astype(BF16), wr_ref[...]) + br_ref[...] for xn_ in xn]
        lane = lax.broadcasted_iota(jnp.int32, (tm, LANES), 1).astype(F32)
        vals = [[] for _ in parts]
        idxs = [[] for _ in parts]
        for _ in range(TOP_K):
            for h in range(len(parts)):
                mx = jnp.max(ls[h], axis=-1, keepdims=True)
                ix = jnp.min(jnp.where(ls[h] == mx, lane, float(LANES)), axis=-1, keepdims=True)
                vals[h].append(mx)
                idxs[h].append(ix)
                ls[h] = jnp.where(lane == ix, -jnp.inf, ls[h])

        carry = cnt_ref[0:1, :]
        for h, rs in enumerate(parts):
            ev = [jnp.exp(v - vals[h][0]) for v in vals[h]]
            den = ev[0] + ev[1] + ev[2] + ev[3]
            cnt = jnp.zeros((tm, LANES), F32)
            for ix in idxs[h]:
                cnt = cnt + jnp.where(lane == ix, 1.0, 0.0)
            pre = _dot(tril_ref[...], cnt.astype(BF16)) + carry
            carry = carry + jnp.sum(cnt, axis=0, keepdims=True)
            eo = jnp.zeros((tm, LANES), F32)
            wo = jnp.zeros((tm, LANES), F32)
            for k in range(TOP_K):
                rank = jnp.sum(jnp.where(lane == idxs[h][k], pre, 0.0), axis=-1, keepdims=True)
                eo = jnp.where(lane == float(k), idxs[h][k], eo)
                eo = jnp.where(lane == float(TOP_K + k), rank, eo)
                wo = jnp.where(lane == float(k), ev[k] / den, wo)
            eo_ref[:, rs] = eo.T[0:SUBLANES, :].astype(jnp.int32)
            wo_ref[rs, :] = wo
        cnt_ref[...] = jnp.broadcast_to(carry, cnt_ref.shape)

    prompt = (ya_p, yb_p, ua_p, ub_p, x_p)
    sample = (ya_s, yb_s, ua_s, ub_s, x_s)

    @pl.when(i < steps_p)
    def _():
        chains(prompt, [slice(0, tm), slice(tm, 2 * tm)], x1p_ref, eop_ref, wop_ref)

    @pl.when(i == steps_p)
    def _():
        chains(sample, [slice(0, tm)], x1s_ref, eos_ref, wos_ref)
        xn_ref[pl.ds(tm * ROW_CHUNKS, tm * ROW_CHUNKS), :] = jnp.zeros((tm * ROW_CHUNKS, LANES), F32)


def _outproj(prompt, sample, wpa, wpb, wout, gffn, wr, br, tm):
    n_p, n_s = prompt[3].shape[0], sample[3].shape[0]
    assert n_s == tm and n_p % (2 * tm) == 0
    steps_p = n_p // (2 * tm)
    tril = jnp.asarray(np.tril(np.ones((tm, tm), np.float32), -1), BF16)
    pidx = lambda i: jnp.minimum(i, steps_p - 1)
    src = lambda rows, idx: [pl.BlockSpec((rows, M_DIM), lambda i: (idx(i), 0)),
                             pl.BlockSpec((rows, G_VDIM), lambda i: (idx(i), 0)),
                             pl.BlockSpec((rows, D_MODEL), lambda i: (idx(i), 7)),
                             pl.BlockSpec((rows, D_MODEL), lambda i: (idx(i), 8)),
                             pl.BlockSpec((rows, D_MODEL), lambda i: (idx(i), 0))]
    const = lambda *shape: pl.BlockSpec(shape, lambda i: (0,) * len(shape))
    ya_p, yb_p, z_p, x_p = prompt
    ya_s, yb_s, z_s, x_s = sample
    return pl.pallas_call(
        functools.partial(_outproj_kernel, tm=tm, steps_p=steps_p),
        grid=(steps_p + 1,),
        in_specs=src(2 * tm, pidx) + src(tm, lambda i: 0) + [
            const(M_DIM, D_MODEL), const(G_VDIM, D_MODEL), const(D_MODEL, D_MODEL), const(1, D_MODEL),
            const(D_MODEL, LANES), const(1, LANES), const(tm, tm),
        ],
        out_specs=[
            pl.BlockSpec((2 * tm, D_MODEL), lambda i: (pidx(i), 0)),
            const(tm, D_MODEL),
            pl.BlockSpec((2 * tm * ROW_CHUNKS, LANES), lambda i: (i, 0)),
            pl.BlockSpec((SUBLANES, 2 * tm), lambda i: (0, pidx(i))),
            const(SUBLANES, tm),
            pl.BlockSpec((2 * tm, LANES), lambda i: (pidx(i), 0)),
            const(tm, LANES),
            const(SUBLANES, LANES),
        ],
        out_shape=[
            jax.ShapeDtypeStruct((n_p, D_MODEL), F32),
            jax.ShapeDtypeStruct((n_s, D_MODEL), F32),
            jax.ShapeDtypeStruct(((n_p + 2 * tm) * ROW_CHUNKS, LANES), F32),
            jax.ShapeDtypeStruct((SUBLANES, n_p), jnp.int32),
            jax.ShapeDtypeStruct((SUBLANES, n_s), jnp.int32),
            jax.ShapeDtypeStruct((n_p, LANES), F32),
            jax.ShapeDtypeStruct((n_s, LANES), F32),
            jax.ShapeDtypeStruct((SUBLANES, LANES), F32),
        ],
        compiler_params=_params("arbitrary"),
        name="outproj_router",
    )(ya_p, yb_p, z_p, z_p, x_p, ya_s, yb_s, z_s, z_s, x_s, wpa, wpb, wout, gffn, wr, br, tril)


def _row_copy(src, src_row, dst, dst_row, sem):
    return pltpu.make_async_copy(
        src.at[pl.ds(pl.multiple_of(src_row * ROW_CHUNKS, ROW_CHUNKS), ROW_CHUNKS)],
        dst.at[pl.ds(pl.multiple_of(dst_row * ROW_CHUNKS, ROW_CHUNKS), ROW_CHUNKS)], sem)


def _dispatch_kernel(pos_ref, fill_ref, xn_ref, xs_ref, zeros_ref, zsem, sem, *, tm, n, n_blocks):
    i = pl.program_id(0)
    base = i * tm

    def issue(r, carry):
        for k in range(TOP_K):
            _row_copy(xn_ref, r, xs_ref, pos_ref[k * n + base + r], sem).start(priority=k % 2)
        return carry

    lax.fori_loop(0, tm, issue, 0)
    for _ in range(TOP_K):
        pltpu.make_async_copy(xn_ref, xs_ref.at[pl.ds(0, tm * ROW_CHUNKS)], sem).wait()

    @pl.when(i == 0)
    def _():
        zeros_ref[...] = jnp.zeros_like(zeros_ref)
        bits = [1 << b for b in range(MOE_ROWS.bit_length() - 1)]

        def zero_copy(off, nrows):
            return pltpu.make_async_copy(
                zeros_ref.at[pl.ds(0, nrows * ROW_CHUNKS)],
                xs_ref.at[pl.ds(pl.multiple_of(off * ROW_CHUNKS, ROW_CHUNKS), nrows * ROW_CHUNKS)], zsem)

        def expert(e, carry):
            npad = fill_ref[N_EXPERTS + e]
            for wait in (False, True):
                off = fill_ref[e]
                for bit in bits:
                    @pl.when((npad & bit) != 0)
                    def _():
                        zero_copy(off, bit).wait() if wait else zero_copy(off, bit).start()
                    off = off + (npad & bit)
            return carry

        lax.fori_loop(0, N_EXPERTS, expert, 0)

        def block(b, carry):
            zero_copy(b * MOE_ROWS, MOE_ROWS).start()
            zero_copy(b * MOE_ROWS, MOE_ROWS).wait()
            return carry

        lax.fori_loop(fill_ref[2 * N_EXPERTS], n_blocks, block, 0)


def _dispatch(pos, fill, xn, n, n_blocks, tm):
    return pl.pallas_call(
        functools.partial(_dispatch_kernel, tm=tm, n=n, n_blocks=n_blocks),
        grid_spec=pltpu.PrefetchScalarGridSpec(
            num_scalar_prefetch=2,
            grid=(n // tm,),
            in_specs=[pl.BlockSpec((tm * ROW_CHUNKS, LANES), lambda i, *_: (i, 0))],
            out_specs=pl.BlockSpec(memory_space=pl.ANY),
            scratch_shapes=[pltpu.VMEM((MOE_ROWS * ROW_CHUNKS, LANES), F32), pltpu.SemaphoreType.DMA(()),
                            pltpu.SemaphoreType.DMA(())],
        ),
        out_shape=jax.ShapeDtypeStruct((n_blocks * MOE_ROWS * ROW_CHUNKS, LANES), F32),
        compiler_params=_params("arbitrary"),
        name="dispatch",
    )(pos, fill, xn)


def _expert_kernel(plan_ref, nused_ref, xs_ref, wgu_hbm, bgu_ref, wd_hbm, bd_ref, y_ref,
                   wgu_f32, wd_f32, wgu_bf, wd_bf, wsem, *, rows):
    i = pl.program_id(0)
    active = i < nused_ref[0]
    expert, next_expert, slot = plan_ref[0, i], plan_ref[1, i], plan_ref[2, i]
    first = jnp.logical_or(i == 0, expert != plan_ref[0, jnp.maximum(i - 1, 0)])

    def weight_copies(e, s):
        return (pltpu.make_async_copy(wgu_hbm.at[e], wgu_f32.at[s], wsem.at[0, s]),
                pltpu.make_async_copy(wd_hbm.at[e], wd_f32.at[s], wsem.at[1, s]))

    @pl.when(i == 0)
    def _():
        for c in weight_copies(expert, slot):
            c.start()

    @pl.when(jnp.logical_and(active, first))
    def _():
        for c in weight_copies(expert, slot):
            c.wait()
        wgu_bf[...] = wgu_f32[slot].astype(BF16)
        wd_bf[...] = wd_f32[slot].astype(BF16)

        @pl.when(next_expert < N_EXPERTS)
        def _():
            for c in weight_copies(next_expert, 1 - slot):
                c.start()

    @pl.when(active)
    def _():
        x = jnp.concatenate([xs_ref[pl.ds(j, rows, stride=ROW_CHUNKS), :] for j in range(ROW_CHUNKS)], axis=-1)
        zz = _dot(x.astype(BF16), wgu_bf[...]) + bgu_ref[...]
        g = jnp.minimum(zz[:, :D_FF], SWIGLU_LIMIT)
        u = jnp.clip(zz[:, D_FF:], -SWIGLU_LIMIT, SWIGLU_LIMIT)
        act = g * _sigmoid(SWIGLU_ALPHA * g) * (u + 1.0)
        y = _dot(act.astype(BF16), wd_bf[...]) + bd_ref[...]
        for j in range(ROW_CHUNKS):
            y_ref[pl.ds(j, rows, stride=ROW_CHUNKS), :] = y[:, j * LANES:(j + 1) * LANES]

    @pl.when(jnp.logical_not(active))
    def _():
        y_ref[...] = jnp.zeros_like(y_ref)


def _experts(plan, n_used, xs, wgu, bgu, wd, bd, rows):
    nb = plan.shape[1]
    return pl.pallas_call(
        functools.partial(_expert_kernel, rows=rows),
        grid_spec=pltpu.PrefetchScalarGridSpec(
            num_scalar_prefetch=2,
            grid=(nb,),
            in_specs=[
                pl.BlockSpec((rows * ROW_CHUNKS, LANES), lambda i, plan, nu: (i, 0)),
                pl.BlockSpec(memory_space=pl.ANY),
                pl.BlockSpec((None, 1, 2 * D_FF), lambda i, plan, nu: (plan[0, i], 0, 0)),
                pl.BlockSpec(memory_space=pl.ANY),
                pl.BlockSpec((None, 1, D_MODEL), lambda i, plan, nu: (plan[0, i], 0, 0)),
            ],
            out_specs=pl.BlockSpec((rows * ROW_CHUNKS, LANES), lambda i, plan, nu: (i, 0)),
            scratch_shapes=[pltpu.VMEM((2, D_MODEL, 2 * D_FF), F32), pltpu.VMEM((2, D_FF, D_MODEL), F32),
                            pltpu.VMEM((D_MODEL, 2 * D_FF), BF16), pltpu.VMEM((D_FF, D_MODEL), BF16),
                            pltpu.SemaphoreType.DMA((2, 2))],
        ),
        out_shape=jax.ShapeDtypeStruct(xs.shape, F32),
        compiler_params=_params("arbitrary"),
        name="experts",
    )(plan, n_used, xs, wgu, bgu, wd, bd)


def _combine_kernel(pos_ref, ye_ref, x1_ref, wo_ref, p_ref, wproj_ref, wgate_ref, gple_ref, gfin_ref,
                    y_ref, rows_a, rows_b, sem, *, tm, nt, n_all, tok0):
    i = pl.program_id(0)
    seg = tm * ROW_CHUNKS

    def start_row(tok, r, dst, sem_slot):
        for k in range(TOP_K):
            srow = pl.multiple_of(pos_ref[k * n_all + tok] * ROW_CHUNKS, ROW_CHUNKS)
            pltpu.make_async_copy(ye_ref.at[pl.ds(srow, ROW_CHUNKS)],
                                  dst.at[pl.ds((k * tm + r) * ROW_CHUNKS, ROW_CHUNKS)], sem_slot).start(priority=k % 2)

    def wait_rows(dst, sem_slot):
        for k in range(TOP_K):
            pltpu.make_async_copy(ye_ref.at[pl.ds(0, seg)], dst.at[pl.ds(k * seg, seg)], sem_slot).wait()

    def compute(half, src):
        rs = slice(half * tm, (half + 1) * tm)
        wo = wo_ref[rs, :]
        moe = jnp.zeros((tm, D_MODEL), F32)
        for k in range(TOP_K):
            rows = jnp.concatenate(
                [src[pl.ds(k * seg + j, tm, stride=ROW_CHUNKS), :] for j in range(ROW_CHUNKS)], axis=-1)
            moe = moe + rows * wo[:, k:k + 1]
        x2 = x1_ref[rs, :] + moe
        gate = _sigmoid(_dot(_rms(x2, gple_ref[...]).astype(BF16), wgate_ref[...]))
        x3 = x2 + _dot(p_ref[rs, :].astype(BF16), wproj_ref[...]) * gate
        y_ref[rs, :] = _rms(x3, gfin_ref[...])

    @pl.when(i == 0)
    def _():
        def issue(r, carry):
            start_row(tok0 + r, r, rows_a, sem.at[0])
            return carry
        lax.fori_loop(0, tm, issue, 0)

    wait_rows(rows_a, sem.at[0])
    base_b = tok0 + (2 * i + 1) * tm
    for r in range(tm):
        start_row(base_b + r, r, rows_b, sem.at[1])
    compute(0, rows_a)

    wait_rows(rows_b, sem.at[1])
    base_a = tok0 + jnp.minimum(2 * i + 2, nt - 1) * tm
    for r in range(tm):
        start_row(base_a + r, r, rows_a, sem.at[0])
    compute(1, rows_b)

    @pl.when(i == nt // 2 - 1)
    def _():
        wait_rows(rows_a, sem.at[0])


def _combine(pos_flat, ye, x1, wo, p, wproj, wgate, gple, gfin, tm, tok0, n_all):
    n = p.shape[0]
    nt = n // tm
    assert nt % 2 == 0
    tok = lambda width: pl.BlockSpec((2 * tm, width), lambda i, pos: (i, 0))
    const = lambda *shape: pl.BlockSpec(shape, lambda i, pos: (0,) * len(shape))
    rows_buf = pltpu.VMEM((TOP_K * tm * ROW_CHUNKS, LANES), F32)
    return pl.pallas_call(
        functools.partial(_combine_kernel, tm=tm, nt=nt, n_all=n_all, tok0=tok0),
        grid_spec=pltpu.PrefetchScalarGridSpec(
            num_scalar_prefetch=1,
            grid=(nt // 2,),
            in_specs=[
                pl.BlockSpec(memory_space=pl.ANY),
                tok(D_MODEL), tok(LANES), tok(PLE_DIM),
                const(PLE_DIM, D_MODEL), const(D_MODEL, D_MODEL), const(1, D_MODEL), const(1, D_MODEL),
            ],
            out_specs=tok(D_MODEL),
            scratch_shapes=[rows_buf, rows_buf, pltpu.SemaphoreType.DMA((2,))],
        ),
        out_shape=jax.ShapeDtypeStruct((n, D_MODEL), F32),
        compiler_params=_params("arbitrary"),
        name="combine_ple",
    )(pos_flat, ye, x1, wo, p, wproj, wgate, gple, gfin)


def _tile(n, pref):
    t = pref
    while n % t:
        t //= 2
    return t


def _seg(a, i):
    return a[..., _OFF[i]:_OFF[i + 1]]


def kernel(x_prompt, x_sample, state_conv, state_mlstm_C, state_mlstm_n, state_mlstm_m, state_gla_S, p_prompt, p_sample, g_mix, w_in, b_in, conv_w, conv_b, g_mnorm, w_a2, b_a2, g_gnorm, w_pa, w_pb, w_out, g_ffn, w_router, b_router, w_gate_up, b_gate_up, w_down, b_down, g_ple, w_ple_proj, w_ple_gate, g_final):
    assert g_mix.shape[0] == 1, "single-layer kernel"
    bp, tp, _ = x_prompt.shape
    bs, ts, _ = x_sample.shape

    big = (0, 1, 2, 5, 6, 7, 8, 10, 11, 12)
    w_big = jnp.concatenate([_seg(w_in[0], i) for i in big], axis=-1).astype(BF16)
    b_big = jnp.concatenate([_seg(b_in[0], i) for i in big], axis=-1)[None, :]
    small = (3, 4, 9)
    n_small = sum(SPLITS[i] for i in small)
    w_small = jnp.pad(jnp.concatenate([_seg(w_in[0], i) for i in small], axis=-1),
                      ((0, 0), (0, LANES - n_small))).astype(BF16)
    b_small = jnp.pad(jnp.concatenate([_seg(b_in[0], i) for i in small], axis=-1), (0, LANES - n_small))[None, :]
    w2 = jnp.pad(w_a2[0], ((2 * M_HEADS, LANES - 2 * M_HEADS - G_RANK), (0, 0))).astype(BF16)
    wr = jnp.pad(w_router[0], ((0, 0), (0, LANES - N_EXPERTS))).astype(BF16)
    br = jnp.pad(b_router[0], (0, LANES - N_EXPERTS), constant_values=NEG_BIG)[None, :]
    wpa, wpb, wout = w_pa[0].astype(BF16), w_pb[0].astype(BF16), w_out[0].astype(BF16)
    wproj, wgate = w_ple_proj[0].astype(BF16), w_ple_gate[0].astype(BF16)
    row = lambda a: a.reshape(1, -1)

    def mix_path(x3d, conv0, c0, n0, m0, s0):
        batch, T, _ = x3d.shape
        x = x3d.reshape(batch * T, D_MODEL)
        z, zs = _inproj(x, row(g_mix[0]), w_big, b_big, w_small, b_small, _tile(batch * T, 2048))
        L = min(MIX_CHUNK, T)
        conv0p = jnp.pad(conv0, ((0, 0), (SUBLANES - (CONV_W - 1), 0), (0, 0)))
        m0b = jnp.broadcast_to(m0[:, :, None, None], (batch, M_HEADS, SUBLANES, LANES))
        ya, yb, tail, c1, n1, m1, st1 = _mixers(
            z, zs, conv0p, c0, n0[:, :, None, :], m0b, jnp.swapaxes(s0, -1, -2),
            conv_w[0], row(conv_b[0]), row(g_mnorm[0]), w2, row(b_a2[0]), row(g_gnorm[0]), batch, T, L)
        new_state = (tail[:, SUBLANES - (CONV_W - 1):, :], c1, n1[:, :, 0, :], m1[:, :, 0, 0],
                     jnp.swapaxes(st1, -1, -2))
        return x, z, ya, yb, new_state

    zero_state = (jnp.zeros((bp, CONV_W - 1, 2 * M_DIM), F32), jnp.zeros((bp, M_HEADS, M_HD, M_HD), F32),
                  jnp.zeros((bp, M_HEADS, M_HD), F32), jnp.zeros((bp, M_HEADS), F32),
                  jnp.zeros((bp, G_HEADS, G_DK, G_DV), F32))
    xp, zp, yap, ybp, st_p = mix_path(x_prompt, *zero_state)
    xs_, zs_, yas, ybs, st_s = mix_path(x_sample, state_conv[0], state_mlstm_C[0], state_mlstm_n[0],
                                        state_mlstm_m[0], state_gla_S[0])

    n_p, n_s = xp.shape[0], xs_.shape[0]
    n_all = n_p + n_s
    tm = n_s
    x1p, x1s, xn, eop, eos, wop, wos, cnt = _outproj((yap, ybp, zp, xp), (yas, ybs, zs_, xs_), wpa, wpb, wout,
                                                     row(g_ffn[0]), wr, br, tm)
    eo = jnp.concatenate([eop, eos], axis=1)

    counts = cnt[0, :N_EXPERTS].astype(jnp.int32)
    padded = (counts + MOE_ROWS - 1) // MOE_ROWS * MOE_ROWS
    pend = jnp.cumsum(padded)
    pstart = pend - padded
    nk = n_all * TOP_K
    n_blocks = (nk + N_EXPERTS * (MOE_ROWS - 1) + MOE_ROWS - 1) // MOE_ROWS
    starts = jnp.arange(n_blocks, dtype=jnp.int32) * MOE_ROWS
    block_e = jnp.minimum(jnp.sum((pend[None, :] <= starts[:, None]).astype(jnp.int32), axis=1), N_EXPERTS - 1)
    ids = jnp.arange(N_EXPERTS, dtype=jnp.int32)
    later = jnp.where((ids[None, :] > ids[:, None]) & (counts[None, :] > 0), ids[None, :], N_EXPERTS)
    next_e = jnp.min(later, axis=1)
    group = jnp.cumsum(jnp.concatenate([jnp.zeros((1,), jnp.int32),
                                        (block_e[1:] != block_e[:-1]).astype(jnp.int32)]))
    plan = jnp.stack([block_e, next_e[block_e], group % 2]).astype(jnp.int32)
    n_used = (pend[-1:] // MOE_ROWS).astype(jnp.int32)
    e, rank = eo[:TOP_K], eo[TOP_K:2 * TOP_K]
    onehot = e[..., None] == jnp.arange(N_EXPERTS, dtype=jnp.int32)
    pos = (jnp.sum(jnp.where(onehot, pstart, 0), axis=-1) + rank).reshape(-1)
    fill = jnp.concatenate([pstart + counts, padded - counts, n_used]).astype(jnp.int32)

    xs_sorted = _dispatch(pos, fill, xn, n_all, n_blocks, tm)
    ye = _experts(plan, n_used, xs_sorted, w_gate_up[0], b_gate_up[0][:, None, :], w_down[0],
                  b_down[0][:, None, :], MOE_ROWS)
    fin = lambda x1, wo, p, tok0, tmc: _combine(pos, ye, x1, wo, p.reshape(-1, PLE_DIM), wproj, wgate,
                                                row(g_ple[0]), row(g_final), tmc, tok0, n_all)
    y_p = fin(x1p, wop, p_prompt[0], 0, tm).reshape(bp, tp, D_MODEL)
    y_s = fin(x1s, wos, p_sample[0], n_p, _tile(n_s // 2, tm)).reshape(bs, ts, D_MODEL)

    lead = lambda a: a[None]
    return (y_p, y_s) + tuple(lead(a) for a in st_p) + tuple(lead(a) for a in st_s)
```

```python
import functools

import numpy as np
import jax
import jax.numpy as jnp
from jax import lax
from jax.experimental import pallas as pl
from jax.experimental.pallas import tpu as pltpu

F32 = jnp.float32
BF16 = jnp.bfloat16

D_MODEL = 1024
EPS = 1e-6
M_HEADS = 4
M_DIM = D_MODEL
M_HD = M_DIM // M_HEADS
CONV_W = 4
G_HEADS = 4
G_KDIM = D_MODEL // 2
G_VDIM = D_MODEL
G_DK = G_KDIM // G_HEADS
G_DV = G_VDIM // G_HEADS
G_RANK = 16
G_TAU = 16.0
N_EXPERTS = 32
TOP_K = 4
D_FF = D_MODEL
SWIGLU_LIMIT = 7.0
SWIGLU_ALPHA = 1.702
PLE_DIM = 256
SPLITS = (M_DIM, M_DIM, M_DIM, M_HEADS, M_HEADS, M_DIM, G_KDIM, G_KDIM, G_VDIM, G_RANK, G_VDIM, D_MODEL, D_MODEL)
_OFF = np.concatenate([[0], np.cumsum(SPLITS)]).tolist()

LANES = 128
SUBLANES = 8
ROW_CHUNKS = D_MODEL // LANES
VMEM_LIMIT = 56 * 1024 * 1024
MIX_CHUNK = 256
MOE_ROWS = 512
NEG_BIG = -1e30


def _sigmoid(x):
    return 0.5 * jnp.tanh(0.5 * x) + 0.5


def _silu(x):
    h = 0.5 * x
    return h * jnp.tanh(h) + h


def _log_sigmoid(x):
    return jnp.minimum(x, 0.0) - jnp.log(1.0 + jnp.exp(-jnp.abs(x)))


def _rms(x, g):
    return x * lax.rsqrt(jnp.mean(x * x, axis=-1, keepdims=True) + EPS) * g


def _dot(a, b):
    return jnp.dot(a, b, preferred_element_type=F32)


def _dot_nt(a, b):
    return lax.dot_general(a, b, (((1,), (1,)), ((), ())), preferred_element_type=F32)


def _dot_tn(a, b):
    return lax.dot_general(a, b, (((0,), (0,)), ((), ())), preferred_element_type=F32)


def _params(*sem):
    return pltpu.CompilerParams(dimension_semantics=sem, vmem_limit_bytes=VMEM_LIMIT)


def _inproj_kernel(x_ref, g_ref, w_ref, b_ref, ws_ref, bs_ref, z_ref, zs_ref, h_scr):
    @pl.when(pl.program_id(1) == 0)
    def _():
        hb = _rms(x_ref[...], g_ref[...]).astype(BF16)
        h_scr[...] = hb
        zs_ref[...] = _dot(hb, ws_ref[...]) + bs_ref[...]

    z_ref[...] = (_dot(h_scr[...], w_ref[...]) + b_ref[...]).astype(BF16)


def _inproj(x, g, w_big, b_big, w_small, b_small, tm, tn=1024):
    n = x.shape[0]
    cols = w_big.shape[1]
    return pl.pallas_call(
        _inproj_kernel,
        grid=(n // tm, cols // tn),
        in_specs=[
            pl.BlockSpec((tm, D_MODEL), lambda i, j: (i, 0)),
            pl.BlockSpec((1, D_MODEL), lambda i, j: (0, 0)),
            pl.BlockSpec((D_MODEL, tn), lambda i, j: (0, j)),
            pl.BlockSpec((1, tn), lambda i, j: (0, j)),
            pl.BlockSpec((D_MODEL, LANES), lambda i, j: (0, 0)),
            pl.BlockSpec((1, LANES), lambda i, j: (0, 0)),
        ],
        out_specs=[
            pl.BlockSpec((tm, tn), lambda i, j: (i, j)),
            pl.BlockSpec((tm, LANES), lambda i, j: (i, 0)),
        ],
        out_shape=[jax.ShapeDtypeStruct((n, cols), BF16), jax.ShapeDtypeStruct((n, LANES), F32)],
        scratch_shapes=[pltpu.VMEM((tm, D_MODEL), BF16)],
        compiler_params=_params("parallel", "arbitrary"),
        name="inproj",
    )(x, g, w_big, b_big, w_small, b_small)


def _level_matrix(L):
    t = np.arange(L)[:, None]
    s = np.arange(L)[None, :]
    x = np.maximum(t ^ s, 1)
    lv = np.floor(np.log2(x)).astype(np.int32)
    lv = np.where(t == s, -1, lv)
    return np.where(s > t, -2, lv).astype(np.int32)


def _bcast_row(x, period, r):
    L, W = x.shape
    x3 = x.reshape(L // period, period, W)
    return jnp.broadcast_to(x3[:, r:r + 1, :], x3.shape).reshape(L, W)


def _decay_products(ela, L):
    W = ela.shape[1]
    row = lax.broadcasted_iota(jnp.int32, (L, W), 0)
    fwd = [ela]
    rex = [jnp.ones_like(ela)]
    b = 1
    while b < L:
        per = 2 * b
        res = row & (per - 1)
        f, r = fwd[-1], rex[-1]
        if per < SUBLANES:
            mulf = jnp.ones_like(ela)
            mulr = jnp.ones_like(ela)
            for j in range(b, per):
                mulf = jnp.where(res == j, pltpu.roll(f, j - (b - 1), 0), mulf)
            for j in range(0, b):
                mulr = jnp.where(res == j, pltpu.roll(f, L - (per - 1 - j), 0), mulr)
        else:
            mulf = jnp.where(res >= b, _bcast_row(f, per, b - 1), 1.0)
            mulr = jnp.where(res < b, _bcast_row(f, per, per - 1), 1.0)
        fwd.append(f * mulf)
        rex.append(r * mulr)
        b = per
    return fwd, rex


def _mixer_kernel(zqk_ref, zv_ref, zo_ref, zgq_ref, zgk_ref, zgv_ref, zgg_ref, zs_ref,
                  conv0_ref, c0_ref, n0_ref, m0_ref, s0_ref,
                  cw_ref, cb_ref, gm_ref, w2_ref, ba2_ref, gg_ref, lv_ref, shift_ref,
                  ya_ref, yb_ref, tail_ref, c_ref, n_ref, m_ref, st_ref, *, L):
    @pl.when(pl.program_id(1) == 0)
    def _():
        tail_ref[...] = conv0_ref[...]
        c_ref[...] = c0_ref[...]
        n_ref[...] = n0_ref[...]
        m_ref[...] = m0_ref[...]
        st_ref[...] = s0_ref[...]

    lv = lv_ref[...]
    causal = lv >= -1

    u_b = zqk_ref[...]
    u = u_b.astype(F32)
    tail = tail_ref[0]
    row8 = lax.broadcasted_iota(jnp.int32, (SUBLANES, 2 * M_DIM), 0)
    acc = cb_ref[...] + cw_ref[CONV_W - 1:CONV_W, :] * u
    for w in range(CONV_W - 1):
        delay = CONV_W - 1 - w
        shifted = _dot(shift_ref[w], u_b)
        head = jnp.where(row8 < delay, pltpu.roll(tail, delay, 0), 0.0)
        shifted = jnp.concatenate([shifted[0:SUBLANES] + head, shifted[SUBLANES:]], axis=0)
        acc = acc + cw_ref[w:w + 1, :] * shifted
    tail_ref[0] = u[L - SUBLANES:L, :]
    qk = _silu(acc)

    zs = zs_ref[...]
    row = lax.broadcasted_iota(jnp.int32, (L, LANES), 0)
    col = lax.broadcasted_iota(jnp.int32, (L, LANES), 1)
    bcs = _log_sigmoid(zs)
    sh = 1
    while sh < L:
        bcs = bcs + jnp.where(row >= sh, pltpu.roll(bcs, sh, 0), 0.0)
        sh *= 2
    gates = jnp.where(col < M_HEADS, zs, bcs)
    if L < LANES:
        gates = jnp.concatenate([gates, jnp.zeros((LANES - L, LANES), F32)], axis=0)
    gt = gates.T[:, :L]

    heads = range(M_HEADS)
    hs = [slice(h * M_HD, (h + 1) * M_HD) for h in heads]
    ig_col = [zs[:, h:h + 1] for h in heads]
    b_col = [bcs[:, M_HEADS + h:M_HEADS + h + 1] for h in heads]
    m_prev = [m_ref[0, h][0:1, 0:1] for h in heads]
    logd = [jnp.where(causal, b_col[h] - gt[M_HEADS + h:M_HEADS + h + 1, :] + gt[h:h + 1, :], -jnp.inf) for h in heads]
    inter = [b_col[h] + m_prev[h] for h in heads]
    m_t = [jnp.maximum(inter[h], jnp.max(logd[h], axis=-1, keepdims=True)) for h in heads]
    q_f = [qk[:, hs[h]] for h in heads]
    k_f = [qk[:, M_DIM + h * M_HD:M_DIM + (h + 1) * M_HD] * (M_HD ** -0.5) for h in heads]
    q_b = [q.astype(BF16) for q in q_f]
    v_b = [zv_ref[:, hs[h]] for h in heads]
    s_raw = [_dot_nt(q_b[h], k_f[h].astype(BF16)) for h in heads]
    c_old = [c_ref[0, h] for h in heads]
    n_old = [n_ref[0, h] for h in heads]
    qc = [_dot(q_b[h], c_old[h].astype(BF16)) for h in heads]
    s = [s_raw[h] * jnp.exp(logd[h] - m_t[h]) for h in heads]
    w_int = [jnp.exp(inter[h] - m_t[h]) for h in heads]
    num = [_dot(s[h].astype(BF16), v_b[h]) + w_int[h] * qc[h] for h in heads]
    den = [jnp.sum(s[h], axis=-1, keepdims=True) + w_int[h] * jnp.sum(q_f[h] * n_old[h], axis=-1, keepdims=True)
           for h in heads]
    hh = [num[h] * (1.0 / jnp.maximum(jnp.abs(den[h]), jnp.exp(-m_t[h]))) for h in heads]
    for h in heads:
        m_new = m_t[h][L - 1:L, :]
        b_last = b_col[h][L - 1:L, :]
        w_k = jnp.exp(b_last - b_col[h] + ig_col[h] - m_new)
        dec = jnp.exp(b_last + m_prev[h] - m_new)
        k_w = k_f[h] * w_k
        c_ref[0, h] = dec * c_old[h] + _dot_tn(k_w.astype(BF16), v_b[h])
        n_ref[0, h] = dec * n_old[h] + jnp.sum(k_w, axis=0, keepdims=True)
        m_ref[0, h] = jnp.broadcast_to(m_new, (SUBLANES, LANES))
    for h in heads:
        hn = _rms(hh[h], gm_ref[:, hs[h]])
        ya_ref[:, hs[h]] = (hn * _sigmoid(zo_ref[:, hs[h]].astype(F32))).astype(BF16)

    la = _log_sigmoid(_dot(zs.astype(BF16), w2_ref[...]) + ba2_ref[...]) * (1.0 / G_TAU)
    fwd, rex = _decay_products(jnp.exp(la), L)
    nlev = len(fwd) - 1
    qg = zgq_ref[...].astype(F32) * (G_DK ** -0.5)
    kg = zgk_ref[...].astype(F32)
    q_d = qg.astype(BF16)
    k_d = zgk_ref[...]
    qs = [(qg * fwd[j]).astype(BF16) for j in range(nlev)]
    ks = [k_d] + [(kg * rex[j]).astype(BF16) for j in range(1, nlev)]
    q_in = (qg * fwd[nlev]).astype(BF16)
    k_out = (kg * rex[nlev]).astype(BF16)
    dec_all = fwd[nlev][L - 1:L, :]
    gheads = range(G_HEADS)
    ks_ = [slice(h * G_DK, (h + 1) * G_DK) for h in gheads]
    vs_ = [slice(h * G_DV, (h + 1) * G_DV) for h in gheads]
    att = [jnp.where(lv == -1, _dot_nt(q_d[:, ks_[h]], k_d[:, ks_[h]]), 0.0) for h in gheads]
    for j in range(nlev):
        for h in gheads:
            att[h] = jnp.where(lv == j, _dot_nt(qs[j][:, ks_[h]], ks[j][:, ks_[h]]), att[h])
    v_g = [zgv_ref[:, vs_[h]] for h in gheads]
    st_old = [st_ref[0, h] for h in gheads]
    o = [_dot(att[h].astype(BF16), v_g[h]) + _dot_nt(q_in[:, ks_[h]], st_old[h].astype(BF16)) for h in gheads]
    for h in gheads:
        st_ref[0, h] = st_old[h] * dec_all[:, ks_[h]] + _dot_tn(v_g[h], k_out[:, ks_[h]])
    for h in gheads:
        on = _rms(o[h], gg_ref[:, vs_[h]])
        g = zgg_ref[:, vs_[h]].astype(F32)
        yb_ref[:, vs_[h]] = (on * _silu(g)).astype(BF16)


def _mixers(z, zs, conv0, c0, n0, m0, st0, cw, cb, gm, w2, ba2, gg, batch, T, L):
    nc = T // L
    lv = jnp.asarray(_level_matrix(L))
    shift = jnp.asarray(np.stack([np.eye(L, k=-(CONV_W - 1 - w)) for w in range(CONV_W - 1)]), BF16)
    row_blk = lambda width, cidx: pl.BlockSpec((L, width), lambda b, c: (b * nc + c, cidx))
    state = lambda *shape: pl.BlockSpec((1,) + shape, lambda b, c: (b,) + (0,) * len(shape))
    const = lambda *shape: pl.BlockSpec(shape, lambda b, c: (0,) * len(shape))
    n_tok = batch * T
    return pl.pallas_call(
        functools.partial(_mixer_kernel, L=L),
        grid=(batch, nc),
        in_specs=[
            row_blk(2 * M_DIM, 0),
            row_blk(M_DIM, 2),
            row_blk(M_DIM, 3),
            row_blk(G_KDIM, 8),
            row_blk(G_KDIM, 9),
            row_blk(G_VDIM, 5),
            row_blk(G_VDIM, 6),
            pl.BlockSpec((L, LANES), lambda b, c: (b * nc + c, 0)),
            state(SUBLANES, 2 * M_DIM),
            state(M_HEADS, M_HD, M_HD),
            state(M_HEADS, 1, M_HD),
            state(M_HEADS, SUBLANES, LANES),
            state(G_HEADS, G_DV, G_DK),
            const(CONV_W, 2 * M_DIM),
            const(1, 2 * M_DIM),
            const(1, M_DIM),
            const(LANES, G_KDIM),
            const(1, G_KDIM),
            const(1, G_VDIM),
            const(L, L),
            const(CONV_W - 1, L, L),
        ],
        out_specs=[
            pl.BlockSpec((L, M_DIM), lambda b, c: (b * nc + c, 0)),
            pl.BlockSpec((L, G_VDIM), lambda b, c: (b * nc + c, 0)),
            state(SUBLANES, 2 * M_DIM),
            state(M_HEADS, M_HD, M_HD),
            state(M_HEADS, 1, M_HD),
            state(M_HEADS, SUBLANES, LANES),
            state(G_HEADS, G_DV, G_DK),
        ],
        out_shape=[
            jax.ShapeDtypeStruct((n_tok, M_DIM), BF16),
            jax.ShapeDtypeStruct((n_tok, G_VDIM), BF16),
            jax.ShapeDtypeStruct((batch, SUBLANES, 2 * M_DIM), F32),
            jax.ShapeDtypeStruct((batch, M_HEADS, M_HD, M_HD), F32),
            jax.ShapeDtypeStruct((batch, M_HEADS, 1, M_HD), F32),
            jax.ShapeDtypeStruct((batch, M_HEADS, SUBLANES, LANES), F32),
            jax.ShapeDtypeStruct((batch, G_HEADS, G_DV, G_DK), F32),
        ],
        compiler_params=_params("parallel", "arbitrary"),
        name="mixers",
    )(z, z, z, z, z, z, z, zs, conv0, c0, n0, m0, st0, cw, cb, gm, w2, ba2, gg, lv, shift)


def _outproj_kernel(ya_p, yb_p, ua_p, ub_p, x_p, ya_s, yb_s, ua_s, ub_s, x_s,
                    wpa_ref, wpb_ref, wout_ref, gffn_ref, wr_ref, br_ref, tril_ref,
                    x1p_ref, x1s_ref, xn_ref, eop_ref, eos_ref, wop_ref, wos_ref, cnt_ref, *, tm, steps_p):
    i = pl.program_id(0)

    @pl.when(i == 0)
    def _():
        cnt_ref[...] = jnp.zeros_like(cnt_ref)

    def chains(ins, parts, x1_ref, eo_ref, wo_ref):
        ya_ref, yb_ref, ua_ref, ub_ref, x_ref = ins
        a = [_dot(ya_ref[rs, :], wpa_ref[...]) for rs in parts]
        b = [_dot(yb_ref[rs, :], wpb_ref[...]) for rs in parts]
        merged = [_sigmoid(ua_ref[rs, :].astype(F32)) * a_ + _sigmoid(ub_ref[rs, :].astype(F32)) * b_
                  for rs, a_, b_ in zip(parts, a, b)]
        x1 = [x_ref[rs, :] + _dot(m.astype(BF16), wout_ref[...]) for rs, m in zip(parts, merged)]
        xn = []
        for h, (rs, x1_) in enumerate(zip(parts, x1)):
            x1_ref[rs, :] = x1_
            xn_ = _rms(x1_, gffn_ref[...])
            for j in range(ROW_CHUNKS):
                xn_ref[pl.ds(h * tm * ROW_CHUNKS + j, tm, stride=ROW_CHUNKS), :] = xn_[:, j * LANES:(j + 1) * LANES]
            xn.append(xn_)

        ls = [_dot(xn_.astype(BF16), wr_ref[...]) + br_ref[...] for xn_ in xn]
        lane = lax.broadcasted_iota(jnp.int32, (tm, LANES), 1).astype(F32)
        vals = [[] for _ in parts]
        idxs = [[] for _ in parts]
        for _ in range(TOP_K):
            for h in range(len(parts)):
                mx = jnp.max(ls[h], axis=-1, keepdims=True)
                ix = jnp.min(jnp.where(ls[h] == mx, lane, float(LANES)), axis=-1, keepdims=True)
                vals[h].append(mx)
                idxs[h].append(ix)
                ls[h] = jnp.where(lane == ix, -jnp.inf, ls[h])

        carry = cnt_ref[0:1, :]
        for h, rs in enumerate(parts):
            ev = [jnp.exp(v - vals[h][0]) for v in vals[h]]
            den = ev[0] + ev[1] + ev[2] + ev[3]
            cnt = jnp.zeros((tm, LANES), F32)
            for ix in idxs[h]:
                cnt = cnt + jnp.where(lane == ix, 1.0, 0.0)
            pre = _dot(tril_ref[...], cnt.astype(BF16)) + carry
            carry = carry + jnp.sum(cnt, axis=0, keepdims=True)
            eo = jnp.zeros((tm, LANES), F32)
            wo = jnp.zeros((tm, LANES), F32)
            for k in range(TOP_K):
                rank = jnp.sum(jnp.where(lane == idxs[h][k], pre, 0.0), axis=-1, keepdims=True)
                eo = jnp.where(lane == float(k), idxs[h][k], eo)
                eo = jnp.where(lane == float(TOP_K + k), rank, eo)
                wo = jnp.where(lane == float(k), ev[k] / den, wo)
            eo_ref[:, rs] = eo.T[0:SUBLANES, :].astype(jnp.int32)
            wo_ref[rs, :] = wo
        cnt_ref[...] = jnp.broadcast_to(carry, cnt_ref.shape)

    prompt = (ya_p, yb_p, ua_p, ub_p, x_p)
    sample = (ya_s, yb_s, ua_s, ub_s, x_s)

    @pl.when(i < steps_p)
    def _():
        chains(prompt, [slice(0, tm), slice(tm, 2 * tm)], x1p_ref, eop_ref, wop_ref)

    @pl.when(i == steps_p)
    def _():
        chains(sample, [slice(0, tm)], x1s_ref, eos_ref, wos_ref)
        xn_ref[pl.ds(tm * ROW_CHUNKS, tm * ROW_CHUNKS), :] = jnp.zeros((tm * ROW_CHUNKS, LANES), F32)


def _outproj(prompt, sample, wpa, wpb, wout, gffn, wr, br, tm):
    n_p, n_s = prompt[3].shape[0], sample[3].shape[0]
    assert n_s == tm and n_p % (2 * tm) == 0
    steps_p = n_p // (2 * tm)
    tril = jnp.asarray(np.tril(np.ones((tm, tm), np.float32), -1), BF16)
    pidx = lambda i: jnp.minimum(i, steps_p - 1)
    src = lambda rows, idx: [pl.BlockSpec((rows, M_DIM), lambda i: (idx(i), 0)),
                             pl.BlockSpec((rows, G_VDIM), lambda i: (idx(i), 0)),
                             pl.BlockSpec((rows, D_MODEL), lambda i: (idx(i), 7)),
                             pl.BlockSpec((rows, D_MODEL), lambda i: (idx(i), 8)),
                             pl.BlockSpec((rows, D_MODEL), lambda i: (idx(i), 0))]
    const = lambda *shape: pl.BlockSpec(shape, lambda i: (0,) * len(shape))
    ya_p, yb_p, z_p, x_p = prompt
    ya_s, yb_s, z_s, x_s = sample
    return pl.pallas_call(
        functools.partial(_outproj_kernel, tm=tm, steps_p=steps_p),
        grid=(steps_p + 1,),
        in_specs=src(2 * tm, pidx) + src(tm, lambda i: 0) + [
            const(M_DIM, D_MODEL), const(G_VDIM, D_MODEL), const(D_MODEL, D_MODEL), const(1, D_MODEL),
            const(D_MODEL, LANES), const(1, LANES), const(tm, tm),
        ],
        out_specs=[
            pl.BlockSpec((2 * tm, D_MODEL), lambda i: (pidx(i), 0)),
            const(tm, D_MODEL),
            pl.BlockSpec((2 * tm * ROW_CHUNKS, LANES), lambda i: (i, 0)),
            pl.BlockSpec((SUBLANES, 2 * tm), lambda i: (0, pidx(i))),
            const(SUBLANES, tm),
            pl.BlockSpec((2 * tm, LANES), lambda i: (pidx(i), 0)),
            const(tm, LANES),
            const(SUBLANES, LANES),
        ],
        out_shape=[
            jax.ShapeDtypeStruct((n_p, D_MODEL), F32),
            jax.ShapeDtypeStruct((n_s, D_MODEL), F32),
            jax.ShapeDtypeStruct(((n_p + 2 * tm) * ROW_CHUNKS, LANES), F32),
            jax.ShapeDtypeStruct((SUBLANES, n_p), jnp.int32),
            jax.ShapeDtypeStruct((SUBLANES, n_s), jnp.int32),
            jax.ShapeDtypeStruct((n_p, LANES), F32),
            jax.ShapeDtypeStruct((n_s, LANES), F32),
            jax.ShapeDtypeStruct((SUBLANES, LANES), F32),
        ],
        compiler_params=_params("arbitrary"),
        name="outproj_router",
    )(ya_p, yb_p, z_p, z_p, x_p, ya_s, yb_s, z_s, z_s, x_s, wpa, wpb, wout, gffn, wr, br, tril)


def _row_copy(src, src_row, dst, dst_row, sem):
    return pltpu.make_async_copy(
        src.at[pl.ds(pl.multiple_of(src_row * ROW_CHUNKS, ROW_CHUNKS), ROW_CHUNKS)],
        dst.at[pl.ds(pl.multiple_of(dst_row * ROW_CHUNKS, ROW_CHUNKS), ROW_CHUNKS)], sem)


def _dispatch_kernel(pos_ref, fill_ref, xn_ref, xs_ref, zeros_ref, zsem, sem, *, tm, n, n_blocks):
    i = pl.program_id(0)
    base = i * tm

    def issue(r, carry):
        for k in range(TOP_K):
            _row_copy(xn_ref, r, xs_ref, pos_ref[k * n + base + r], sem).start(priority=k % 2)
        return carry

    lax.fori_loop(0, tm, issue, 0)
    for _ in range(TOP_K):
        pltpu.make_async_copy(xn_ref, xs_ref.at[pl.ds(0, tm * ROW_CHUNKS)], sem).wait()

    @pl.when(i == 0)
    def _():
        zeros_ref[...] = jnp.zeros_like(zeros_ref)
        bits = [1 << b for b in range(MOE_ROWS.bit_length() - 1)]

        def zero_copy(off, nrows):
            return pltpu.make_async_copy(
                zeros_ref.at[pl.ds(0, nrows * ROW_CHUNKS)],
                xs_ref.at[pl.ds(pl.multiple_of(off * ROW_CHUNKS, ROW_CHUNKS), nrows * ROW_CHUNKS)], zsem)

        def expert(e, carry):
            npad = fill_ref[N_EXPERTS + e]
            for wait in (False, True):
                off = fill_ref[e]
                for bit in bits:
                    @pl.when((npad & bit) != 0)
                    def _():
                        zero_copy(off, bit).wait() if wait else zero_copy(off, bit).start()
                    off = off + (npad & bit)
            return carry

        lax.fori_loop(0, N_EXPERTS, expert, 0)

        def block(b, carry):
            zero_copy(b * MOE_ROWS, MOE_ROWS).start()
            zero_copy(b * MOE_ROWS, MOE_ROWS).wait()
            return carry

        lax.fori_loop(fill_ref[2 * N_EXPERTS], n_blocks, block, 0)


def _dispatch(pos, fill, xn, n, n_blocks, tm):
    return pl.pallas_call(
        functools.partial(_dispatch_kernel, tm=tm, n=n, n_blocks=n_blocks),
        grid_spec=pltpu.PrefetchScalarGridSpec(
            num_scalar_prefetch=2,
            grid=(n // tm,),
            in_specs=[pl.BlockSpec((tm * ROW_CHUNKS, LANES), lambda i, *_: (i, 0))],
            out_specs=pl.BlockSpec(memory_space=pl.ANY),
            scratch_shapes=[pltpu.VMEM((MOE_ROWS * ROW_CHUNKS, LANES), F32), pltpu.SemaphoreType.DMA(()),
                            pltpu.SemaphoreType.DMA(())],
        ),
        out_shape=jax.ShapeDtypeStruct((n_blocks * MOE_ROWS * ROW_CHUNKS, LANES), F32),
        compiler_params=_params("arbitrary"),
        name="dispatch",
    )(pos, fill, xn)


def _expert_kernel(plan_ref, nused_ref, xs_ref, wgu_hbm, bgu_ref, wd_hbm, bd_ref, y_ref,
                   wgu_f32, wd_f32, wgu_bf, wd_bf, wsem, *, rows):
    i = pl.program_id(0)
    active = i < nused_ref[0]
    expert, next_expert, slot = plan_ref[0, i], plan_ref[1, i], plan_ref[2, i]
    first = jnp.logical_or(i == 0, expert != plan_ref[0, jnp.maximum(i - 1, 0)])

    def weight_copies(e, s):
        return (pltpu.make_async_copy(wgu_hbm.at[e], wgu_f32.at[s], wsem.at[0, s]),
                pltpu.make_async_copy(wd_hbm.at[e], wd_f32.at[s], wsem.at[1, s]))

    @pl.when(i == 0)
    def _():
        for c in weight_copies(expert, slot):
            c.start()

    @pl.when(jnp.logical_and(active, first))
    def _():
        for c in weight_copies(expert, slot):
            c.wait()
        wgu_bf[...] = wgu_f32[slot].astype(BF16)
        wd_bf[...] = wd_f32[slot].astype(BF16)

        @pl.when(next_expert < N_EXPERTS)
        def _():
            for c in weight_copies(next_expert, 1 - slot):
                c.start()

    @pl.when(active)
    def _():
        x = jnp.concatenate([xs_ref[pl.ds(j, rows, stride=ROW_CHUNKS), :] for j in range(ROW_CHUNKS)], axis=-1)
        zz = _dot(x.astype(BF16), wgu_bf[...]) + bgu_ref[...]
        g = jnp.minimum(zz[:, :D_FF], SWIGLU_LIMIT)
        u = jnp.clip(zz[:, D_FF:], -SWIGLU_LIMIT, SWIGLU_LIMIT)
        act = g * _sigmoid(SWIGLU_ALPHA * g) * (u + 1.0)
        y = _dot(act.astype(BF16), wd_bf[...]) + bd_ref[...]
        for j in range(ROW_CHUNKS):
            y_ref[pl.ds(j, rows, stride=ROW_CHUNKS), :] = y[:, j * LANES:(j + 1) * LANES]

    @pl.when(jnp.logical_not(active))
    def _():
        y_ref[...] = jnp.zeros_like(y_ref)


def _experts(plan, n_used, xs, wgu, bgu, wd, bd, rows):
    nb = plan.shape[1]
    return pl.pallas_call(
        functools.partial(_expert_kernel, rows=rows),
        grid_spec=pltpu.PrefetchScalarGridSpec(
            num_scalar_prefetch=2,
            grid=(nb,),
            in_specs=[
                pl.BlockSpec((rows * ROW_CHUNKS, LANES), lambda i, plan, nu: (i, 0)),
                pl.BlockSpec(memory_space=pl.ANY),
                pl.BlockSpec((None, 1, 2 * D_FF), lambda i, plan, nu: (plan[0, i], 0, 0)),
                pl.BlockSpec(memory_space=pl.ANY),
                pl.BlockSpec((None, 1, D_MODEL), lambda i, plan, nu: (plan[0, i], 0, 0)),
            ],
            out_specs=pl.BlockSpec((rows * ROW_CHUNKS, LANES), lambda i, plan, nu: (i, 0)),
            scratch_shapes=[pltpu.VMEM((2, D_MODEL, 2 * D_FF), F32), pltpu.VMEM((2, D_FF, D_MODEL), F32),
                            pltpu.VMEM((D_MODEL, 2 * D_FF), BF16), pltpu.VMEM((D_FF, D_MODEL), BF16),
                            pltpu.SemaphoreType.DMA((2, 2))],
        ),
        out_shape=jax.ShapeDtypeStruct(xs.shape, F32),
        compiler_params=_params("arbitrary"),
        name="experts",
    )(plan, n_used, xs, wgu, bgu, wd, bd)


def _combine_kernel(pos_ref, ye_ref, x1_ref, wo_ref, p_ref, wproj_ref, wgate_ref, gple_ref, gfin_ref,
                    y_ref, rows_a, rows_b, sem, *, tm, nt, n_all, tok0):
    i = pl.program_id(0)
    seg = tm * ROW_CHUNKS

    def start_row(tok, r, dst, sem_slot):
        for k in range(TOP_K):
            srow = pl.multiple_of(pos_ref[k * n_all + tok] * ROW_CHUNKS, ROW_CHUNKS)
            pltpu.make_async_copy(ye_ref.at[pl.ds(srow, ROW_CHUNKS)],
                                  dst.at[pl.ds((k * tm + r) * ROW_CHUNKS, ROW_CHUNKS)], sem_slot).start(priority=k % 2)

    def wait_rows(dst, sem_slot):
        for k in range(TOP_K):
            pltpu.make_async_copy(ye_ref.at[pl.ds(0, seg)], dst.at[pl.ds(k * seg, seg)], sem_slot).wait()

    def compute(half, src):
        rs = slice(half * tm, (half + 1) * tm)
        wo = wo_ref[rs, :]
        moe = jnp.zeros((tm, D_MODEL), F32)
        for k in range(TOP_K):
            rows = jnp.concatenate(
                [src[pl.ds(k * seg + j, tm, stride=ROW_CHUNKS), :] for j in range(ROW_CHUNKS)], axis=-1)
            moe = moe + rows * wo[:, k:k + 1]
        x2 = x1_ref[rs, :] + moe
        gate = _sigmoid(_dot(_rms(x2, gple_ref[...]).astype(BF16), wgate_ref[...]))
        x3 = x2 + _dot(p_ref[rs, :].astype(BF16), wproj_ref[...]) * gate
        y_ref[rs, :] = _rms(x3, gfin_ref[...])

    @pl.when(i == 0)
    def _():
        def issue(r, carry):
            start_row(tok0 + r, r, rows_a, sem.at[0])
            return carry
        lax.fori_loop(0, tm, issue, 0)

    wait_rows(rows_a, sem.at[0])
    base_b = tok0 + (2 * i + 1) * tm
    for r in range(tm):
        start_row(base_b + r, r, rows_b, sem.at[1])
    compute(0, rows_a)

    wait_rows(rows_b, sem.at[1])
    base_a = tok0 + jnp.minimum(2 * i + 2, nt - 1) * tm
    for r in range(tm):
        start_row(base_a + r, r, rows_a, sem.at[0])
    compute(1, rows_b)

    @pl.when(i == nt // 2 - 1)
    def _():
        wait_rows(rows_a, sem.at[0])


def _combine(pos_flat, ye, x1, wo, p, wproj, wgate, gple, gfin, tm, tok0, n_all):
    n = p.shape[0]
    nt = n // tm
    assert nt % 2 == 0
    tok = lambda width: pl.BlockSpec((2 * tm, width), lambda i, pos: (i, 0))
    const = lambda *shape: pl.BlockSpec(shape, lambda i, pos: (0,) * len(shape))
    rows_buf = pltpu.VMEM((TOP_K * tm * ROW_CHUNKS, LANES), F32)
    return pl.pallas_call(
        functools.partial(_combine_kernel, tm=tm, nt=nt, n_all=n_all, tok0=tok0),
        grid_spec=pltpu.PrefetchScalarGridSpec(
            num_scalar_prefetch=1,
            grid=(nt // 2,),
            in_specs=[
                pl.BlockSpec(memory_space=pl.ANY),
                tok(D_MODEL), tok(LANES), tok(PLE_DIM),
                const(PLE_DIM, D_MODEL), const(D_MODEL, D_MODEL), const(1, D_MODEL), const(1, D_MODEL),
            ],
            out_specs=tok(D_MODEL),
            scratch_shapes=[rows_buf, rows_buf, pltpu.SemaphoreType.DMA((2,))],
        ),
        out_shape=jax.ShapeDtypeStruct((n, D_MODEL), F32),
        compiler_params=_params("arbitrary"),
        name="combine_ple",
    )(pos_flat, ye, x1, wo, p, wproj, wgate, gple, gfin)


def _tile(n, pref):
    t = pref
    while n % t:
        t //= 2
    return t


def _seg(a, i):
    return a[..., _OFF[i]:_OFF[i + 1]]


def kernel(x_prompt, x_sample, state_conv, state_mlstm_C, state_mlstm_n, state_mlstm_m, state_gla_S, p_prompt, p_sample, g_mix, w_in, b_in, conv_w, conv_b, g_mnorm, w_a2, b_a2, g_gnorm, w_pa, w_pb, w_out, g_ffn, w_router, b_router, w_gate_up, b_gate_up, w_down, b_down, g_ple, w_ple_proj, w_ple_gate, g_final):
    assert g_mix.shape[0] == 1, "single-layer kernel"
    bp, tp, _ = x_prompt.shape
    bs, ts, _ = x_sample.shape

    big = (0, 1, 2, 5, 6, 7, 8, 10, 11, 12)
    w_big = jnp.concatenate([_seg(w_in[0], i) for i in big], axis=-1).astype(BF16)
    b_big = jnp.concatenate([_seg(b_in[0], i) for i in big], axis=-1)[None, :]
    small = (3, 4, 9)
    n_small = sum(SPLITS[i] for i in small)
    w_small = jnp.pad(jnp.concatenate([_seg(w_in[0], i) for i in small], axis=-1),
                      ((0, 0), (0, LANES - n_small))).astype(BF16)
    b_small = jnp.pad(jnp.concatenate([_seg(b_in[0], i) for i in small], axis=-1), (0, LANES - n_small))[None, :]
    w2 = jnp.pad(w_a2[0], ((2 * M_HEADS, LANES - 2 * M_HEADS - G_RANK), (0, 0))).astype(BF16)
    wr = jnp.pad(w_router[0], ((0, 0), (0, LANES - N_EXPERTS))).astype(BF16)
    br = jnp.pad(b_router[0], (0, LANES - N_EXPERTS), constant_values=NEG_BIG)[None, :]
    wpa, wpb, wout = w_pa[0].astype(BF16), w_pb[0].astype(BF16), w_out[0].astype(BF16)
    wproj, wgate = w_ple_proj[0].astype(BF16), w_ple_gate[0].astype(BF16)
    row = lambda a: a.reshape(1, -1)

    def mix_path(x3d, conv0, c0, n0, m0, s0):
        batch, T, _ = x3d.shape
        x = x3d.reshape(batch * T, D_MODEL)
        z, zs = _inproj(x, row(g_mix[0]), w_big, b_big, w_small, b_small, _tile(batch * T, 2048))
        L = min(MIX_CHUNK, T)
        conv0p = jnp.pad(conv0, ((0, 0), (SUBLANES - (CONV_W - 1), 0), (0, 0)))
        m0b = jnp.broadcast_to(m0[:, :, None, None], (batch, M_HEADS, SUBLANES, LANES))
        ya, yb, tail, c1, n1, m1, st1 = _mixers(
            z, zs, conv0p, c0, n0[:, :, None, :], m0b, jnp.swapaxes(s0, -1, -2),
            conv_w[0], row(conv_b[0]), row(g_mnorm[0]), w2, row(b_a2[0]), row(g_gnorm[0]), batch, T, L)
        new_state = (tail[:, SUBLANES - (CONV_W - 1):, :], c1, n1[:, :, 0, :], m1[:, :, 0, 0],
                     jnp.swapaxes(st1, -1, -2))
        return x, z, ya, yb, new_state

    zero_state = (jnp.zeros((bp, CONV_W - 1, 2 * M_DIM), F32), jnp.zeros((bp, M_HEADS, M_HD, M_HD), F32),
                  jnp.zeros((bp, M_HEADS, M_HD), F32), jnp.zeros((bp, M_HEADS), F32),
                  jnp.zeros((bp, G_HEADS, G_DK, G_DV), F32))
    xp, zp, yap, ybp, st_p = mix_path(x_prompt, *zero_state)
    xs_, zs_, yas, ybs, st_s = mix_path(x_sample, state_conv[0], state_mlstm_C[0], state_mlstm_n[0],
                                        state_mlstm_m[0], state_gla_S[0])

    n_p, n_s = xp.shape[0], xs_.shape[0]
    n_all = n_p + n_s
    tm = n_s
    x1p, x1s, xn, eop, eos, wop, wos, cnt = _outproj((yap, ybp, zp, xp), (yas, ybs, zs_, xs_), wpa, wpb, wout,
                                                     row(g_ffn[0]), wr, br, tm)
    eo = jnp.concatenate([eop, eos], axis=1)

    counts = cnt[0, :N_EXPERTS].astype(jnp.int32)
    padded = (counts + MOE_ROWS - 1) // MOE_ROWS * MOE_ROWS
    pend = jnp.cumsum(padded)
    pstart = pend - padded
    nk = n_all * TOP_K
    n_blocks = (nk + N_EXPERTS * (MOE_ROWS - 1) + MOE_ROWS - 1) // MOE_ROWS
    starts = jnp.arange(n_blocks, dtype=jnp.int32) * MOE_ROWS
    block_e = jnp.minimum(jnp.sum((pend[None, :] <= starts[:, None]).astype(jnp.int32), axis=1), N_EXPERTS - 1)
    ids = jnp.arange(N_EXPERTS, dtype=jnp.int32)
    later = jnp.where((ids[None, :] > ids[:, None]) & (counts[None, :] > 0), ids[None, :], N_EXPERTS)
    next_e = jnp.min(later, axis=1)
    group = jnp.cumsum(jnp.concatenate([jnp.zeros((1,), jnp.int32),
                                        (block_e[1:] != block_e[:-1]).astype(jnp.int32)]))
    plan = jnp.stack([block_e, next_e[block_e], group % 2]).astype(jnp.int32)
    n_used = (pend[-1:] // MOE_ROWS).astype(jnp.int32)
    e, rank = eo[:TOP_K], eo[TOP_K:2 * TOP_K]
    onehot = e[..., None] == jnp.arange(N_EXPERTS, dtype=jnp.int32)
    pos = (jnp.sum(jnp.where(onehot, pstart, 0), axis=-1) + rank).reshape(-1)
    fill = jnp.concatenate([pstart + counts, padded - counts, n_used]).astype(jnp.int32)

    xs_sorted = _dispatch(pos, fill, xn, n_all, n_blocks, tm)
    ye = _experts(plan, n_used, xs_sorted, w_gate_up[0], b_gate_up[0][:, None, :], w_down[0],
                  b_down[0][:, None, :], MOE_ROWS)
    fin = lambda x1, wo, p, tok0, tmc: _combine(pos, ye, x1, wo, p.reshape(-1, PLE_DIM), wproj, wgate,
                                                row(g_ple[0]), row(g_final), tmc, tok0, n_all)
    y_p = fin(x1p, wop, p_prompt[0], 0, tm).reshape(bp, tp, D_MODEL)
    y_s = fin(x1s, wos, p_sample[0], n_p, _tile(n_s // 2, tm)).reshape(bs, ts, D_MODEL)

    lead = lambda a: a[None]
    return (y_p, y_s) + tuple(lead(a) for a in st_p) + tuple(lead(a) for a in st_s)
```

```python
import functools

import numpy as np
import jax
import jax.numpy as jnp
from jax import lax
from jax.experimental import pallas as pl
from jax.experimental.pallas import tpu as pltpu

F32 = jnp.float32
BF16 = jnp.bfloat16

D_MODEL = 1024
EPS = 1e-6
M_HEADS = 4
M_DIM = D_MODEL
M_HD = M_DIM // M_HEADS
CONV_W = 4
G_HEADS = 4
G_KDIM = D_MODEL // 2
G_VDIM = D_MODEL
G_DK = G_KDIM // G_HEADS
G_DV = G_VDIM // G_HEADS
G_RANK = 16
G_TAU = 16.0
N_EXPERTS = 32
TOP_K = 4
D_FF = D_MODEL
SWIGLU_LIMIT = 7.0
SWIGLU_ALPHA = 1.702
PLE_DIM = 256
SPLITS = (M_DIM, M_DIM, M_DIM, M_HEADS, M_HEADS, M_DIM, G_KDIM, G_KDIM, G_VDIM, G_RANK, G_VDIM, D_MODEL, D_MODEL)
_OFF = np.concatenate([[0], np.cumsum(SPLITS)]).tolist()

LANES = 128
SUBLANES = 8
ROW_CHUNKS = D_MODEL // LANES
VMEM_LIMIT = 56 * 1024 * 1024
MIX_CHUNK = 256
MOE_ROWS = 512
DISPATCH_ROWS = 1280
NEG_BIG = -1e30


def _sigmoid(x):
    return 0.5 * jnp.tanh(0.5 * x) + 0.5


def _silu(x):
    h = 0.5 * x
    return h * jnp.tanh(h) + h


def _log_sigmoid(x):
    return jnp.minimum(x, 0.0) - jnp.log(1.0 + jnp.exp(-jnp.abs(x)))


def _rms(x, g):
    return x * lax.rsqrt(jnp.mean(x * x, axis=-1, keepdims=True) + EPS) * g


def _dot(a, b):
    return jnp.dot(a, b, preferred_element_type=F32)


def _dot_nt(a, b):
    return lax.dot_general(a, b, (((1,), (1,)), ((), ())), preferred_element_type=F32)


def _dot_tn(a, b):
    return lax.dot_general(a, b, (((0,), (0,)), ((), ())), preferred_element_type=F32)


def _params(*sem):
    return pltpu.CompilerParams(dimension_semantics=sem, vmem_limit_bytes=VMEM_LIMIT)


def _inproj_kernel(x_ref, g_ref, w_ref, b_ref, ws_ref, bs_ref, z_ref, zs_ref, h_scr):
    @pl.when(pl.program_id(1) == 0)
    def _():
        hb = _rms(x_ref[...], g_ref[...]).astype(BF16)
        h_scr[...] = hb
        zs_ref[...] = _dot(hb, ws_ref[...]) + bs_ref[...]

    z_ref[...] = (_dot(h_scr[...], w_ref[...]) + b_ref[...]).astype(BF16)


def _inproj(x, g, w_big, b_big, w_small, b_small, tm, tn=1024):
    n = x.shape[0]
    cols = w_big.shape[1]
    return pl.pallas_call(
        _inproj_kernel,
        grid=(n // tm, cols // tn),
        in_specs=[
            pl.BlockSpec((tm, D_MODEL), lambda i, j: (i, 0)),
            pl.BlockSpec((1, D_MODEL), lambda i, j: (0, 0)),
            pl.BlockSpec((D_MODEL, tn), lambda i, j: (0, j)),
            pl.BlockSpec((1, tn), lambda i, j: (0, j)),
            pl.BlockSpec((D_MODEL, LANES), lambda i, j: (0, 0)),
            pl.BlockSpec((1, LANES), lambda i, j: (0, 0)),
        ],
        out_specs=[
            pl.BlockSpec((tm, tn), lambda i, j: (i, j)),
            pl.BlockSpec((tm, LANES), lambda i, j: (i, 0)),
        ],
        out_shape=[jax.ShapeDtypeStruct((n, cols), BF16), jax.ShapeDtypeStruct((n, LANES), F32)],
        scratch_shapes=[pltpu.VMEM((tm, D_MODEL), BF16)],
        compiler_params=_params("parallel", "arbitrary"),
        name="inproj",
    )(x, g, w_big, b_big, w_small, b_small)


def _level_matrix(L):
    t = np.arange(L)[:, None]
    s = np.arange(L)[None, :]
    x = np.maximum(t ^ s, 1)
    lv = np.floor(np.log2(x)).astype(np.int32)
    lv = np.where(t == s, -1, lv)
    return np.where(s > t, -2, lv).astype(np.int32)


def _bcast_row(x, period, r):
    L, W = x.shape
    x3 = x.reshape(L // period, period, W)
    return jnp.broadcast_to(x3[:, r:r + 1, :], x3.shape).reshape(L, W)


def _decay_products(ela, L):
    W = ela.shape[1]
    row = lax.broadcasted_iota(jnp.int32, (L, W), 0)
    fwd = [ela]
    rex = [jnp.ones_like(ela)]
    b = 1
    while b < L:
        per = 2 * b
        res = row & (per - 1)
        f, r = fwd[-1], rex[-1]
        if per < SUBLANES:
            mulf = jnp.ones_like(ela)
            mulr = jnp.ones_like(ela)
            for j in range(b, per):
                mulf = jnp.where(res == j, pltpu.roll(f, j - (b - 1), 0), mulf)
            for j in range(0, b):
                mulr = jnp.where(res == j, pltpu.roll(f, L - (per - 1 - j), 0), mulr)
        else:
            mulf = jnp.where(res >= b, _bcast_row(f, per, b - 1), 1.0)
            mulr = jnp.where(res < b, _bcast_row(f, per, per - 1), 1.0)
        fwd.append(f * mulf)
        rex.append(r * mulr)
        b = per
    return fwd, rex


def _mixer_kernel(zqk_ref, zv_ref, zo_ref, zgq_ref, zgk_ref, zgv_ref, zgg_ref, zs_ref,
                  conv0_ref, c0_ref, n0_ref, m0_ref, s0_ref,
                  cw_ref, cb_ref, gm_ref, w2_ref, ba2_ref, gg_ref, lv_ref, shift_ref,
                  ya_ref, yb_ref, tail_ref, c_ref, n_ref, m_ref, st_ref, *, L):
    @pl.when(pl.program_id(1) == 0)
    def _():
        tail_ref[...] = conv0_ref[...]
        c_ref[...] = c0_ref[...]
        n_ref[...] = n0_ref[...]
        m_ref[...] = m0_ref[...]
        st_ref[...] = s0_ref[...]

    lv = lv_ref[...]
    causal = lv >= -1

    u_b = zqk_ref[...]
    u = u_b.astype(F32)
    tail = tail_ref[0]
    row8 = lax.broadcasted_iota(jnp.int32, (SUBLANES, 2 * M_DIM), 0)
    acc = cb_ref[...] + cw_ref[CONV_W - 1:CONV_W, :] * u
    for w in range(CONV_W - 1):
        delay = CONV_W - 1 - w
        shifted = _dot(shift_ref[w], u_b)
        head = jnp.where(row8 < delay, pltpu.roll(tail, delay, 0), 0.0)
        shifted = jnp.concatenate([shifted[0:SUBLANES] + head, shifted[SUBLANES:]], axis=0)
        acc = acc + cw_ref[w:w + 1, :] * shifted
    tail_ref[0] = u[L - SUBLANES:L, :]
    qk = _silu(acc)

    zs = zs_ref[...]
    row = lax.broadcasted_iota(jnp.int32, (L, LANES), 0)
    col = lax.broadcasted_iota(jnp.int32, (L, LANES), 1)
    bcs = _log_sigmoid(zs)
    sh = 1
    while sh < L:
        bcs = bcs + jnp.where(row >= sh, pltpu.roll(bcs, sh, 0), 0.0)
        sh *= 2
    gates = jnp.where(col < M_HEADS, zs, bcs)
    if L < LANES:
        gates = jnp.concatenate([gates, jnp.zeros((LANES - L, LANES), F32)], axis=0)
    gt = gates.T[:, :L]

    heads = range(M_HEADS)
    hs = [slice(h * M_HD, (h + 1) * M_HD) for h in heads]
    ig_col = [zs[:, h:h + 1] for h in heads]
    b_col = [bcs[:, M_HEADS + h:M_HEADS + h + 1] for h in heads]
    m_prev = [m_ref[0, h][0:1, 0:1] for h in heads]
    logd = [jnp.where(causal, b_col[h] - gt[M_HEADS + h:M_HEADS + h + 1, :] + gt[h:h + 1, :], -jnp.inf) for h in heads]
    inter = [b_col[h] + m_prev[h] for h in heads]
    m_t = [jnp.maximum(inter[h], jnp.max(logd[h], axis=-1, keepdims=True)) for h in heads]
    q_f = [qk[:, hs[h]] for h in heads]
    k_f = [qk[:, M_DIM + h * M_HD:M_DIM + (h + 1) * M_HD] * (M_HD ** -0.5) for h in heads]
    q_b = [q.astype(BF16) for q in q_f]
    v_b = [zv_ref[:, hs[h]] for h in heads]
    s_raw = [_dot_nt(q_b[h], k_f[h].astype(BF16)) for h in heads]
    c_old = [c_ref[0, h] for h in heads]
    n_old = [n_ref[0, h] for h in heads]
    qc = [_dot(q_b[h], c_old[h].astype(BF16)) for h in heads]
    s = [s_raw[h] * jnp.exp(logd[h] - m_t[h]) for h in heads]
    w_int = [jnp.exp(inter[h] - m_t[h]) for h in heads]
    num = [_dot(s[h].astype(BF16), v_b[h]) + w_int[h] * qc[h] for h in heads]
    den = [jnp.sum(s[h], axis=-1, keepdims=True) + w_int[h] * jnp.sum(q_f[h] * n_old[h], axis=-1, keepdims=True)
           for h in heads]
    hh = [num[h] * (1.0 / jnp.maximum(jnp.abs(den[h]), jnp.exp(-m_t[h]))) for h in heads]
    for h in heads:
        m_new = m_t[h][L - 1:L, :]
        b_last = b_col[h][L - 1:L, :]
        w_k = jnp.exp(b_last - b_col[h] + ig_col[h] - m_new)
        dec = jnp.exp(b_last + m_prev[h] - m_new)
        k_w = k_f[h] * w_k
        c_ref[0, h] = dec * c_old[h] + _dot_tn(k_w.astype(BF16), v_b[h])
        n_ref[0, h] = dec * n_old[h] + jnp.sum(k_w, axis=0, keepdims=True)
        m_ref[0, h] = jnp.broadcast_to(m_new, (SUBLANES, LANES))
    for h in heads:
        hn = _rms(hh[h], gm_ref[:, hs[h]])
        ya_ref[:, hs[h]] = (hn * _sigmoid(zo_ref[:, hs[h]].astype(F32))).astype(BF16)

    la = _log_sigmoid(_dot(zs.astype(BF16), w2_ref[...]) + ba2_ref[...]) * (1.0 / G_TAU)
    fwd, rex = _decay_products(jnp.exp(la), L)
    nlev = len(fwd) - 1
    qg = zgq_ref[...].astype(F32) * (G_DK ** -0.5)
    kg = zgk_ref[...].astype(F32)
    q_d = qg.astype(BF16)
    k_d = zgk_ref[...]
    qs = [(qg * fwd[j]).astype(BF16) for j in range(nlev)]
    ks = [k_d] + [(kg * rex[j]).astype(BF16) for j in range(1, nlev)]
    q_in = (qg * fwd[nlev]).astype(BF16)
    k_out = (kg * rex[nlev]).astype(BF16)
    dec_all = fwd[nlev][L - 1:L, :]
    gheads = range(G_HEADS)
    ks_ = [slice(h * G_DK, (h + 1) * G_DK) for h in gheads]
    vs_ = [slice(h * G_DV, (h + 1) * G_DV) for h in gheads]
    att = [jnp.where(lv == -1, _dot_nt(q_d[:, ks_[h]], k_d[:, ks_[h]]), 0.0) for h in gheads]
    for j in range(nlev):
        for h in gheads:
            att[h] = jnp.where(lv == j, _dot_nt(qs[j][:, ks_[h]], ks[j][:, ks_[h]]), att[h])
    v_g = [zgv_ref[:, vs_[h]] for h in gheads]
    st_old = [st_ref[0, h] for h in gheads]
    o = [_dot(att[h].astype(BF16), v_g[h]) + _dot_nt(q_in[:, ks_[h]], st_old[h].astype(BF16)) for h in gheads]
    for h in gheads:
        st_ref[0, h] = st_old[h] * dec_all[:, ks_[h]] + _dot_tn(v_g[h], k_out[:, ks_[h]])
    for h in gheads:
        on = _rms(o[h], gg_ref[:, vs_[h]])
        g = zgg_ref[:, vs_[h]].astype(F32)
        yb_ref[:, vs_[h]] = (on * _silu(g)).astype(BF16)


def _mixers(z, zs, conv0, c0, n0, m0, st0, cw, cb, gm, w2, ba2, gg, batch, T, L):
    nc = T // L
    lv = jnp.asarray(_level_matrix(L))
    shift = jnp.asarray(np.stack([np.eye(L, k=-(CONV_W - 1 - w)) for w in range(CONV_W - 1)]), BF16)
    row_blk = lambda width, cidx: pl.BlockSpec((L, width), lambda b, c: (b * nc + c, cidx))
    state = lambda *shape: pl.BlockSpec((1,) + shape, lambda b, c: (b,) + (0,) * len(shape))
    const = lambda *shape: pl.BlockSpec(shape, lambda b, c: (0,) * len(shape))
    n_tok = batch * T
    return pl.pallas_call(
        functools.partial(_mixer_kernel, L=L),
        grid=(batch, nc),
        in_specs=[
            row_blk(2 * M_DIM, 0),
            row_blk(M_DIM, 2),
            row_blk(M_DIM, 3),
            row_blk(G_KDIM, 8),
            row_blk(G_KDIM, 9),
            row_blk(G_VDIM, 5),
            row_blk(G_VDIM, 6),
            pl.BlockSpec((L, LANES), lambda b, c: (b * nc + c, 0)),
            state(SUBLANES, 2 * M_DIM),
            state(M_HEADS, M_HD, M_HD),
            state(M_HEADS, 1, M_HD),
            state(M_HEADS, SUBLANES, LANES),
            state(G_HEADS, G_DV, G_DK),
            const(CONV_W, 2 * M_DIM),
            const(1, 2 * M_DIM),
            const(1, M_DIM),
            const(LANES, G_KDIM),
            const(1, G_KDIM),
            const(1, G_VDIM),
            const(L, L),
            const(CONV_W - 1, L, L),
        ],
        out_specs=[
            pl.BlockSpec((L, M_DIM), lambda b, c: (b * nc + c, 0)),
            pl.BlockSpec((L, G_VDIM), lambda b, c: (b * nc + c, 0)),
            state(SUBLANES, 2 * M_DIM),
            state(M_HEADS, M_HD, M_HD),
            state(M_HEADS, 1, M_HD),
            state(M_HEADS, SUBLANES, LANES),
            state(G_HEADS, G_DV, G_DK),
        ],
        out_shape=[
            jax.ShapeDtypeStruct((n_tok, M_DIM), BF16),
            jax.ShapeDtypeStruct((n_tok, G_VDIM), BF16),
            jax.ShapeDtypeStruct((batch, SUBLANES, 2 * M_DIM), F32),
            jax.ShapeDtypeStruct((batch, M_HEADS, M_HD, M_HD), F32),
            jax.ShapeDtypeStruct((batch, M_HEADS, 1, M_HD), F32),
            jax.ShapeDtypeStruct((batch, M_HEADS, SUBLANES, LANES), F32),
            jax.ShapeDtypeStruct((batch, G_HEADS, G_DV, G_DK), F32),
        ],
        compiler_params=_params("parallel", "arbitrary"),
        name="mixers",
    )(z, z, z, z, z, z, z, zs, conv0, c0, n0, m0, st0, cw, cb, gm, w2, ba2, gg, lv, shift)


def _outproj_kernel(ya_p, yb_p, ua_p, ub_p, x_p, ya_s, yb_s, ua_s, ub_s, x_s,
                    wpa_ref, wpb_ref, wout_ref, gffn_ref, wr_ref, br_ref, tril_ref,
                    x1p_ref, x1s_ref, xn_ref, eop_ref, eos_ref, wop_ref, wos_ref, cnt_ref, *, tm, steps_p):
    i = pl.program_id(0)

    @pl.when(i == 0)
    def _():
        cnt_ref[...] = jnp.zeros_like(cnt_ref)

    def chains(ins, parts, x1_ref, eo_ref, wo_ref):
        ya_ref, yb_ref, ua_ref, ub_ref, x_ref = ins
        a = [_dot(ya_ref[rs, :], wpa_ref[...]) for rs in parts]
        b = [_dot(yb_ref[rs, :], wpb_ref[...]) for rs in parts]
        merged = [_sigmoid(ua_ref[rs, :].astype(F32)) * a_ + _sigmoid(ub_ref[rs, :].astype(F32)) * b_
                  for rs, a_, b_ in zip(parts, a, b)]
        x1 = [x_ref[rs, :] + _dot(m.astype(BF16), wout_ref[...]) for rs, m in zip(parts, merged)]
        xn = []
        for h, (rs, x1_) in enumerate(zip(parts, x1)):
            x1_ref[rs, :] = x1_
            xn_ = _rms(x1_, gffn_ref[...])
            for j in range(ROW_CHUNKS):
                xn_ref[pl.ds(h * tm * ROW_CHUNKS + j, tm, stride=ROW_CHUNKS), :] = xn_[:, j * LANES:(j + 1) * LANES]
            xn.append(xn_)

        ls = [_dot(xn_.astype(BF16), wr_ref[...]) + br_ref[...] for xn_ in xn]
        lane = lax.broadcasted_iota(jnp.int32, (tm, LANES), 1).astype(F32)
        vals = [[] for _ in parts]
        idxs = [[] for _ in parts]
        for _ in range(TOP_K):
            for h in range(len(parts)):
                mx = jnp.max(ls[h], axis=-1, keepdims=True)
                ix = jnp.min(jnp.where(ls[h] == mx, lane, float(LANES)), axis=-1, keepdims=True)
                vals[h].append(mx)
                idxs[h].append(ix)
                ls[h] = jnp.where(lane == ix, -jnp.inf, ls[h])

        carry = cnt_ref[0:1, :]
        for h, rs in enumerate(parts):
            ev = [jnp.exp(v - vals[h][0]) for v in vals[h]]
            den = ev[0] + ev[1] + ev[2] + ev[3]
            cnt = jnp.zeros((tm, LANES), F32)
            for ix in idxs[h]:
                cnt = cnt + jnp.where(lane == ix, 1.0, 0.0)
            pre = _dot(tril_ref[...], cnt.astype(BF16)) + carry
            carry = carry + jnp.sum(cnt, axis=0, keepdims=True)
            eo = jnp.zeros((tm, LANES), F32)
            wo = jnp.zeros((tm, LANES), F32)
            for k in range(TOP_K):
                rank = jnp.sum(jnp.where(lane == idxs[h][k], pre, 0.0), axis=-1, keepdims=True)
                eo = jnp.where(lane == float(k), idxs[h][k], eo)
                eo = jnp.where(lane == float(TOP_K + k), rank, eo)
                wo = jnp.where(lane == float(k), ev[k] / den, wo)
            eo_ref[:, rs] = eo.T[0:SUBLANES, :].astype(jnp.int32)
            wo_ref[rs, :] = wo
        cnt_ref[...] = jnp.broadcast_to(carry, cnt_ref.shape)

    prompt = (ya_p, yb_p, ua_p, ub_p, x_p)
    sample = (ya_s, yb_s, ua_s, ub_s, x_s)

    @pl.when(i < steps_p)
    def _():
        chains(prompt, [slice(0, tm), slice(tm, 2 * tm)], x1p_ref, eop_ref, wop_ref)

    @pl.when(i == steps_p)
    def _():
        chains(sample, [slice(0, tm)], x1s_ref, eos_ref, wos_ref)
        xn_ref[pl.ds(tm * ROW_CHUNKS, tm * ROW_CHUNKS), :] = jnp.zeros((tm * ROW_CHUNKS, LANES), F32)


def _outproj(prompt, sample, wpa, wpb, wout, gffn, wr, br, tm):
    n_p, n_s = prompt[3].shape[0], sample[3].shape[0]
    assert n_s == tm and n_p % (2 * tm) == 0
    steps_p = n_p // (2 * tm)
    tril = jnp.asarray(np.tril(np.ones((tm, tm), np.float32), -1), BF16)
    pidx = lambda i: jnp.minimum(i, steps_p - 1)
    src = lambda rows, idx: [pl.BlockSpec((rows, M_DIM), lambda i: (idx(i), 0)),
                             pl.BlockSpec((rows, G_VDIM), lambda i: (idx(i), 0)),
                             pl.BlockSpec((rows, D_MODEL), lambda i: (idx(i), 7)),
                             pl.BlockSpec((rows, D_MODEL), lambda i: (idx(i), 8)),
                             pl.BlockSpec((rows, D_MODEL), lambda i: (idx(i), 0))]
    const = lambda *shape: pl.BlockSpec(shape, lambda i: (0,) * len(shape))
    ya_p, yb_p, z_p, x_p = prompt
    ya_s, yb_s, z_s, x_s = sample
    return pl.pallas_call(
        functools.partial(_outproj_kernel, tm=tm, steps_p=steps_p),
        grid=(steps_p + 1,),
        in_specs=src(2 * tm, pidx) + src(tm, lambda i: 0) + [
            const(M_DIM, D_MODEL), const(G_VDIM, D_MODEL), const(D_MODEL, D_MODEL), const(1, D_MODEL),
            const(D_MODEL, LANES), const(1, LANES), const(tm, tm),
        ],
        out_specs=[
            pl.BlockSpec((2 * tm, D_MODEL), lambda i: (pidx(i), 0)),
            const(tm, D_MODEL),
            pl.BlockSpec((2 * tm * ROW_CHUNKS, LANES), lambda i: (i, 0)),
            pl.BlockSpec((SUBLANES, 2 * tm), lambda i: (0, pidx(i))),
            const(SUBLANES, tm),
            pl.BlockSpec((2 * tm, LANES), lambda i: (pidx(i), 0)),
            const(tm, LANES),
            const(SUBLANES, LANES),
        ],
        out_shape=[
            jax.ShapeDtypeStruct((n_p, D_MODEL), F32),
            jax.ShapeDtypeStruct((n_s, D_MODEL), F32),
            jax.ShapeDtypeStruct(((n_p + 2 * tm) * ROW_CHUNKS, LANES), F32),
            jax.ShapeDtypeStruct((SUBLANES, n_p), jnp.int32),
            jax.ShapeDtypeStruct((SUBLANES, n_s), jnp.int32),
            jax.ShapeDtypeStruct((n_p, LANES), F32),
            jax.ShapeDtypeStruct((n_s, LANES), F32),
            jax.ShapeDtypeStruct((SUBLANES, LANES), F32),
        ],
        compiler_params=_params("arbitrary"),
        name="outproj_router",
    )(ya_p, yb_p, z_p, z_p, x_p, ya_s, yb_s, z_s, z_s, x_s, wpa, wpb, wout, gffn, wr, br, tril)


def _row_copy(src, src_row, dst, dst_row, sem):
    return pltpu.make_async_copy(
        src.at[pl.ds(pl.multiple_of(src_row * ROW_CHUNKS, ROW_CHUNKS), ROW_CHUNKS)],
        dst.at[pl.ds(pl.multiple_of(dst_row * ROW_CHUNKS, ROW_CHUNKS), ROW_CHUNKS)], sem)


def _dispatch_kernel(pos_ref, fill_ref, xn_ref, xs_ref, zeros_ref, zsem, sem, *, tm, n, n_blocks):
    i = pl.program_id(0)
    base = i * tm

    def issue(r, carry):
        for k in range(TOP_K):
            _row_copy(xn_ref, r, xs_ref, pos_ref[k * n + base + r], sem).start(priority=k % 2)
        return carry

    lax.fori_loop(0, tm, issue, 0)
    for _ in range(TOP_K):
        pltpu.make_async_copy(xn_ref, xs_ref.at[pl.ds(0, tm * ROW_CHUNKS)], sem).wait()

    @pl.when(i == 0)
    def _():
        zeros_ref[...] = jnp.zeros_like(zeros_ref)
        bits = [1 << b for b in range(MOE_ROWS.bit_length() - 1)]

        def zero_copy(off, nrows):
            return pltpu.make_async_copy(
                zeros_ref.at[pl.ds(0, nrows * ROW_CHUNKS)],
                xs_ref.at[pl.ds(pl.multiple_of(off * ROW_CHUNKS, ROW_CHUNKS), nrows * ROW_CHUNKS)], zsem)

        def expert(e, carry):
            npad = fill_ref[N_EXPERTS + e]
            for wait in (False, True):
                off = fill_ref[e]
                for bit in bits:
                    @pl.when((npad & bit) != 0)
                    def _():
                        zero_copy(off, bit).wait() if wait else zero_copy(off, bit).start()
                    off = off + (npad & bit)
            return carry

        lax.fori_loop(0, N_EXPERTS, expert, 0)

        def block(b, carry):
            zero_copy(b * MOE_ROWS, MOE_ROWS).start()
            zero_copy(b * MOE_ROWS, MOE_ROWS).wait()
            return carry

        lax.fori_loop(fill_ref[2 * N_EXPERTS], n_blocks, block, 0)


def _dispatch(pos, fill, xn, n, n_blocks, tm):
    return pl.pallas_call(
        functools.partial(_dispatch_kernel, tm=tm, n=n, n_blocks=n_blocks),
        grid_spec=pltpu.PrefetchScalarGridSpec(
            num_scalar_prefetch=2,
            grid=(n // tm,),
            in_specs=[pl.BlockSpec((tm * ROW_CHUNKS, LANES), lambda i, *_: (i, 0))],
            out_specs=pl.BlockSpec(memory_space=pl.ANY),
            scratch_shapes=[pltpu.VMEM((MOE_ROWS * ROW_CHUNKS, LANES), F32), pltpu.SemaphoreType.DMA(()),
                            pltpu.SemaphoreType.DMA(())],
        ),
        out_shape=jax.ShapeDtypeStruct((n_blocks * MOE_ROWS * ROW_CHUNKS, LANES), F32),
        compiler_params=_params("arbitrary"),
        name="dispatch",
    )(pos, fill, xn)


def _expert_kernel(plan_ref, nused_ref, xs_ref, wgu_hbm, bgu_ref, wd_hbm, bd_ref, y_ref,
                   wgu_f32, wd_f32, wgu_bf, wd_bf, wsem, *, rows):
    i = pl.program_id(0)
    active = i < nused_ref[0]
    expert, next_expert, slot = plan_ref[0, i], plan_ref[1, i], plan_ref[2, i]
    first = jnp.logical_or(i == 0, expert != plan_ref[0, jnp.maximum(i - 1, 0)])

    def weight_copies(e, s):
        return (pltpu.make_async_copy(wgu_hbm.at[e], wgu_f32.at[s], wsem.at[0, s]),
                pltpu.make_async_copy(wd_hbm.at[e], wd_f32.at[s], wsem.at[1, s]))

    @pl.when(i == 0)
    def _():
        for c in weight_copies(expert, slot):
            c.start()

    @pl.when(jnp.logical_and(active, first))
    def _():
        for c in weight_copies(expert, slot):
            c.wait()
        wgu_bf[...] = wgu_f32[slot].astype(BF16)
        wd_bf[...] = wd_f32[slot].astype(BF16)

        @pl.when(next_expert < N_EXPERTS)
        def _():
            for c in weight_copies(next_expert, 1 - slot):
                c.start()

    @pl.when(active)
    def _():
        x = jnp.concatenate([xs_ref[pl.ds(j, rows, stride=ROW_CHUNKS), :] for j in range(ROW_CHUNKS)], axis=-1)
        zz = _dot(x.astype(BF16), wgu_bf[...]) + bgu_ref[...]
        g = jnp.minimum(zz[:, :D_FF], SWIGLU_LIMIT)
        u = jnp.clip(zz[:, D_FF:], -SWIGLU_LIMIT, SWIGLU_LIMIT)
        act = g * _sigmoid(SWIGLU_ALPHA * g) * (u + 1.0)
        y = _dot(act.astype(BF16), wd_bf[...]) + bd_ref[...]
        for j in range(ROW_CHUNKS):
            y_ref[pl.ds(j, rows, stride=ROW_CHUNKS), :] = y[:, j * LANES:(j + 1) * LANES]

    @pl.when(jnp.logical_not(active))
    def _():
        y_ref[...] = jnp.zeros_like(y_ref)


def _experts(plan, n_used, xs, wgu, bgu, wd, bd, rows):
    nb = plan.shape[1]
    return pl.pallas_call(
        functools.partial(_expert_kernel, rows=rows),
        grid_spec=pltpu.PrefetchScalarGridSpec(
            num_scalar_prefetch=2,
            grid=(nb,),
            in_specs=[
                pl.BlockSpec((rows * ROW_CHUNKS, LANES), lambda i, plan, nu: (i, 0)),
                pl.BlockSpec(memory_space=pl.ANY),
                pl.BlockSpec((None, 1, 2 * D_FF), lambda i, plan, nu: (plan[0, i], 0, 0)),
                pl.BlockSpec(memory_space=pl.ANY),
                pl.BlockSpec((None, 1, D_MODEL), lambda i, plan, nu: (plan[0, i], 0, 0)),
            ],
            out_specs=pl.BlockSpec((rows * ROW_CHUNKS, LANES), lambda i, plan, nu: (i, 0)),
            scratch_shapes=[pltpu.VMEM((2, D_MODEL, 2 * D_FF), F32), pltpu.VMEM((2, D_FF, D_MODEL), F32),
                            pltpu.VMEM((D_MODEL, 2 * D_FF), BF16), pltpu.VMEM((D_FF, D_MODEL), BF16),
                            pltpu.SemaphoreType.DMA((2, 2))],
        ),
        out_shape=jax.ShapeDtypeStruct(xs.shape, F32),
        compiler_params=_params("arbitrary"),
        name="experts",
    )(plan, n_used, xs, wgu, bgu, wd, bd)


def _combine_kernel(pos_ref, ye_ref, x1_ref, wo_ref, p_ref, wproj_ref, wgate_ref, gple_ref, gfin_ref,
                    y_ref, rows_a, rows_b, sem, *, tm, nt, n_all, tok0):
    i = pl.program_id(0)
    seg = tm * ROW_CHUNKS

    def start_row(tok, r, dst, sem_slot):
        for k in range(TOP_K):
            srow = pl.multiple_of(pos_ref[k * n_all + tok] * ROW_CHUNKS, ROW_CHUNKS)
            pltpu.make_async_copy(ye_ref.at[pl.ds(srow, ROW_CHUNKS)],
                                  dst.at[pl.ds((k * tm + r) * ROW_CHUNKS, ROW_CHUNKS)], sem_slot).start(priority=k % 2)

    def wait_rows(dst, sem_slot):
        for k in range(TOP_K):
            pltpu.make_async_copy(ye_ref.at[pl.ds(0, seg)], dst.at[pl.ds(k * seg, seg)], sem_slot).wait()

    def compute(half, src):
        rs = slice(half * tm, (half + 1) * tm)
        wo = wo_ref[rs, :]
        moe = jnp.zeros((tm, D_MODEL), F32)
        for k in range(TOP_K):
            rows = jnp.concatenate(
                [src[pl.ds(k * seg + j, tm, stride=ROW_CHUNKS), :] for j in range(ROW_CHUNKS)], axis=-1)
            moe = moe + rows * wo[:, k:k + 1]
        x2 = x1_ref[rs, :] + moe
        gate = _sigmoid(_dot(_rms(x2, gple_ref[...]).astype(BF16), wgate_ref[...]))
        x3 = x2 + _dot(p_ref[rs, :].astype(BF16), wproj_ref[...]) * gate
        y_ref[rs, :] = _rms(x3, gfin_ref[...])

    @pl.when(i == 0)
    def _():
        def issue(r, carry):
            start_row(tok0 + r, r, rows_a, sem.at[0])
            return carry
        lax.fori_loop(0, tm, issue, 0)

    wait_rows(rows_a, sem.at[0])
    base_b = tok0 + (2 * i + 1) * tm
    for r in range(tm):
        start_row(base_b + r, r, rows_b, sem.at[1])
    compute(0, rows_a)

    wait_rows(rows_b, sem.at[1])
    base_a = tok0 + jnp.minimum(2 * i + 2, nt - 1) * tm
    for r in range(tm):
        start_row(base_a + r, r, rows_a, sem.at[0])
    compute(1, rows_b)

    @pl.when(i == nt // 2 - 1)
    def _():
        wait_rows(rows_a, sem.at[0])


def _combine(pos_flat, ye, x1, wo, p, wproj, wgate, gple, gfin, tm, tok0, n_all):
    n = p.shape[0]
    nt = n // tm
    assert nt % 2 == 0
    tok = lambda width: pl.BlockSpec((2 * tm, width), lambda i, pos: (i, 0))
    const = lambda *shape: pl.BlockSpec(shape, lambda i, pos: (0,) * len(shape))
    rows_buf = pltpu.VMEM((TOP_K * tm * ROW_CHUNKS, LANES), F32)
    return pl.pallas_call(
        functools.partial(_combine_kernel, tm=tm, nt=nt, n_all=n_all, tok0=tok0),
        grid_spec=pltpu.PrefetchScalarGridSpec(
            num_scalar_prefetch=1,
            grid=(nt // 2,),
            in_specs=[
                pl.BlockSpec(memory_space=pl.ANY),
                tok(D_MODEL), tok(LANES), tok(PLE_DIM),
                const(PLE_DIM, D_MODEL), const(D_MODEL, D_MODEL), const(1, D_MODEL), const(1, D_MODEL),
            ],
            out_specs=tok(D_MODEL),
            scratch_shapes=[rows_buf, rows_buf, pltpu.SemaphoreType.DMA((2,))],
        ),
        out_shape=jax.ShapeDtypeStruct((n, D_MODEL), F32),
        compiler_params=_params("arbitrary"),
        name="combine_ple",
    )(pos_flat, ye, x1, wo, p, wproj, wgate, gple, gfin)


def _tile(n, pref):
    t = pref
    while n % t:
        t //= 2
    return t


def _seg(a, i):
    return a[..., _OFF[i]:_OFF[i + 1]]


def kernel(x_prompt, x_sample, state_conv, state_mlstm_C, state_mlstm_n, state_mlstm_m, state_gla_S, p_prompt, p_sample, g_mix, w_in, b_in, conv_w, conv_b, g_mnorm, w_a2, b_a2, g_gnorm, w_pa, w_pb, w_out, g_ffn, w_router, b_router, w_gate_up, b_gate_up, w_down, b_down, g_ple, w_ple_proj, w_ple_gate, g_final):
    assert g_mix.shape[0] == 1, "single-layer kernel"
    bp, tp, _ = x_prompt.shape
    bs, ts, _ = x_sample.shape

    big = (0, 1, 2, 5, 6, 7, 8, 10, 11, 12)
    w_big = jnp.concatenate([_seg(w_in[0], i) for i in big], axis=-1).astype(BF16)
    b_big = jnp.concatenate([_seg(b_in[0], i) for i in big], axis=-1)[None, :]
    small = (3, 4, 9)
    n_small = sum(SPLITS[i] for i in small)
    w_small = jnp.pad(jnp.concatenate([_seg(w_in[0], i) for i in small], axis=-1),
                      ((0, 0), (0, LANES - n_small))).astype(BF16)
    b_small = jnp.pad(jnp.concatenate([_seg(b_in[0], i) for i in small], axis=-1), (0, LANES - n_small))[None, :]
    w2 = jnp.pad(w_a2[0], ((2 * M_HEADS, LANES - 2 * M_HEADS - G_RANK), (0, 0))).astype(BF16)
    wr = jnp.pad(w_router[0], ((0, 0), (0, LANES - N_EXPERTS))).astype(BF16)
    br = jnp.pad(b_router[0], (0, LANES - N_EXPERTS), constant_values=NEG_BIG)[None, :]
    wpa, wpb, wout = w_pa[0].astype(BF16), w_pb[0].astype(BF16), w_out[0].astype(BF16)
    wproj, wgate = w_ple_proj[0].astype(BF16), w_ple_gate[0].astype(BF16)
    row = lambda a: a.reshape(1, -1)

    def mix_path(x3d, conv0, c0, n0, m0, s0):
        batch, T, _ = x3d.shape
        x = x3d.reshape(batch * T, D_MODEL)
        z, zs = _inproj(x, row(g_mix[0]), w_big, b_big, w_small, b_small, _tile(batch * T, 2048))
        L = min(MIX_CHUNK, T)
        conv0p = jnp.pad(conv0, ((0, 0), (SUBLANES - (CONV_W - 1), 0), (0, 0)))
        m0b = jnp.broadcast_to(m0[:, :, None, None], (batch, M_HEADS, SUBLANES, LANES))
        ya, yb, tail, c1, n1, m1, st1 = _mixers(
            z, zs, conv0p, c0, n0[:, :, None, :], m0b, jnp.swapaxes(s0, -1, -2),
            conv_w[0], row(conv_b[0]), row(g_mnorm[0]), w2, row(b_a2[0]), row(g_gnorm[0]), batch, T, L)
        new_state = (tail[:, SUBLANES - (CONV_W - 1):, :], c1, n1[:, :, 0, :], m1[:, :, 0, 0],
                     jnp.swapaxes(st1, -1, -2))
        return x, z, ya, yb, new_state

    zero_state = (jnp.zeros((bp, CONV_W - 1, 2 * M_DIM), F32), jnp.zeros((bp, M_HEADS, M_HD, M_HD), F32),
                  jnp.zeros((bp, M_HEADS, M_HD), F32), jnp.zeros((bp, M_HEADS), F32),
                  jnp.zeros((bp, G_HEADS, G_DK, G_DV), F32))
    xp, zp, yap, ybp, st_p = mix_path(x_prompt, *zero_state)
    xs_, zs_, yas, ybs, st_s = mix_path(x_sample, state_conv[0], state_mlstm_C[0], state_mlstm_n[0],
                                        state_mlstm_m[0], state_gla_S[0])

    n_p, n_s = xp.shape[0], xs_.shape[0]
    n_all = n_p + n_s
    tm = n_s
    x1p, x1s, xn, eop, eos, wop, wos, cnt = _outproj((yap, ybp, zp, xp), (yas, ybs, zs_, xs_), wpa, wpb, wout,
                                                     row(g_ffn[0]), wr, br, tm)
    eo = jnp.concatenate([eop, eos], axis=1)

    counts = cnt[0, :N_EXPERTS].astype(jnp.int32)
    padded = (counts + MOE_ROWS - 1) // MOE_ROWS * MOE_ROWS
    pend = jnp.cumsum(padded)
    pstart = pend - padded
    nk = n_all * TOP_K
    n_blocks = (nk + N_EXPERTS * (MOE_ROWS - 1) + MOE_ROWS - 1) // MOE_ROWS
    starts = jnp.arange(n_blocks, dtype=jnp.int32) * MOE_ROWS
    block_e = jnp.minimum(jnp.sum((pend[None, :] <= starts[:, None]).astype(jnp.int32), axis=1), N_EXPERTS - 1)
    ids = jnp.arange(N_EXPERTS, dtype=jnp.int32)
    later = jnp.where((ids[None, :] > ids[:, None]) & (counts[None, :] > 0), ids[None, :], N_EXPERTS)
    next_e = jnp.min(later, axis=1)
    group = jnp.cumsum(jnp.concatenate([jnp.zeros((1,), jnp.int32),
                                        (block_e[1:] != block_e[:-1]).astype(jnp.int32)]))
    plan = jnp.stack([block_e, next_e[block_e], group % 2]).astype(jnp.int32)
    n_used = (pend[-1:] // MOE_ROWS).astype(jnp.int32)
    e, rank = eo[:TOP_K], eo[TOP_K:2 * TOP_K]
    onehot = e[..., None] == jnp.arange(N_EXPERTS, dtype=jnp.int32)
    pos = (jnp.sum(jnp.where(onehot, pstart, 0), axis=-1) + rank).reshape(-1)
    fill = jnp.concatenate([pstart + counts, padded - counts, n_used]).astype(jnp.int32)

    tm_d = DISPATCH_ROWS if n_all % DISPATCH_ROWS == 0 else tm
    xs_sorted = _dispatch(pos, fill, xn, n_all, n_blocks, tm_d)
    ye = _experts(plan, n_used, xs_sorted, w_gate_up[0], b_gate_up[0][:, None, :], w_down[0],
                  b_down[0][:, None, :], MOE_ROWS)
    fin = lambda x1, wo, p, tok0, tmc: _combine(pos, ye, x1, wo, p.reshape(-1, PLE_DIM), wproj, wgate,
                                                row(g_ple[0]), row(g_final), tmc, tok0, n_all)
    y_p = fin(x1p, wop, p_prompt[0], 0, tm).reshape(bp, tp, D_MODEL)
    y_s = fin(x1s, wos, p_sample[0], n_p, _tile(n_s // 2, tm)).reshape(bs, ts, D_MODEL)

    lead = lambda a: a[None]
    return (y_p, y_s) + tuple(lead(a) for a in st_p) + tuple(lead(a) for a in st_s)
```

```python
import functools

import numpy as np
import jax
import jax.numpy as jnp
from jax import lax
from jax.experimental import pallas as pl
from jax.experimental.pallas import tpu as pltpu

F32 = jnp.float32
BF16 = jnp.bfloat16

D_MODEL = 1024
EPS = 1e-6
M_HEADS = 4
M_DIM = D_MODEL
M_HD = M_DIM // M_HEADS
CONV_W = 4
G_HEADS = 4
G_KDIM = D_MODEL // 2
G_VDIM = D_MODEL
G_DK = G_KDIM // G_HEADS
G_DV = G_VDIM // G_HEADS
G_RANK = 16
G_TAU = 16.0
N_EXPERTS = 32
TOP_K = 4
D_FF = D_MODEL
SWIGLU_LIMIT = 7.0
SWIGLU_ALPHA = 1.702
PLE_DIM = 256
SPLITS = (M_DIM, M_DIM, M_DIM, M_HEADS, M_HEADS, M_DIM, G_KDIM, G_KDIM, G_VDIM, G_RANK, G_VDIM, D_MODEL, D_MODEL)
_OFF = np.concatenate([[0], np.cumsum(SPLITS)]).tolist()

LANES = 128
SUBLANES = 8
ROW_CHUNKS = D_MODEL // LANES
VMEM_LIMIT = 56 * 1024 * 1024
MIX_CHUNK = 256
MOE_ROWS = 512
DISPATCH_ROWS = 1280
NEG_BIG = -1e30


def _sigmoid(x):
    return 0.5 * jnp.tanh(0.5 * x) + 0.5


def _silu(x):
    h = 0.5 * x
    return h * jnp.tanh(h) + h


def _log_sigmoid(x):
    return jnp.minimum(x, 0.0) - jnp.log(1.0 + jnp.exp(-jnp.abs(x)))


def _rms(x, g):
    return x * lax.rsqrt(jnp.mean(x * x, axis=-1, keepdims=True) + EPS) * g


def _dot(a, b):
    return jnp.dot(a, b, preferred_element_type=F32)


def _dot_nt(a, b):
    return lax.dot_general(a, b, (((1,), (1,)), ((), ())), preferred_element_type=F32)


def _dot_tn(a, b):
    return lax.dot_general(a, b, (((0,), (0,)), ((), ())), preferred_element_type=F32)


def _params(*sem):
    return pltpu.CompilerParams(dimension_semantics=sem, vmem_limit_bytes=VMEM_LIMIT)


def _inproj_kernel(x_ref, g_ref, w_ref, b_ref, ws_ref, bs_ref, z_ref, zs_ref, h_scr):
    @pl.when(pl.program_id(1) == 0)
    def _():
        hb = _rms(x_ref[...], g_ref[...]).astype(BF16)
        h_scr[...] = hb
        zs_ref[...] = _dot(hb, ws_ref[...]) + bs_ref[...]

    z_ref[...] = (_dot(h_scr[...], w_ref[...]) + b_ref[...]).astype(BF16)


def _inproj(x, g, w_big, b_big, w_small, b_small, tm, tn=1024):
    n = x.shape[0]
    cols = w_big.shape[1]
    return pl.pallas_call(
        _inproj_kernel,
        grid=(n // tm, cols // tn),
        in_specs=[
            pl.BlockSpec((tm, D_MODEL), lambda i, j: (i, 0)),
            pl.BlockSpec((1, D_MODEL), lambda i, j: (0, 0)),
            pl.BlockSpec((D_MODEL, tn), lambda i, j: (0, j)),
            pl.BlockSpec((1, tn), lambda i, j: (0, j)),
            pl.BlockSpec((D_MODEL, LANES), lambda i, j: (0, 0)),
            pl.BlockSpec((1, LANES), lambda i, j: (0, 0)),
        ],
        out_specs=[
            pl.BlockSpec((tm, tn), lambda i, j: (i, j)),
            pl.BlockSpec((tm, LANES), lambda i, j: (i, 0)),
        ],
        out_shape=[jax.ShapeDtypeStruct((n, cols), BF16), jax.ShapeDtypeStruct((n, LANES), F32)],
        scratch_shapes=[pltpu.VMEM((tm, D_MODEL), BF16)],
        compiler_params=_params("parallel", "arbitrary"),
        name="inproj",
    )(x, g, w_big, b_big, w_small, b_small)


def _level_matrix(L):
    t = np.arange(L)[:, None]
    s = np.arange(L)[None, :]
    x = np.maximum(t ^ s, 1)
    lv = np.floor(np.log2(x)).astype(np.int32)
    lv = np.where(t == s, -1, lv)
    return np.where(s > t, -2, lv).astype(np.int32)


def _bcast_row(x, period, r):
    L, W = x.shape
    x3 = x.reshape(L // period, period, W)
    return jnp.broadcast_to(x3[:, r:r + 1, :], x3.shape).reshape(L, W)


def _decay_products(ela, L):
    W = ela.shape[1]
    row = lax.broadcasted_iota(jnp.int32, (L, W), 0)
    fwd = [ela]
    rex = [jnp.ones_like(ela)]
    b = 1
    while b < L:
        per = 2 * b
        res = row & (per - 1)
        f, r = fwd[-1], rex[-1]
        if per < SUBLANES:
            mulf = jnp.ones_like(ela)
            mulr = jnp.ones_like(ela)
            for j in range(b, per):
                mulf = jnp.where(res == j, pltpu.roll(f, j - (b - 1), 0), mulf)
            for j in range(0, b):
                mulr = jnp.where(res == j, pltpu.roll(f, L - (per - 1 - j), 0), mulr)
        else:
            mulf = jnp.where(res >= b, _bcast_row(f, per, b - 1), 1.0)
            mulr = jnp.where(res < b, _bcast_row(f, per, per - 1), 1.0)
        fwd.append(f * mulf)
        rex.append(r * mulr)
        b = per
    return fwd, rex


def _mixer_kernel(zqk_ref, zv_ref, zo_ref, zgq_ref, zgk_ref, zgv_ref, zgg_ref, zs_ref,
                  conv0_ref, c0_ref, n0_ref, m0_ref, s0_ref,
                  cw_ref, cb_ref, gm_ref, w2_ref, ba2_ref, gg_ref, lv_ref, shift_ref,
                  ya_ref, yb_ref, tail_ref, c_ref, n_ref, m_ref, st_ref, *, L):
    @pl.when(pl.program_id(1) == 0)
    def _():
        tail_ref[...] = conv0_ref[...]
        c_ref[...] = c0_ref[...]
        n_ref[...] = n0_ref[...]
        m_ref[...] = m0_ref[...]
        st_ref[...] = s0_ref[...]

    lv = lv_ref[...]
    causal = lv >= -1

    u_b = zqk_ref[...]
    u = u_b.astype(F32)
    tail = tail_ref[0]
    row8 = lax.broadcasted_iota(jnp.int32, (SUBLANES, 2 * M_DIM), 0)
    acc = cb_ref[...] + cw_ref[CONV_W - 1:CONV_W, :] * u
    for w in range(CONV_W - 1):
        delay = CONV_W - 1 - w
        shifted = _dot(shift_ref[w], u_b)
        head = jnp.where(row8 < delay, pltpu.roll(tail, delay, 0), 0.0)
        shifted = jnp.concatenate([shifted[0:SUBLANES] + head, shifted[SUBLANES:]], axis=0)
        acc = acc + cw_ref[w:w + 1, :] * shifted
    tail_ref[0] = u[L - SUBLANES:L, :]
    qk = _silu(acc)

    zs = zs_ref[...]
    row = lax.broadcasted_iota(jnp.int32, (L, LANES), 0)
    col = lax.broadcasted_iota(jnp.int32, (L, LANES), 1)
    bcs = _log_sigmoid(zs)
    sh = 1
    while sh < L:
        bcs = bcs + jnp.where(row >= sh, pltpu.roll(bcs, sh, 0), 0.0)
        sh *= 2
    gates = jnp.where(col < M_HEADS, zs, bcs)
    if L < LANES:
        gates = jnp.concatenate([gates, jnp.zeros((LANES - L, LANES), F32)], axis=0)
    gt = gates.T[:, :L]

    heads = range(M_HEADS)
    hs = [slice(h * M_HD, (h + 1) * M_HD) for h in heads]
    ig_col = [zs[:, h:h + 1] for h in heads]
    b_col = [bcs[:, M_HEADS + h:M_HEADS + h + 1] for h in heads]
    m_prev = [m_ref[0, h][0:1, 0:1] for h in heads]
    logd = [jnp.where(causal, b_col[h] - gt[M_HEADS + h:M_HEADS + h + 1, :] + gt[h:h + 1, :], -jnp.inf) for h in heads]
    inter = [b_col[h] + m_prev[h] for h in heads]
    m_t = [jnp.maximum(inter[h], jnp.max(logd[h], axis=-1, keepdims=True)) for h in heads]
    q_f = [qk[:, hs[h]] for h in heads]
    k_f = [qk[:, M_DIM + h * M_HD:M_DIM + (h + 1) * M_HD] * (M_HD ** -0.5) for h in heads]
    q_b = [q.astype(BF16) for q in q_f]
    v_b = [zv_ref[:, hs[h]] for h in heads]
    s_raw = [_dot_nt(q_b[h], k_f[h].astype(BF16)) for h in heads]
    c_old = [c_ref[0, h] for h in heads]
    n_old = [n_ref[0, h] for h in heads]
    qc = [_dot(q_b[h], c_old[h].astype(BF16)) for h in heads]
    s = [s_raw[h] * jnp.exp(logd[h] - m_t[h]) for h in heads]
    w_int = [jnp.exp(inter[h] - m_t[h]) for h in heads]
    num = [_dot(s[h].astype(BF16), v_b[h]) + w_int[h] * qc[h] for h in heads]
    den = [jnp.sum(s[h], axis=-1, keepdims=True) + w_int[h] * jnp.sum(q_f[h] * n_old[h], axis=-1, keepdims=True)
           for h in heads]
    hh = [num[h] * (1.0 / jnp.maximum(jnp.abs(den[h]), jnp.exp(-m_t[h]))) for h in heads]
    for h in heads:
        m_new = m_t[h][L - 1:L, :]
        b_last = b_col[h][L - 1:L, :]
        w_k = jnp.exp(b_last - b_col[h] + ig_col[h] - m_new)
        dec = jnp.exp(b_last + m_prev[h] - m_new)
        k_w = k_f[h] * w_k
        c_ref[0, h] = dec * c_old[h] + _dot_tn(k_w.astype(BF16), v_b[h])
        n_ref[0, h] = dec * n_old[h] + jnp.sum(k_w, axis=0, keepdims=True)
        m_ref[0, h] = jnp.broadcast_to(m_new, (SUBLANES, LANES))
    for h in heads:
        hn = _rms(hh[h], gm_ref[:, hs[h]])
        ya_ref[:, hs[h]] = (hn * _sigmoid(zo_ref[:, hs[h]].astype(F32))).astype(BF16)

    la = _log_sigmoid(_dot(zs.astype(BF16), w2_ref[...]) + ba2_ref[...]) * (1.0 / G_TAU)
    fwd, rex = _decay_products(jnp.exp(la), L)
    nlev = len(fwd) - 1
    qg = zgq_ref[...].astype(F32) * (G_DK ** -0.5)
    kg = zgk_ref[...].astype(F32)
    q_d = qg.astype(BF16)
    k_d = zgk_ref[...]
    qs = [(qg * fwd[j]).astype(BF16) for j in range(nlev)]
    ks = [k_d] + [(kg * rex[j]).astype(BF16) for j in range(1, nlev)]
    q_in = (qg * fwd[nlev]).astype(BF16)
    k_out = (kg * rex[nlev]).astype(BF16)
    dec_all = fwd[nlev][L - 1:L, :]
    gheads = range(G_HEADS)
    ks_ = [slice(h * G_DK, (h + 1) * G_DK) for h in gheads]
    vs_ = [slice(h * G_DV, (h + 1) * G_DV) for h in gheads]
    att = [jnp.where(lv == -1, _dot_nt(q_d[:, ks_[h]], k_d[:, ks_[h]]), 0.0) for h in gheads]
    for j in range(nlev):
        for h in gheads:
            att[h] = jnp.where(lv == j, _dot_nt(qs[j][:, ks_[h]], ks[j][:, ks_[h]]), att[h])
    v_g = [zgv_ref[:, vs_[h]] for h in gheads]
    st_old = [st_ref[0, h] for h in gheads]
    o = [_dot(att[h].astype(BF16), v_g[h]) + _dot_nt(q_in[:, ks_[h]], st_old[h].astype(BF16)) for h in gheads]
    for h in gheads:
        st_ref[0, h] = st_old[h] * dec_all[:, ks_[h]] + _dot_tn(v_g[h], k_out[:, ks_[h]])
    for h in gheads:
        on = _rms(o[h], gg_ref[:, vs_[h]])
        g = zgg_ref[:, vs_[h]].astype(F32)
        yb_ref[:, vs_[h]] = (on * _silu(g)).astype(BF16)


def _mixers(z, zs, conv0, c0, n0, m0, st0, cw, cb, gm, w2, ba2, gg, batch, T, L):
    nc = T // L
    lv = jnp.asarray(_level_matrix(L))
    shift = jnp.asarray(np.stack([np.eye(L, k=-(CONV_W - 1 - w)) for w in range(CONV_W - 1)]), BF16)
    row_blk = lambda width, cidx: pl.BlockSpec((L, width), lambda b, c: (b * nc + c, cidx))
    state = lambda *shape: pl.BlockSpec((1,) + shape, lambda b, c: (b,) + (0,) * len(shape))
    const = lambda *shape: pl.BlockSpec(shape, lambda b, c: (0,) * len(shape))
    n_tok = batch * T
    return pl.pallas_call(
        functools.partial(_mixer_kernel, L=L),
        grid=(batch, nc),
        in_specs=[
            row_blk(2 * M_DIM, 0),
            row_blk(M_DIM, 2),
            row_blk(M_DIM, 3),
            row_blk(G_KDIM, 8),
            row_blk(G_KDIM, 9),
            row_blk(G_VDIM, 5),
            row_blk(G_VDIM, 6),
            pl.BlockSpec((L, LANES), lambda b, c: (b * nc + c, 0)),
            state(SUBLANES, 2 * M_DIM),
            state(M_HEADS, M_HD, M_HD),
            state(M_HEADS, 1, M_HD),
            state(M_HEADS, SUBLANES, LANES),
            state(G_HEADS, G_DV, G_DK),
            const(CONV_W, 2 * M_DIM),
            const(1, 2 * M_DIM),
            const(1, M_DIM),
            const(LANES, G_KDIM),
            const(1, G_KDIM),
            const(1, G_VDIM),
            const(L, L),
            const(CONV_W - 1, L, L),
        ],
        out_specs=[
            pl.BlockSpec((L, M_DIM), lambda b, c: (b * nc + c, 0)),
            pl.BlockSpec((L, G_VDIM), lambda b, c: (b * nc + c, 0)),
            state(SUBLANES, 2 * M_DIM),
            state(M_HEADS, M_HD, M_HD),
            state(M_HEADS, 1, M_HD),
            state(M_HEADS, SUBLANES, LANES),
            state(G_HEADS, G_DV, G_DK),
        ],
        out_shape=[
            jax.ShapeDtypeStruct((n_tok, M_DIM), BF16),
            jax.ShapeDtypeStruct((n_tok, G_VDIM), BF16),
            jax.ShapeDtypeStruct((batch, SUBLANES, 2 * M_DIM), F32),
            jax.ShapeDtypeStruct((batch, M_HEADS, M_HD, M_HD), F32),
            jax.ShapeDtypeStruct((batch, M_HEADS, 1, M_HD), F32),
            jax.ShapeDtypeStruct((batch, M_HEADS, SUBLANES, LANES), F32),
            jax.ShapeDtypeStruct((batch, G_HEADS, G_DV, G_DK), F32),
        ],
        compiler_params=_params("parallel", "arbitrary"),
        name="mixers",
    )(z, z, z, z, z, z, z, zs, conv0, c0, n0, m0, st0, cw, cb, gm, w2, ba2, gg, lv, shift)


def _outproj_kernel(ya_p, yb_p, ua_p, ub_p, x_p, ya_s, yb_s, ua_s, ub_s, x_s,
                    wpa_ref, wpb_ref, wout_ref, gffn_ref, wr_ref, br_ref, tril_ref,
                    x1p_ref, x1s_ref, xn_ref, eop_ref, eos_ref, wop_ref, wos_ref, cnt_ref, *, tm, steps_p):
    i = pl.program_id(0)

    @pl.when(i == 0)
    def _():
        cnt_ref[...] = jnp.zeros_like(cnt_ref)

    def chains(ins, parts, x1_ref, eo_ref, wo_ref):
        ya_ref, yb_ref, ua_ref, ub_ref, x_ref = ins
        a = [_dot(ya_ref[rs, :], wpa_ref[...]) for rs in parts]
        b = [_dot(yb_ref[rs, :], wpb_ref[...]) for rs in parts]
        merged = [_sigmoid(ua_ref[rs, :].astype(F32)) * a_ + _sigmoid(ub_ref[rs, :].astype(F32)) * b_
                  for rs, a_, b_ in zip(parts, a, b)]
        x1 = [x_ref[rs, :] + _dot(m.astype(BF16), wout_ref[...]) for rs, m in zip(parts, merged)]
        xn = []
        for h, (rs, x1_) in enumerate(zip(parts, x1)):
            x1_ref[rs, :] = x1_
            xn_ = _rms(x1_, gffn_ref[...])
            for j in range(ROW_CHUNKS):
                xn_ref[pl.ds(h * tm * ROW_CHUNKS + j, tm, stride=ROW_CHUNKS), :] = xn_[:, j * LANES:(j + 1) * LANES]
            xn.append(xn_)

        ls = [_dot(xn_.astype(BF16), wr_ref[...]) + br_ref[...] for xn_ in xn]
        lane = lax.broadcasted_iota(jnp.int32, (tm, LANES), 1).astype(F32)
        vals = [[] for _ in parts]
        idxs = [[] for _ in parts]
        for _ in range(TOP_K):
            for h in range(len(parts)):
                mx = jnp.max(ls[h], axis=-1, keepdims=True)
                ix = jnp.min(jnp.where(ls[h] == mx, lane, float(LANES)), axis=-1, keepdims=True)
                vals[h].append(mx)
                idxs[h].append(ix)
                ls[h] = jnp.where(lane == ix, -jnp.inf, ls[h])

        carry = cnt_ref[0:1, :]
        for h, rs in enumerate(parts):
            ev = [jnp.exp(v - vals[h][0]) for v in vals[h]]
            den = ev[0] + ev[1] + ev[2] + ev[3]
            cnt = jnp.zeros((tm, LANES), F32)
            for ix in idxs[h]:
                cnt = cnt + jnp.where(lane == ix, 1.0, 0.0)
            pre = _dot(tril_ref[...], cnt.astype(BF16)) + carry
            carry = carry + jnp.sum(cnt, axis=0, keepdims=True)
            eo = jnp.zeros((tm, LANES), F32)
            wo = jnp.zeros((tm, LANES), F32)
            for k in range(TOP_K):
                rank = jnp.sum(jnp.where(lane == idxs[h][k], pre, 0.0), axis=-1, keepdims=True)
                eo = jnp.where(lane == float(k), idxs[h][k], eo)
                eo = jnp.where(lane == float(TOP_K + k), rank, eo)
                wo = jnp.where(lane == float(k), ev[k] / den, wo)
            eo_ref[:, rs] = eo.T[0:SUBLANES, :].astype(jnp.int32)
            wo_ref[rs, :] = wo
        cnt_ref[...] = jnp.broadcast_to(carry, cnt_ref.shape)

    prompt = (ya_p, yb_p, ua_p, ub_p, x_p)
    sample = (ya_s, yb_s, ua_s, ub_s, x_s)

    @pl.when(i < steps_p)
    def _():
        chains(prompt, [slice(0, tm), slice(tm, 2 * tm)], x1p_ref, eop_ref, wop_ref)

    @pl.when(i == steps_p)
    def _():
        chains(sample, [slice(0, tm)], x1s_ref, eos_ref, wos_ref)
        xn_ref[pl.ds(tm * ROW_CHUNKS, tm * ROW_CHUNKS), :] = jnp.zeros((tm * ROW_CHUNKS, LANES), F32)


def _outproj(prompt, sample, wpa, wpb, wout, gffn, wr, br, tm):
    n_p, n_s = prompt[3].shape[0], sample[3].shape[0]
    assert n_s == tm and n_p % (2 * tm) == 0
    steps_p = n_p // (2 * tm)
    tril = jnp.asarray(np.tril(np.ones((tm, tm), np.float32), -1), BF16)
    pidx = lambda i: jnp.minimum(i, steps_p - 1)
    src = lambda rows, idx: [pl.BlockSpec((rows, M_DIM), lambda i: (idx(i), 0)),
                             pl.BlockSpec((rows, G_VDIM), lambda i: (idx(i), 0)),
                             pl.BlockSpec((rows, D_MODEL), lambda i: (idx(i), 7)),
                             pl.BlockSpec((rows, D_MODEL), lambda i: (idx(i), 8)),
                             pl.BlockSpec((rows, D_MODEL), lambda i: (idx(i), 0))]
    const = lambda *shape: pl.BlockSpec(shape, lambda i: (0,) * len(shape))
    ya_p, yb_p, z_p, x_p = prompt
    ya_s, yb_s, z_s, x_s = sample
    return pl.pallas_call(
        functools.partial(_outproj_kernel, tm=tm, steps_p=steps_p),
        grid=(steps_p + 1,),
        in_specs=src(2 * tm, pidx) + src(tm, lambda i: 0) + [
            const(M_DIM, D_MODEL), const(G_VDIM, D_MODEL), const(D_MODEL, D_MODEL), const(1, D_MODEL),
            const(D_MODEL, LANES), const(1, LANES), const(tm, tm),
        ],
        out_specs=[
            pl.BlockSpec((2 * tm, D_MODEL), lambda i: (pidx(i), 0)),
            const(tm, D_MODEL),
            pl.BlockSpec((2 * tm * ROW_CHUNKS, LANES), lambda i: (i, 0)),
            pl.BlockSpec((SUBLANES, 2 * tm), lambda i: (0, pidx(i))),
            const(SUBLANES, tm),
            pl.BlockSpec((2 * tm, LANES), lambda i: (pidx(i), 0)),
            const(tm, LANES),
            const(SUBLANES, LANES),
        ],
        out_shape=[
            jax.ShapeDtypeStruct((n_p, D_MODEL), F32),
            jax.ShapeDtypeStruct((n_s, D_MODEL), F32),
            jax.ShapeDtypeStruct(((n_p + 2 * tm) * ROW_CHUNKS, LANES), F32),
            jax.ShapeDtypeStruct((SUBLANES, n_p), jnp.int32),
            jax.ShapeDtypeStruct((SUBLANES, n_s), jnp.int32),
            jax.ShapeDtypeStruct((n_p, LANES), F32),
            jax.ShapeDtypeStruct((n_s, LANES), F32),
            jax.ShapeDtypeStruct((SUBLANES, LANES), F32),
        ],
        compiler_params=_params("arbitrary"),
        name="outproj_router",
    )(ya_p, yb_p, z_p, z_p, x_p, ya_s, yb_s, z_s, z_s, x_s, wpa, wpb, wout, gffn, wr, br, tril)


def _row_copy(src, src_row, dst, dst_row, sem):
    return pltpu.make_async_copy(
        src.at[pl.ds(pl.multiple_of(src_row * ROW_CHUNKS, ROW_CHUNKS), ROW_CHUNKS)],
        dst.at[pl.ds(pl.multiple_of(dst_row * ROW_CHUNKS, ROW_CHUNKS), ROW_CHUNKS)], sem)


def _dispatch_kernel(pos_ref, fill_ref, xn_ref, xs_ref, zeros_ref, zsem, sem, *, tm, n, n_blocks):
    i = pl.program_id(0)
    base = i * tm

    def issue(r, carry):
        for k in range(TOP_K):
            _row_copy(xn_ref, r, xs_ref, pos_ref[k * n + base + r], sem).start(priority=k % 2)
        return carry

    lax.fori_loop(0, tm, issue, 0)
    for _ in range(TOP_K):
        pltpu.make_async_copy(xn_ref, xs_ref.at[pl.ds(0, tm * ROW_CHUNKS)], sem).wait()

    @pl.when(i == 0)
    def _():
        zeros_ref[...] = jnp.zeros_like(zeros_ref)
        bits = [1 << b for b in range(MOE_ROWS.bit_length() - 1)]

        def zero_copy(off, nrows):
            return pltpu.make_async_copy(
                zeros_ref.at[pl.ds(0, nrows * ROW_CHUNKS)],
                xs_ref.at[pl.ds(pl.multiple_of(off * ROW_CHUNKS, ROW_CHUNKS), nrows * ROW_CHUNKS)], zsem)

        def expert(e, carry):
            npad = fill_ref[N_EXPERTS + e]
            for wait in (False, True):
                off = fill_ref[e]
                for bit in bits:
                    @pl.when((npad & bit) != 0)
                    def _():
                        zero_copy(off, bit).wait() if wait else zero_copy(off, bit).start()
                    off = off + (npad & bit)
            return carry

        lax.fori_loop(0, N_EXPERTS, expert, 0)

        def start_block(b, carry):
            zero_copy(b * MOE_ROWS, MOE_ROWS).start()
            return carry

        def wait_block(b, carry):
            zero_copy(b * MOE_ROWS, MOE_ROWS).wait()
            return carry

        lax.fori_loop(fill_ref[2 * N_EXPERTS], n_blocks, start_block, 0)
        lax.fori_loop(fill_ref[2 * N_EXPERTS], n_blocks, wait_block, 0)


def _dispatch(pos, fill, xn, n, n_blocks, tm):
    return pl.pallas_call(
        functools.partial(_dispatch_kernel, tm=tm, n=n, n_blocks=n_blocks),
        grid_spec=pltpu.PrefetchScalarGridSpec(
            num_scalar_prefetch=2,
            grid=(n // tm,),
            in_specs=[pl.BlockSpec((tm * ROW_CHUNKS, LANES), lambda i, *_: (i, 0))],
            out_specs=pl.BlockSpec(memory_space=pl.ANY),
            scratch_shapes=[pltpu.VMEM((MOE_ROWS * ROW_CHUNKS, LANES), F32), pltpu.SemaphoreType.DMA(()),
                            pltpu.SemaphoreType.DMA(())],
        ),
        out_shape=jax.ShapeDtypeStruct((n_blocks * MOE_ROWS * ROW_CHUNKS, LANES), F32),
        compiler_params=_params("arbitrary"),
        name="dispatch",
    )(pos, fill, xn)


def _expert_kernel(plan_ref, nused_ref, xs_ref, wgu_hbm, bgu_ref, wd_hbm, bd_ref, y_ref,
                   wgu_f32, wd_f32, wgu_bf, wd_bf, wsem, *, rows):
    i = pl.program_id(0)
    active = i < nused_ref[0]
    expert, next_expert, slot = plan_ref[0, i], plan_ref[1, i], plan_ref[2, i]
    first = jnp.logical_or(i == 0, expert != plan_ref[0, jnp.maximum(i - 1, 0)])

    def weight_copies(e, s):
        return (pltpu.make_async_copy(wgu_hbm.at[e], wgu_f32.at[s], wsem.at[0, s]),
                pltpu.make_async_copy(wd_hbm.at[e], wd_f32.at[s], wsem.at[1, s]))

    @pl.when(i == 0)
    def _():
        for c in weight_copies(expert, slot):
            c.start()

    @pl.when(jnp.logical_and(active, first))
    def _():
        for c in weight_copies(expert, slot):
            c.wait()
        wgu_bf[...] = wgu_f32[slot].astype(BF16)
        wd_bf[...] = wd_f32[slot].astype(BF16)

        @pl.when(next_expert < N_EXPERTS)
        def _():
            for c in weight_copies(next_expert, 1 - slot):
                c.start()

    @pl.when(active)
    def _():
        x = jnp.concatenate([xs_ref[pl.ds(j, rows, stride=ROW_CHUNKS), :] for j in range(ROW_CHUNKS)], axis=-1)
        zz = _dot(x.astype(BF16), wgu_bf[...]) + bgu_ref[...]
        g = jnp.minimum(zz[:, :D_FF], SWIGLU_LIMIT)
        u = jnp.clip(zz[:, D_FF:], -SWIGLU_LIMIT, SWIGLU_LIMIT)
        act = g * _sigmoid(SWIGLU_ALPHA * g) * (u + 1.0)
        y = _dot(act.astype(BF16), wd_bf[...]) + bd_ref[...]
        for j in range(ROW_CHUNKS):
            y_ref[pl.ds(j, rows, stride=ROW_CHUNKS), :] = y[:, j * LANES:(j + 1) * LANES]

    @pl.when(jnp.logical_not(active))
    def _():
        y_ref[...] = jnp.zeros_like(y_ref)


def _experts(plan, n_used, xs, wgu, bgu, wd, bd, rows):
    nb = plan.shape[1]
    return pl.pallas_call(
        functools.partial(_expert_kernel, rows=rows),
        grid_spec=pltpu.PrefetchScalarGridSpec(
            num_scalar_prefetch=2,
            grid=(nb,),
            in_specs=[
                pl.BlockSpec((rows * ROW_CHUNKS, LANES), lambda i, plan, nu: (i, 0)),
                pl.BlockSpec(memory_space=pl.ANY),
                pl.BlockSpec((None, 1, 2 * D_FF), lambda i, plan, nu: (plan[0, i], 0, 0)),
                pl.BlockSpec(memory_space=pl.ANY),
                pl.BlockSpec((None, 1, D_MODEL), lambda i, plan, nu: (plan[0, i], 0, 0)),
            ],
            out_specs=pl.BlockSpec((rows * ROW_CHUNKS, LANES), lambda i, plan, nu: (i, 0)),
            scratch_shapes=[pltpu.VMEM((2, D_MODEL, 2 * D_FF), F32), pltpu.VMEM((2, D_FF, D_MODEL), F32),
                            pltpu.VMEM((D_MODEL, 2 * D_FF), BF16), pltpu.VMEM((D_FF, D_MODEL), BF16),
                            pltpu.SemaphoreType.DMA((2, 2))],
        ),
        out_shape=jax.ShapeDtypeStruct(xs.shape, F32),
        compiler_params=_params("arbitrary"),
        name="experts",
    )(plan, n_used, xs, wgu, bgu, wd, bd)


def _combine_kernel(pos_ref, ye_ref, x1_ref, wo_ref, p_ref, wproj_ref, wgate_ref, gple_ref, gfin_ref,
                    y_ref, rows_a, rows_b, sem, *, tm, nt, n_all, tok0):
    i = pl.program_id(0)
    seg = tm * ROW_CHUNKS

    def start_row(tok, r, dst, sem_slot):
        for k in range(TOP_K):
            srow = pl.multiple_of(pos_ref[k * n_all + tok] * ROW_CHUNKS, ROW_CHUNKS)
            pltpu.make_async_copy(ye_ref.at[pl.ds(srow, ROW_CHUNKS)],
                                  dst.at[pl.ds((k * tm + r) * ROW_CHUNKS, ROW_CHUNKS)], sem_slot).start(priority=k % 2)

    def wait_rows(dst, sem_slot):
        for k in range(TOP_K):
            pltpu.make_async_copy(ye_ref.at[pl.ds(0, seg)], dst.at[pl.ds(k * seg, seg)], sem_slot).wait()

    def compute(half, src):
        rs = slice(half * tm, (half + 1) * tm)
        wo = wo_ref[rs, :]
        moe = jnp.zeros((tm, D_MODEL), F32)
        for k in range(TOP_K):
            rows = jnp.concatenate(
                [src[pl.ds(k * seg + j, tm, stride=ROW_CHUNKS), :] for j in range(ROW_CHUNKS)], axis=-1)
            moe = moe + rows * wo[:, k:k + 1]
        x2 = x1_ref[rs, :] + moe
        gate = _sigmoid(_dot(_rms(x2, gple_ref[...]).astype(BF16), wgate_ref[...]))
        x3 = x2 + _dot(p_ref[rs, :].astype(BF16), wproj_ref[...]) * gate
        y_ref[rs, :] = _rms(x3, gfin_ref[...])

    @pl.when(i == 0)
    def _():
        def issue(r, carry):
            start_row(tok0 + r, r, rows_a, sem.at[0])
            return carry
        lax.fori_loop(0, tm, issue, 0)

    wait_rows(rows_a, sem.at[0])
    base_b = tok0 + (2 * i + 1) * tm
    for r in range(tm):
        start_row(base_b + r, r, rows_b, sem.at[1])
    compute(0, rows_a)

    wait_rows(rows_b, sem.at[1])
    base_a = tok0 + jnp.minimum(2 * i + 2, nt - 1) * tm
    for r in range(tm):
        start_row(base_a + r, r, rows_a, sem.at[0])
    compute(1, rows_b)

    @pl.when(i == nt // 2 - 1)
    def _():
        wait_rows(rows_a, sem.at[0])


def _combine(pos_flat, ye, x1, wo, p, wproj, wgate, gple, gfin, tm, tok0, n_all):
    n = p.shape[0]
    nt = n // tm
    assert nt % 2 == 0
    tok = lambda width: pl.BlockSpec((2 * tm, width), lambda i, pos: (i, 0))
    const = lambda *shape: pl.BlockSpec(shape, lambda i, pos: (0,) * len(shape))
    rows_buf = pltpu.VMEM((TOP_K * tm * ROW_CHUNKS, LANES), F32)
    return pl.pallas_call(
        functools.partial(_combine_kernel, tm=tm, nt=nt, n_all=n_all, tok0=tok0),
        grid_spec=pltpu.PrefetchScalarGridSpec(
            num_scalar_prefetch=1,
            grid=(nt // 2,),
            in_specs=[
                pl.BlockSpec(memory_space=pl.ANY),
                tok(D_MODEL), tok(LANES), tok(PLE_DIM),
                const(PLE_DIM, D_MODEL), const(D_MODEL, D_MODEL), const(1, D_MODEL), const(1, D_MODEL),
            ],
            out_specs=tok(D_MODEL),
            scratch_shapes=[rows_buf, rows_buf, pltpu.SemaphoreType.DMA((2,))],
        ),
        out_shape=jax.ShapeDtypeStruct((n, D_MODEL), F32),
        compiler_params=_params("arbitrary"),
        name="combine_ple",
    )(pos_flat, ye, x1, wo, p, wproj, wgate, gple, gfin)


def _tile(n, pref):
    t = pref
    while n % t:
        t //= 2
    return t


def _seg(a, i):
    return a[..., _OFF[i]:_OFF[i + 1]]


def kernel(x_prompt, x_sample, state_conv, state_mlstm_C, state_mlstm_n, state_mlstm_m, state_gla_S, p_prompt, p_sample, g_mix, w_in, b_in, conv_w, conv_b, g_mnorm, w_a2, b_a2, g_gnorm, w_pa, w_pb, w_out, g_ffn, w_router, b_router, w_gate_up, b_gate_up, w_down, b_down, g_ple, w_ple_proj, w_ple_gate, g_final):
    assert g_mix.shape[0] == 1, "single-layer kernel"
    bp, tp, _ = x_prompt.shape
    bs, ts, _ = x_sample.shape

    big = (0, 1, 2, 5, 6, 7, 8, 10, 11, 12)
    w_big = jnp.concatenate([_seg(w_in[0], i) for i in big], axis=-1).astype(BF16)
    b_big = jnp.concatenate([_seg(b_in[0], i) for i in big], axis=-1)[None, :]
    small = (3, 4, 9)
    n_small = sum(SPLITS[i] for i in small)
    w_small = jnp.pad(jnp.concatenate([_seg(w_in[0], i) for i in small], axis=-1),
                      ((0, 0), (0, LANES - n_small))).astype(BF16)
    b_small = jnp.pad(jnp.concatenate([_seg(b_in[0], i) for i in small], axis=-1), (0, LANES - n_small))[None, :]
    w2 = jnp.pad(w_a2[0], ((2 * M_HEADS, LANES - 2 * M_HEADS - G_RANK), (0, 0))).astype(BF16)
    wr = jnp.pad(w_router[0], ((0, 0), (0, LANES - N_EXPERTS))).astype(BF16)
    br = jnp.pad(b_router[0], (0, LANES - N_EXPERTS), constant_values=NEG_BIG)[None, :]
    wpa, wpb, wout = w_pa[0].astype(BF16), w_pb[0].astype(BF16), w_out[0].astype(BF16)
    wproj, wgate = w_ple_proj[0].astype(BF16), w_ple_gate[0].astype(BF16)
    row = lambda a: a.reshape(1, -1)

    def mix_path(x3d, conv0, c0, n0, m0, s0):
        batch, T, _ = x3d.shape
        x = x3d.reshape(batch * T, D_MODEL)
        z, zs = _inproj(x, row(g_mix[0]), w_big, b_big, w_small, b_small, _tile(batch * T, 2048))
        L = min(MIX_CHUNK, T)
        conv0p = jnp.pad(conv0, ((0, 0), (SUBLANES - (CONV_W - 1), 0), (0, 0)))
        m0b = jnp.broadcast_to(m0[:, :, None, None], (batch, M_HEADS, SUBLANES, LANES))
        ya, yb, tail, c1, n1, m1, st1 = _mixers(
            z, zs, conv0p, c0, n0[:, :, None, :], m0b, jnp.swapaxes(s0, -1, -2),
            conv_w[0], row(conv_b[0]), row(g_mnorm[0]), w2, row(b_a2[0]), row(g_gnorm[0]), batch, T, L)
        new_state = (tail[:, SUBLANES - (CONV_W - 1):, :], c1, n1[:, :, 0, :], m1[:, :, 0, 0],
                     jnp.swapaxes(st1, -1, -2))
        return x, z, ya, yb, new_state

    zero_state = (jnp.zeros((bp, CONV_W - 1, 2 * M_DIM), F32), jnp.zeros((bp, M_HEADS, M_HD, M_HD), F32),
                  jnp.zeros((bp, M_HEADS, M_HD), F32), jnp.zeros((bp, M_HEADS), F32),
                  jnp.zeros((bp, G_HEADS, G_DK, G_DV), F32))
    xp, zp, yap, ybp, st_p = mix_path(x_prompt, *zero_state)
    xs_, zs_, yas, ybs, st_s = mix_path(x_sample, state_conv[0], state_mlstm_C[0], state_mlstm_n[0],
                                        state_mlstm_m[0], state_gla_S[0])

    n_p, n_s = xp.shape[0], xs_.shape[0]
    n_all = n_p + n_s
    tm = n_s
    x1p, x1s, xn, eop, eos, wop, wos, cnt = _outproj((yap, ybp, zp, xp), (yas, ybs, zs_, xs_), wpa, wpb, wout,
                                                     row(g_ffn[0]), wr, br, tm)
    eo = jnp.concatenate([eop, eos], axis=1)

    counts = cnt[0, :N_EXPERTS].astype(jnp.int32)
    padded = (counts + MOE_ROWS - 1) // MOE_ROWS * MOE_ROWS
    pend = jnp.cumsum(padded)
    pstart = pend - padded
    nk = n_all * TOP_K
    n_blocks = (nk + N_EXPERTS * (MOE_ROWS - 1) + MOE_ROWS - 1) // MOE_ROWS
    starts = jnp.arange(n_blocks, dtype=jnp.int32) * MOE_ROWS
    block_e = jnp.minimum(jnp.sum((pend[None, :] <= starts[:, None]).astype(jnp.int32), axis=1), N_EXPERTS - 1)
    ids = jnp.arange(N_EXPERTS, dtype=jnp.int32)
    later = jnp.where((ids[None, :] > ids[:, None]) & (counts[None, :] > 0), ids[None, :], N_EXPERTS)
    next_e = jnp.min(later, axis=1)
    group = jnp.cumsum(jnp.concatenate([jnp.zeros((1,), jnp.int32),
                                        (block_e[1:] != block_e[:-1]).astype(jnp.int32)]))
    plan = jnp.stack([block_e, next_e[block_e], group % 2]).astype(jnp.int32)
    n_used = (pend[-1:] // MOE_ROWS).astype(jnp.int32)
    e, rank = eo[:TOP_K], eo[TOP_K:2 * TOP_K]
    onehot = e[..., None] == jnp.arange(N_EXPERTS, dtype=jnp.int32)
    pos = (jnp.sum(jnp.where(onehot, pstart, 0), axis=-1) + rank).reshape(-1)
    fill = jnp.concatenate([pstart + counts, padded - counts, n_used]).astype(jnp.int32)

    tm_d = DISPATCH_ROWS if n_all % DISPATCH_ROWS == 0 else tm
    xs_sorted = _dispatch(pos, fill, xn, n_all, n_blocks, tm_d)
    ye = _experts(plan, n_used, xs_sorted, w_gate_up[0], b_gate_up[0][:, None, :], w_down[0],
                  b_down[0][:, None, :], MOE_ROWS)
    fin = lambda x1, wo, p, tok0, tmc: _combine(pos, ye, x1, wo, p.reshape(-1, PLE_DIM), wproj, wgate,
                                                row(g_ple[0]), row(g_final), tmc, tok0, n_all)
    y_p = fin(x1p, wop, p_prompt[0], 0, tm).reshape(bp, tp, D_MODEL)
    y_s = fin(x1s, wos, p_sample[0], n_p, _tile(n_s // 2, tm)).reshape(bs, ts, D_MODEL)

    lead = lambda a: a[None]
    return (y_p, y_s) + tuple(lead(a) for a in st_p) + tuple(lead(a) for a in st_s)
```

```python
import functools

import numpy as np
import jax
import jax.numpy as jnp
from jax import lax
from jax.experimental import pallas as pl
from jax.experimental.pallas import tpu as pltpu

F32 = jnp.float32
BF16 = jnp.bfloat16

D_MODEL = 1024
EPS = 1e-6
M_HEADS = 4
M_DIM = D_MODEL
M_HD = M_DIM // M_HEADS
CONV_W = 4
G_HEADS = 4
G_KDIM = D_MODEL // 2
G_VDIM = D_MODEL
G_DK = G_KDIM // G_HEADS
G_DV = G_VDIM // G_HEADS
G_RANK = 16
G_TAU = 16.0
N_EXPERTS = 32
TOP_K = 4
D_FF = D_MODEL
SWIGLU_LIMIT = 7.0
SWIGLU_ALPHA = 1.702
PLE_DIM = 256
SPLITS = (M_DIM, M_DIM, M_DIM, M_HEADS, M_HEADS, M_DIM, G_KDIM, G_KDIM, G_VDIM, G_RANK, G_VDIM, D_MODEL, D_MODEL)
_OFF = np.concatenate([[0], np.cumsum(SPLITS)]).tolist()

LANES = 128
SUBLANES = 8
ROW_CHUNKS = D_MODEL // LANES
VMEM_LIMIT = 56 * 1024 * 1024
MIX_CHUNK = 256
MOE_ROWS = 512
DISPATCH_ROWS = 3328
NEG_BIG = -1e30


def _sigmoid(x):
    return 0.5 * jnp.tanh(0.5 * x) + 0.5


def _silu(x):
    h = 0.5 * x
    return h * jnp.tanh(h) + h


def _log_sigmoid(x):
    return jnp.minimum(x, 0.0) - jnp.log(1.0 + jnp.exp(-jnp.abs(x)))


def _rms(x, g):
    return x * lax.rsqrt(jnp.mean(x * x, axis=-1, keepdims=True) + EPS) * g


def _dot(a, b):
    return jnp.dot(a, b, preferred_element_type=F32)


def _dot_nt(a, b):
    return lax.dot_general(a, b, (((1,), (1,)), ((), ())), preferred_element_type=F32)


def _dot_tn(a, b):
    return lax.dot_general(a, b, (((0,), (0,)), ((), ())), preferred_element_type=F32)


def _params(*sem):
    return pltpu.CompilerParams(dimension_semantics=sem, vmem_limit_bytes=VMEM_LIMIT)


def _inproj_kernel(x_ref, g_ref, w_ref, b_ref, ws_ref, bs_ref, z_ref, zs_ref, h_scr):
    @pl.when(pl.program_id(1) == 0)
    def _():
        hb = _rms(x_ref[...], g_ref[...]).astype(BF16)
        h_scr[...] = hb
        zs_ref[...] = _dot(hb, ws_ref[...]) + bs_ref[...]

    z_ref[...] = (_dot(h_scr[...], w_ref[...]) + b_ref[...]).astype(BF16)


def _inproj(x, g, w_big, b_big, w_small, b_small, tm, tn=1024):
    n = x.shape[0]
    cols = w_big.shape[1]
    return pl.pallas_call(
        _inproj_kernel,
        grid=(n // tm, cols // tn),
        in_specs=[
            pl.BlockSpec((tm, D_MODEL), lambda i, j: (i, 0)),
            pl.BlockSpec((1, D_MODEL), lambda i, j: (0, 0)),
            pl.BlockSpec((D_MODEL, tn), lambda i, j: (0, j)),
            pl.BlockSpec((1, tn), lambda i, j: (0, j)),
            pl.BlockSpec((D_MODEL, LANES), lambda i, j: (0, 0)),
            pl.BlockSpec((1, LANES), lambda i, j: (0, 0)),
        ],
        out_specs=[
            pl.BlockSpec((tm, tn), lambda i, j: (i, j)),
            pl.BlockSpec((tm, LANES), lambda i, j: (i, 0)),
        ],
        out_shape=[jax.ShapeDtypeStruct((n, cols), BF16), jax.ShapeDtypeStruct((n, LANES), F32)],
        scratch_shapes=[pltpu.VMEM((tm, D_MODEL), BF16)],
        compiler_params=_params("parallel", "arbitrary"),
        name="inproj",
    )(x, g, w_big, b_big, w_small, b_small)


def _level_matrix(L):
    t = np.arange(L)[:, None]
    s = np.arange(L)[None, :]
    x = np.maximum(t ^ s, 1)
    lv = np.floor(np.log2(x)).astype(np.int32)
    lv = np.where(t == s, -1, lv)
    return np.where(s > t, -2, lv).astype(np.int32)


def _bcast_row(x, period, r):
    L, W = x.shape
    x3 = x.reshape(L // period, period, W)
    return jnp.broadcast_to(x3[:, r:r + 1, :], x3.shape).reshape(L, W)


def _decay_products(ela, L):
    W = ela.shape[1]
    row = lax.broadcasted_iota(jnp.int32, (L, W), 0)
    fwd = [ela]
    rex = [jnp.ones_like(ela)]
    b = 1
    while b < L:
        per = 2 * b
        res = row & (per - 1)
        f, r = fwd[-1], rex[-1]
        if per < SUBLANES:
            mulf = jnp.ones_like(ela)
            mulr = jnp.ones_like(ela)
            for j in range(b, per):
                mulf = jnp.where(res == j, pltpu.roll(f, j - (b - 1), 0), mulf)
            for j in range(0, b):
                mulr = jnp.where(res == j, pltpu.roll(f, L - (per - 1 - j), 0), mulr)
        else:
            mulf = jnp.where(res >= b, _bcast_row(f, per, b - 1), 1.0)
            mulr = jnp.where(res < b, _bcast_row(f, per, per - 1), 1.0)
        fwd.append(f * mulf)
        rex.append(r * mulr)
        b = per
    return fwd, rex


def _mixer_kernel(zqk_ref, zv_ref, zo_ref, zgq_ref, zgk_ref, zgv_ref, zgg_ref, zs_ref,
                  conv0_ref, c0_ref, n0_ref, m0_ref, s0_ref,
                  cw_ref, cb_ref, gm_ref, w2_ref, ba2_ref, gg_ref, lv_ref, shift_ref,
                  ya_ref, yb_ref, tail_ref, c_ref, n_ref, m_ref, st_ref, *, L):
    @pl.when(pl.program_id(1) == 0)
    def _():
        tail_ref[...] = conv0_ref[...]
        c_ref[...] = c0_ref[...]
        n_ref[...] = n0_ref[...]
        m_ref[...] = m0_ref[...]
        st_ref[...] = s0_ref[...]

    lv = lv_ref[...]
    causal = lv >= -1

    u_b = zqk_ref[...]
    u = u_b.astype(F32)
    tail = tail_ref[0]
    row8 = lax.broadcasted_iota(jnp.int32, (SUBLANES, 2 * M_DIM), 0)
    acc = cb_ref[...] + cw_ref[CONV_W - 1:CONV_W, :] * u
    for w in range(CONV_W - 1):
        delay = CONV_W - 1 - w
        shifted = _dot(shift_ref[w], u_b)
        head = jnp.where(row8 < delay, pltpu.roll(tail, delay, 0), 0.0)
        shifted = jnp.concatenate([shifted[0:SUBLANES] + head, shifted[SUBLANES:]], axis=0)
        acc = acc + cw_ref[w:w + 1, :] * shifted
    tail_ref[0] = u[L - SUBLANES:L, :]
    qk = _silu(acc)

    zs = zs_ref[...]
    row = lax.broadcasted_iota(jnp.int32, (L, LANES), 0)
    col = lax.broadcasted_iota(jnp.int32, (L, LANES), 1)
    bcs = _log_sigmoid(zs)
    sh = 1
    while sh < L:
        bcs = bcs + jnp.where(row >= sh, pltpu.roll(bcs, sh, 0), 0.0)
        sh *= 2
    gates = jnp.where(col < M_HEADS, zs, bcs)
    if L < LANES:
        gates = jnp.concatenate([gates, jnp.zeros((LANES - L, LANES), F32)], axis=0)
    gt = gates.T[:, :L]

    heads = range(M_HEADS)
    hs = [slice(h * M_HD, (h + 1) * M_HD) for h in heads]
    ig_col = [zs[:, h:h + 1] for h in heads]
    b_col = [bcs[:, M_HEADS + h:M_HEADS + h + 1] for h in heads]
    m_prev = [m_ref[0, h][0:1, 0:1] for h in heads]
    logd = [jnp.where(causal, b_col[h] - gt[M_HEADS + h:M_HEADS + h + 1, :] + gt[h:h + 1, :], -jnp.inf) for h in heads]
    inter = [b_col[h] + m_prev[h] for h in heads]
    m_t = [jnp.maximum(inter[h], jnp.max(logd[h], axis=-1, keepdims=True)) for h in heads]
    q_f = [qk[:, hs[h]] for h in heads]
    k_f = [qk[:, M_DIM + h * M_HD:M_DIM + (h + 1) * M_HD] * (M_HD ** -0.5) for h in heads]
    q_b = [q.astype(BF16) for q in q_f]
    v_b = [zv_ref[:, hs[h]] for h in heads]
    s_raw = [_dot_nt(q_b[h], k_f[h].astype(BF16)) for h in heads]
    c_old = [c_ref[0, h] for h in heads]
    n_old = [n_ref[0, h] for h in heads]
    qc = [_dot(q_b[h], c_old[h].astype(BF16)) for h in heads]
    s = [s_raw[h] * jnp.exp(logd[h] - m_t[h]) for h in heads]
    w_int = [jnp.exp(inter[h] - m_t[h]) for h in heads]
    num = [_dot(s[h].astype(BF16), v_b[h]) + w_int[h] * qc[h] for h in heads]
    den = [jnp.sum(s[h], axis=-1, keepdims=True) + w_int[h] * jnp.sum(q_f[h] * n_old[h], axis=-1, keepdims=True)
           for h in heads]
    hh = [num[h] * (1.0 / jnp.maximum(jnp.abs(den[h]), jnp.exp(-m_t[h]))) for h in heads]
    for h in heads:
        m_new = m_t[h][L - 1:L, :]
        b_last = b_col[h][L - 1:L, :]
        w_k = jnp.exp(b_last - b_col[h] + ig_col[h] - m_new)
        dec = jnp.exp(b_last + m_prev[h] - m_new)
        k_w = k_f[h] * w_k
        c_ref[0, h] = dec * c_old[h] + _dot_tn(k_w.astype(BF16), v_b[h])
        n_ref[0, h] = dec * n_old[h] + jnp.sum(k_w, axis=0, keepdims=True)
        m_ref[0, h] = jnp.broadcast_to(m_new, (SUBLANES, LANES))
    for h in heads:
        hn = _rms(hh[h], gm_ref[:, hs[h]])
        ya_ref[:, hs[h]] = (hn * _sigmoid(zo_ref[:, hs[h]].astype(F32))).astype(BF16)

    la = _log_sigmoid(_dot(zs.astype(BF16), w2_ref[...]) + ba2_ref[...]) * (1.0 / G_TAU)
    fwd, rex = _decay_products(jnp.exp(la), L)
    nlev = len(fwd) - 1
    qg = zgq_ref[...].astype(F32) * (G_DK ** -0.5)
    kg = zgk_ref[...].astype(F32)
    q_d = qg.astype(BF16)
    k_d = zgk_ref[...]
    qs = [(qg * fwd[j]).astype(BF16) for j in range(nlev)]
    ks = [k_d] + [(kg * rex[j]).astype(BF16) for j in range(1, nlev)]
    q_in = (qg * fwd[nlev]).astype(BF16)
    k_out = (kg * rex[nlev]).astype(BF16)
    dec_all = fwd[nlev][L - 1:L, :]
    gheads = range(G_HEADS)
    ks_ = [slice(h * G_DK, (h + 1) * G_DK) for h in gheads]
    vs_ = [slice(h * G_DV, (h + 1) * G_DV) for h in gheads]
    att = [jnp.where(lv == -1, _dot_nt(q_d[:, ks_[h]], k_d[:, ks_[h]]), 0.0) for h in gheads]
    for j in range(nlev):
        for h in gheads:
            att[h] = jnp.where(lv == j, _dot_nt(qs[j][:, ks_[h]], ks[j][:, ks_[h]]), att[h])
    v_g = [zgv_ref[:, vs_[h]] for h in gheads]
    st_old = [st_ref[0, h] for h in gheads]
    o = [_dot(att[h].astype(BF16), v_g[h]) + _dot_nt(q_in[:, ks_[h]], st_old[h].astype(BF16)) for h in gheads]
    for h in gheads:
        st_ref[0, h] = st_old[h] * dec_all[:, ks_[h]] + _dot_tn(v_g[h], k_out[:, ks_[h]])
    for h in gheads:
        on = _rms(o[h], gg_ref[:, vs_[h]])
        g = zgg_ref[:, vs_[h]].astype(F32)
        yb_ref[:, vs_[h]] = (on * _silu(g)).astype(BF16)


def _mixers(z, zs, conv0, c0, n0, m0, st0, cw, cb, gm, w2, ba2, gg, batch, T, L):
    nc = T // L
    lv = jnp.asarray(_level_matrix(L))
    shift = jnp.asarray(np.stack([np.eye(L, k=-(CONV_W - 1 - w)) for w in range(CONV_W - 1)]), BF16)
    row_blk = lambda width, cidx: pl.BlockSpec((L, width), lambda b, c: (b * nc + c, cidx))
    state = lambda *shape: pl.BlockSpec((1,) + shape, lambda b, c: (b,) + (0,) * len(shape))
    const = lambda *shape: pl.BlockSpec(shape, lambda b, c: (0,) * len(shape))
    n_tok = batch * T
    return pl.pallas_call(
        functools.partial(_mixer_kernel, L=L),
        grid=(batch, nc),
        in_specs=[
            row_blk(2 * M_DIM, 0),
            row_blk(M_DIM, 2),
            row_blk(M_DIM, 3),
            row_blk(G_KDIM, 8),
            row_blk(G_KDIM, 9),
            row_blk(G_VDIM, 5),
            row_blk(G_VDIM, 6),
            pl.BlockSpec((L, LANES), lambda b, c: (b * nc + c, 0)),
            state(SUBLANES, 2 * M_DIM),
            state(M_HEADS, M_HD, M_HD),
            state(M_HEADS, 1, M_HD),
            state(M_HEADS, SUBLANES, LANES),
            state(G_HEADS, G_DV, G_DK),
            const(CONV_W, 2 * M_DIM),
            const(1, 2 * M_DIM),
            const(1, M_DIM),
            const(LANES, G_KDIM),
            const(1, G_KDIM),
            const(1, G_VDIM),
            const(L, L),
            const(CONV_W - 1, L, L),
        ],
        out_specs=[
            pl.BlockSpec((L, M_DIM), lambda b, c: (b * nc + c, 0)),
            pl.BlockSpec((L, G_VDIM), lambda b, c: (b * nc + c, 0)),
            state(SUBLANES, 2 * M_DIM),
            state(M_HEADS, M_HD, M_HD),
            state(M_HEADS, 1, M_HD),
            state(M_HEADS, SUBLANES, LANES),
            state(G_HEADS, G_DV, G_DK),
        ],
        out_shape=[
            jax.ShapeDtypeStruct((n_tok, M_DIM), BF16),
            jax.ShapeDtypeStruct((n_tok, G_VDIM), BF16),
            jax.ShapeDtypeStruct((batch, SUBLANES, 2 * M_DIM), F32),
            jax.ShapeDtypeStruct((batch, M_HEADS, M_HD, M_HD), F32),
            jax.ShapeDtypeStruct((batch, M_HEADS, 1, M_HD), F32),
            jax.ShapeDtypeStruct((batch, M_HEADS, SUBLANES, LANES), F32),
            jax.ShapeDtypeStruct((batch, G_HEADS, G_DV, G_DK), F32),
        ],
        compiler_params=_params("parallel", "arbitrary"),
        name="mixers",
    )(z, z, z, z, z, z, z, zs, conv0, c0, n0, m0, st0, cw, cb, gm, w2, ba2, gg, lv, shift)


def _outproj_kernel(ya_p, yb_p, ua_p, ub_p, x_p, ya_s, yb_s, ua_s, ub_s, x_s,
                    wpa_ref, wpb_ref, wout_ref, gffn_ref, wr_ref, br_ref, tril_ref,
                    x1p_ref, x1s_ref, xn_ref, eop_ref, eos_ref, wop_ref, wos_ref, cnt_ref, *, tm, steps_p):
    i = pl.program_id(0)

    @pl.when(i == 0)
    def _():
        cnt_ref[...] = jnp.zeros_like(cnt_ref)

    def chains(ins, parts, x1_ref, eo_ref, wo_ref):
        ya_ref, yb_ref, ua_ref, ub_ref, x_ref = ins
        a = [_dot(ya_ref[rs, :], wpa_ref[...]) for rs in parts]
        b = [_dot(yb_ref[rs, :], wpb_ref[...]) for rs in parts]
        merged = [_sigmoid(ua_ref[rs, :].astype(F32)) * a_ + _sigmoid(ub_ref[rs, :].astype(F32)) * b_
                  for rs, a_, b_ in zip(parts, a, b)]
        x1 = [x_ref[rs, :] + _dot(m.astype(BF16), wout_ref[...]) for rs, m in zip(parts, merged)]
        xn = []
        for h, (rs, x1_) in enumerate(zip(parts, x1)):
            x1_ref[rs, :] = x1_
            xn_ = _rms(x1_, gffn_ref[...])
            for j in range(ROW_CHUNKS):
                xn_ref[pl.ds(h * tm * ROW_CHUNKS + j, tm, stride=ROW_CHUNKS), :] = xn_[:, j * LANES:(j + 1) * LANES]
            xn.append(xn_)

        ls = [_dot(xn_.astype(BF16), wr_ref[...]) + br_ref[...] for xn_ in xn]
        lane = lax.broadcasted_iota(jnp.int32, (tm, LANES), 1).astype(F32)
        vals = [[] for _ in parts]
        idxs = [[] for _ in parts]
        for _ in range(TOP_K):
            for h in range(len(parts)):
                mx = jnp.max(ls[h], axis=-1, keepdims=True)
                ix = jnp.min(jnp.where(ls[h] == mx, lane, float(LANES)), axis=-1, keepdims=True)
                vals[h].append(mx)
                idxs[h].append(ix)
                ls[h] = jnp.where(lane == ix, -jnp.inf, ls[h])

        carry = cnt_ref[0:1, :]
        for h, rs in enumerate(parts):
            ev = [jnp.exp(v - vals[h][0]) for v in vals[h]]
            den = ev[0] + ev[1] + ev[2] + ev[3]
            cnt = jnp.zeros((tm, LANES), F32)
            for ix in idxs[h]:
                cnt = cnt + jnp.where(lane == ix, 1.0, 0.0)
            pre = _dot(tril_ref[...], cnt.astype(BF16)) + carry
            carry = carry + jnp.sum(cnt, axis=0, keepdims=True)
            eo = jnp.zeros((tm, LANES), F32)
            wo = jnp.zeros((tm, LANES), F32)
            for k in range(TOP_K):
                rank = jnp.sum(jnp.where(lane == idxs[h][k], pre, 0.0), axis=-1, keepdims=True)
                eo = jnp.where(lane == float(k), idxs[h][k], eo)
                eo = jnp.where(lane == float(TOP_K + k), rank, eo)
                wo = jnp.where(lane == float(k), ev[k] / den, wo)
            eo_ref[:, rs] = eo.T[0:SUBLANES, :].astype(jnp.int32)
            wo_ref[rs, :] = wo
        cnt_ref[...] = jnp.broadcast_to(carry, cnt_ref.shape)

    prompt = (ya_p, yb_p, ua_p, ub_p, x_p)
    sample = (ya_s, yb_s, ua_s, ub_s, x_s)

    @pl.when(i < steps_p)
    def _():
        chains(prompt, [slice(0, tm), slice(tm, 2 * tm)], x1p_ref, eop_ref, wop_ref)

    @pl.when(i == steps_p)
    def _():
        chains(sample, [slice(0, tm)], x1s_ref, eos_ref, wos_ref)
        xn_ref[pl.ds(tm * ROW_CHUNKS, tm * ROW_CHUNKS), :] = jnp.zeros((tm * ROW_CHUNKS, LANES), F32)


def _outproj(prompt, sample, wpa, wpb, wout, gffn, wr, br, tm):
    n_p, n_s = prompt[3].shape[0], sample[3].shape[0]
    assert n_s == tm and n_p % (2 * tm) == 0
    steps_p = n_p // (2 * tm)
    tril = jnp.asarray(np.tril(np.ones((tm, tm), np.float32), -1), BF16)
    pidx = lambda i: jnp.minimum(i, steps_p - 1)
    src = lambda rows, idx: [pl.BlockSpec((rows, M_DIM), lambda i: (idx(i), 0)),
                             pl.BlockSpec((rows, G_VDIM), lambda i: (idx(i), 0)),
                             pl.BlockSpec((rows, D_MODEL), lambda i: (idx(i), 7)),
                             pl.BlockSpec((rows, D_MODEL), lambda i: (idx(i), 8)),
                             pl.BlockSpec((rows, D_MODEL), lambda i: (idx(i), 0))]
    const = lambda *shape: pl.BlockSpec(shape, lambda i: (0,) * len(shape))
    ya_p, yb_p, z_p, x_p = prompt
    ya_s, yb_s, z_s, x_s = sample
    return pl.pallas_call(
        functools.partial(_outproj_kernel, tm=tm, steps_p=steps_p),
        grid=(steps_p + 1,),
        in_specs=src(2 * tm, pidx) + src(tm, lambda i: 0) + [
            const(M_DIM, D_MODEL), const(G_VDIM, D_MODEL), const(D_MODEL, D_MODEL), const(1, D_MODEL),
            const(D_MODEL, LANES), const(1, LANES), const(tm, tm),
        ],
        out_specs=[
            pl.BlockSpec((2 * tm, D_MODEL), lambda i: (pidx(i), 0)),
            const(tm, D_MODEL),
            pl.BlockSpec((2 * tm * ROW_CHUNKS, LANES), lambda i: (i, 0)),
            pl.BlockSpec((SUBLANES, 2 * tm), lambda i: (0, pidx(i))),
            const(SUBLANES, tm),
            pl.BlockSpec((2 * tm, LANES), lambda i: (pidx(i), 0)),
            const(tm, LANES),
            const(SUBLANES, LANES),
        ],
        out_shape=[
            jax.ShapeDtypeStruct((n_p, D_MODEL), F32),
            jax.ShapeDtypeStruct((n_s, D_MODEL), F32),
            jax.ShapeDtypeStruct(((n_p + 2 * tm) * ROW_CHUNKS, LANES), F32),
            jax.ShapeDtypeStruct((SUBLANES, n_p), jnp.int32),
            jax.ShapeDtypeStruct((SUBLANES, n_s), jnp.int32),
            jax.ShapeDtypeStruct((n_p, LANES), F32),
            jax.ShapeDtypeStruct((n_s, LANES), F32),
            jax.ShapeDtypeStruct((SUBLANES, LANES), F32),
        ],
        compiler_params=_params("arbitrary"),
        name="outproj_router",
    )(ya_p, yb_p, z_p, z_p, x_p, ya_s, yb_s, z_s, z_s, x_s, wpa, wpb, wout, gffn, wr, br, tril)


def _row_copy(src, src_row, dst, dst_row, sem):
    return pltpu.make_async_copy(
        src.at[pl.ds(pl.multiple_of(src_row * ROW_CHUNKS, ROW_CHUNKS), ROW_CHUNKS)],
        dst.at[pl.ds(pl.multiple_of(dst_row * ROW_CHUNKS, ROW_CHUNKS), ROW_CHUNKS)], sem)


def _dispatch_kernel(pos_ref, fill_ref, xn_ref, xs_ref, zeros_ref, zsem, sem, *, tm, n, n_blocks):
    i = pl.program_id(0)
    base = i * tm

    def issue(r, carry):
        for k in range(TOP_K):
            _row_copy(xn_ref, r, xs_ref, pos_ref[k * n + base + r], sem).start(priority=k % 2)
        return carry

    lax.fori_loop(0, tm, issue, 0)
    for _ in range(TOP_K):
        pltpu.make_async_copy(xn_ref, xs_ref.at[pl.ds(0, tm * ROW_CHUNKS)], sem).wait()

    @pl.when(i == 0)
    def _():
        zeros_ref[...] = jnp.zeros_like(zeros_ref)
        bits = [1 << b for b in range(MOE_ROWS.bit_length() - 1)]

        def zero_copy(off, nrows):
            return pltpu.make_async_copy(
                zeros_ref.at[pl.ds(0, nrows * ROW_CHUNKS)],
                xs_ref.at[pl.ds(pl.multiple_of(off * ROW_CHUNKS, ROW_CHUNKS), nrows * ROW_CHUNKS)], zsem)

        def expert(e, carry):
            npad = fill_ref[N_EXPERTS + e]
            for wait in (False, True):
                off = fill_ref[e]
                for bit in bits:
                    @pl.when((npad & bit) != 0)
                    def _():
                        zero_copy(off, bit).wait() if wait else zero_copy(off, bit).start()
                    off = off + (npad & bit)
            return carry

        lax.fori_loop(0, N_EXPERTS, expert, 0)

        def start_block(b, carry):
            zero_copy(b * MOE_ROWS, MOE_ROWS).start()
            return carry

        def wait_block(b, carry):
            zero_copy(b * MOE_ROWS, MOE_ROWS).wait()
            return carry

        lax.fori_loop(fill_ref[2 * N_EXPERTS], n_blocks, start_block, 0)
        lax.fori_loop(fill_ref[2 * N_EXPERTS], n_blocks, wait_block, 0)


def _dispatch(pos, fill, xn, n, n_blocks, tm):
    return pl.pallas_call(
        functools.partial(_dispatch_kernel, tm=tm, n=n, n_blocks=n_blocks),
        grid_spec=pltpu.PrefetchScalarGridSpec(
            num_scalar_prefetch=2,
            grid=(n // tm,),
            in_specs=[pl.BlockSpec((tm * ROW_CHUNKS, LANES), lambda i, *_: (i, 0))],
            out_specs=pl.BlockSpec(memory_space=pl.ANY),
            scratch_shapes=[pltpu.VMEM((MOE_ROWS * ROW_CHUNKS, LANES), F32), pltpu.SemaphoreType.DMA(()),
                            pltpu.SemaphoreType.DMA(())],
        ),
        out_shape=jax.ShapeDtypeStruct((n_blocks * MOE_ROWS * ROW_CHUNKS, LANES), F32),
        compiler_params=_params("arbitrary"),
        name="dispatch",
    )(pos, fill, xn)


def _expert_kernel(plan_ref, nused_ref, xs_ref, wgu_hbm, bgu_ref, wd_hbm, bd_ref, y_ref,
                   wgu_f32, wd_f32, wgu_bf, wd_bf, wsem, *, rows):
    i = pl.program_id(0)
    active = i < nused_ref[0]
    expert, next_expert, slot = plan_ref[0, i], plan_ref[1, i], plan_ref[2, i]
    first = jnp.logical_or(i == 0, expert != plan_ref[0, jnp.maximum(i - 1, 0)])

    def weight_copies(e, s):
        return (pltpu.make_async_copy(wgu_hbm.at[e], wgu_f32.at[s], wsem.at[0, s]),
                pltpu.make_async_copy(wd_hbm.at[e], wd_f32.at[s], wsem.at[1, s]))

    @pl.when(i == 0)
    def _():
        for c in weight_copies(expert, slot):
            c.start()

    @pl.when(jnp.logical_and(active, first))
    def _():
        for c in weight_copies(expert, slot):
            c.wait()
        wgu_bf[...] = wgu_f32[slot].astype(BF16)
        wd_bf[...] = wd_f32[slot].astype(BF16)

        @pl.when(next_expert < N_EXPERTS)
        def _():
            for c in weight_copies(next_expert, 1 - slot):
                c.start()

    @pl.when(active)
    def _():
        x = jnp.concatenate([xs_ref[pl.ds(j, rows, stride=ROW_CHUNKS), :] for j in range(ROW_CHUNKS)], axis=-1)
        zz = _dot(x.astype(BF16), wgu_bf[...]) + bgu_ref[...]
        g = jnp.minimum(zz[:, :D_FF], SWIGLU_LIMIT)
        u = jnp.clip(zz[:, D_FF:], -SWIGLU_LIMIT, SWIGLU_LIMIT)
        act = g * _sigmoid(SWIGLU_ALPHA * g) * (u + 1.0)
        y = _dot(act.astype(BF16), wd_bf[...]) + bd_ref[...]
        for j in range(ROW_CHUNKS):
            y_ref[pl.ds(j, rows, stride=ROW_CHUNKS), :] = y[:, j * LANES:(j + 1) * LANES]

    @pl.when(jnp.logical_not(active))
    def _():
        y_ref[...] = jnp.zeros_like(y_ref)


def _experts(plan, n_used, xs, wgu, bgu, wd, bd, rows):
    nb = plan.shape[1]
    return pl.pallas_call(
        functools.partial(_expert_kernel, rows=rows),
        grid_spec=pltpu.PrefetchScalarGridSpec(
            num_scalar_prefetch=2,
            grid=(nb,),
            in_specs=[
                pl.BlockSpec((rows * ROW_CHUNKS, LANES), lambda i, plan, nu: (jnp.minimum(i, nu[0] - 1), 0)),
                pl.BlockSpec(memory_space=pl.ANY),
                pl.BlockSpec((None, 1, 2 * D_FF), lambda i, plan, nu: (plan[0, i], 0, 0)),
                pl.BlockSpec(memory_space=pl.ANY),
                pl.BlockSpec((None, 1, D_MODEL), lambda i, plan, nu: (plan[0, i], 0, 0)),
            ],
            out_specs=pl.BlockSpec((rows * ROW_CHUNKS, LANES), lambda i, plan, nu: (i, 0)),
            scratch_shapes=[pltpu.VMEM((2, D_MODEL, 2 * D_FF), F32), pltpu.VMEM((2, D_FF, D_MODEL), F32),
                            pltpu.VMEM((D_MODEL, 2 * D_FF), BF16), pltpu.VMEM((D_FF, D_MODEL), BF16),
                            pltpu.SemaphoreType.DMA((2, 2))],
        ),
        out_shape=jax.ShapeDtypeStruct(xs.shape, F32),
        compiler_params=_params("arbitrary"),
        name="experts",
    )(plan, n_used, xs, wgu, bgu, wd, bd)


def _combine_kernel(pos_ref, ye_ref, x1_ref, wo_ref, p_ref, wproj_ref, wgate_ref, gple_ref, gfin_ref,
                    y_ref, rows_a, rows_b, sem, *, tm, nt, n_all, tok0):
    i = pl.program_id(0)
    seg = tm * ROW_CHUNKS

    def start_row(tok, r, dst, sem_slot):
        for k in range(TOP_K):
            srow = pl.multiple_of(pos_ref[k * n_all + tok] * ROW_CHUNKS, ROW_CHUNKS)
            pltpu.make_async_copy(ye_ref.at[pl.ds(srow, ROW_CHUNKS)],
                                  dst.at[pl.ds((k * tm + r) * ROW_CHUNKS, ROW_CHUNKS)], sem_slot).start(priority=k % 2)

    def wait_rows(dst, sem_slot):
        for k in range(TOP_K):
            pltpu.make_async_copy(ye_ref.at[pl.ds(0, seg)], dst.at[pl.ds(k * seg, seg)], sem_slot).wait()

    def compute(half, src):
        rs = slice(half * tm, (half + 1) * tm)
        wo = wo_ref[rs, :]
        moe = jnp.zeros((tm, D_MODEL), F32)
        for k in range(TOP_K):
            rows = jnp.concatenate(
                [src[pl.ds(k * seg + j, tm, stride=ROW_CHUNKS), :] for j in range(ROW_CHUNKS)], axis=-1)
            moe = moe + rows * wo[:, k:k + 1]
        x2 = x1_ref[rs, :] + moe
        gate = _sigmoid(_dot(_rms(x2, gple_ref[...]).astype(BF16), wgate_ref[...]))
        x3 = x2 + _dot(p_ref[rs, :].astype(BF16), wproj_ref[...]) * gate
        y_ref[rs, :] = _rms(x3, gfin_ref[...])

    @pl.when(i == 0)
    def _():
        def issue(r, carry):
            start_row(tok0 + r, r, rows_a, sem.at[0])
            return carry
        lax.fori_loop(0, tm, issue, 0)

    wait_rows(rows_a, sem.at[0])
    base_b = tok0 + (2 * i + 1) * tm
    for r in range(tm):
        start_row(base_b + r, r, rows_b, sem.at[1])
    compute(0, rows_a)

    wait_rows(rows_b, sem.at[1])
    base_a = tok0 + jnp.minimum(2 * i + 2, nt - 1) * tm
    for r in range(tm):
        start_row(base_a + r, r, rows_a, sem.at[0])
    compute(1, rows_b)

    @pl.when(i == nt // 2 - 1)
    def _():
        wait_rows(rows_a, sem.at[0])


def _combine(pos_flat, ye, x1, wo, p, wproj, wgate, gple, gfin, tm, tok0, n_all):
    n = p.shape[0]
    nt = n // tm
    assert nt % 2 == 0
    tok = lambda width: pl.BlockSpec((2 * tm, width), lambda i, pos: (i, 0))
    const = lambda *shape: pl.BlockSpec(shape, lambda i, pos: (0,) * len(shape))
    rows_buf = pltpu.VMEM((TOP_K * tm * ROW_CHUNKS, LANES), F32)
    return pl.pallas_call(
        functools.partial(_combine_kernel, tm=tm, nt=nt, n_all=n_all, tok0=tok0),
        grid_spec=pltpu.PrefetchScalarGridSpec(
            num_scalar_prefetch=1,
            grid=(nt // 2,),
            in_specs=[
                pl.BlockSpec(memory_space=pl.ANY),
                tok(D_MODEL), tok(LANES), tok(PLE_DIM),
                const(PLE_DIM, D_MODEL), const(D_MODEL, D_MODEL), const(1, D_MODEL), const(1, D_MODEL),
            ],
            out_specs=tok(D_MODEL),
            scratch_shapes=[rows_buf, rows_buf, pltpu.SemaphoreType.DMA((2,))],
        ),
        out_shape=jax.ShapeDtypeStruct((n, D_MODEL), F32),
        compiler_params=_params("arbitrary"),
        name="combine_ple",
    )(pos_flat, ye, x1, wo, p, wproj, wgate, gple, gfin)


def _tile(n, pref):
    t = pref
    while n % t:
        t //= 2
    return t


def _seg(a, i):
    return a[..., _OFF[i]:_OFF[i + 1]]


def kernel(x_prompt, x_sample, state_conv, state_mlstm_C, state_mlstm_n, state_mlstm_m, state_gla_S, p_prompt, p_sample, g_mix, w_in, b_in, conv_w, conv_b, g_mnorm, w_a2, b_a2, g_gnorm, w_pa, w_pb, w_out, g_ffn, w_router, b_router, w_gate_up, b_gate_up, w_down, b_down, g_ple, w_ple_proj, w_ple_gate, g_final):
    assert g_mix.shape[0] == 1, "single-layer kernel"
    bp, tp, _ = x_prompt.shape
    bs, ts, _ = x_sample.shape

    big = (0, 1, 2, 5, 6, 7, 8, 10, 11, 12)
    w_big = jnp.concatenate([_seg(w_in[0], i) for i in big], axis=-1).astype(BF16)
    b_big = jnp.concatenate([_seg(b_in[0], i) for i in big], axis=-1)[None, :]
    small = (3, 4, 9)
    n_small = sum(SPLITS[i] for i in small)
    w_small = jnp.pad(jnp.concatenate([_seg(w_in[0], i) for i in small], axis=-1),
                      ((0, 0), (0, LANES - n_small))).astype(BF16)
    b_small = jnp.pad(jnp.concatenate([_seg(b_in[0], i) for i in small], axis=-1), (0, LANES - n_small))[None, :]
    w2 = jnp.pad(w_a2[0], ((2 * M_HEADS, LANES - 2 * M_HEADS - G_RANK), (0, 0))).astype(BF16)
    wr = jnp.pad(w_router[0], ((0, 0), (0, LANES - N_EXPERTS))).astype(BF16)
    br = jnp.pad(b_router[0], (0, LANES - N_EXPERTS), constant_values=NEG_BIG)[None, :]
    wpa, wpb, wout = w_pa[0].astype(BF16), w_pb[0].astype(BF16), w_out[0].astype(BF16)
    wproj, wgate = w_ple_proj[0].astype(BF16), w_ple_gate[0].astype(BF16)
    row = lambda a: a.reshape(1, -1)

    def mix_path(x3d, conv0, c0, n0, m0, s0):
        batch, T, _ = x3d.shape
        x = x3d.reshape(batch * T, D_MODEL)
        z, zs = _inproj(x, row(g_mix[0]), w_big, b_big, w_small, b_small, _tile(batch * T, 2048))
        L = min(MIX_CHUNK, T)
        conv0p = jnp.pad(conv0, ((0, 0), (SUBLANES - (CONV_W - 1), 0), (0, 0)))
        m0b = jnp.broadcast_to(m0[:, :, None, None], (batch, M_HEADS, SUBLANES, LANES))
        ya, yb, tail, c1, n1, m1, st1 = _mixers(
            z, zs, conv0p, c0, n0[:, :, None, :], m0b, jnp.swapaxes(s0, -1, -2),
            conv_w[0], row(conv_b[0]), row(g_mnorm[0]), w2, row(b_a2[0]), row(g_gnorm[0]), batch, T, L)
        new_state = (tail[:, SUBLANES - (CONV_W - 1):, :], c1, n1[:, :, 0, :], m1[:, :, 0, 0],
                     jnp.swapaxes(st1, -1, -2))
        return x, z, ya, yb, new_state

    zero_state = (jnp.zeros((bp, CONV_W - 1, 2 * M_DIM), F32), jnp.zeros((bp, M_HEADS, M_HD, M_HD), F32),
                  jnp.zeros((bp, M_HEADS, M_HD), F32), jnp.zeros((bp, M_HEADS), F32),
                  jnp.zeros((bp, G_HEADS, G_DK, G_DV), F32))
    xp, zp, yap, ybp, st_p = mix_path(x_prompt, *zero_state)
    xs_, zs_, yas, ybs, st_s = mix_path(x_sample, state_conv[0], state_mlstm_C[0], state_mlstm_n[0],
                                        state_mlstm_m[0], state_gla_S[0])

    n_p, n_s = xp.shape[0], xs_.shape[0]
    n_all = n_p + n_s
    tm = n_s
    x1p, x1s, xn, eop, eos, wop, wos, cnt = _outproj((yap, ybp, zp, xp), (yas, ybs, zs_, xs_), wpa, wpb, wout,
                                                     row(g_ffn[0]), wr, br, tm)
    eo = jnp.concatenate([eop, eos], axis=1)

    counts = cnt[0, :N_EXPERTS].astype(jnp.int32)
    padded = (counts + MOE_ROWS - 1) // MOE_ROWS * MOE_ROWS
    pend = jnp.cumsum(padded)
    pstart = pend - padded
    nk = n_all * TOP_K
    n_blocks = (nk + N_EXPERTS * (MOE_ROWS - 1) + MOE_ROWS - 1) // MOE_ROWS
    starts = jnp.arange(n_blocks, dtype=jnp.int32) * MOE_ROWS
    block_e = jnp.minimum(jnp.sum((pend[None, :] <= starts[:, None]).astype(jnp.int32), axis=1), N_EXPERTS - 1)
    ids = jnp.arange(N_EXPERTS, dtype=jnp.int32)
    later = jnp.where((ids[None, :] > ids[:, None]) & (counts[None, :] > 0), ids[None, :], N_EXPERTS)
    next_e = jnp.min(later, axis=1)
    group = jnp.cumsum(jnp.concatenate([jnp.zeros((1,), jnp.int32),
                                        (block_e[1:] != block_e[:-1]).astype(jnp.int32)]))
    plan = jnp.stack([block_e, next_e[block_e], group % 2]).astype(jnp.int32)
    n_used = (pend[-1:] // MOE_ROWS).astype(jnp.int32)
    e, rank = eo[:TOP_K], eo[TOP_K:2 * TOP_K]
    onehot = e[..., None] == jnp.arange(N_EXPERTS, dtype=jnp.int32)
    pos = (jnp.sum(jnp.where(onehot, pstart, 0), axis=-1) + rank).reshape(-1)
    fill = jnp.concatenate([pstart + counts, padded - counts, n_used]).astype(jnp.int32)

    tm_d = DISPATCH_ROWS if n_all % DISPATCH_ROWS == 0 else tm
    xs_sorted = _dispatch(pos, fill, xn, n_all, n_blocks, tm_d)
    ye = _experts(plan, n_used, xs_sorted, w_gate_up[0], b_gate_up[0][:, None, :], w_down[0],
                  b_down[0][:, None, :], MOE_ROWS)
    fin = lambda x1, wo, p, tok0, tmc: _combine(pos, ye, x1, wo, p.reshape(-1, PLE_DIM), wproj, wgate,
                                                row(g_ple[0]), row(g_final), tmc, tok0, n_all)
    y_p = fin(x1p, wop, p_prompt[0], 0, tm).reshape(bp, tp, D_MODEL)
    y_s = fin(x1s, wos, p_sample[0], n_p, _tile(n_s // 2, tm)).reshape(bs, ts, D_MODEL)

    lead = lambda a: a[None]
    return (y_p, y_s) + tuple(lead(a) for a in st_p) + tuple(lead(a) for a in st_s)
```
